```python
import jax
import jax.numpy as jnp
from jax import lax
import numpy as np

D_MODEL = 1024
BATCH = 4
SEQ = 4096
DEPTH = 1
DEC_BATCH = 128
DEC_SEQ = 8
PAST_LEN = 8192
PAGE_SIZE = 128

N_HEADS = 16
N_KV_HEADS = 4
HEAD_DIM = 64
GROUP = N_HEADS // N_KV_HEADS
ATT_DIM = N_HEADS * HEAD_DIM
KV_DIM = N_KV_HEADS * HEAD_DIM
WINDOW = 128
Q_BLOCK = WINDOW
CONV_DIM = D_MODEL
CONV_WIDTH = 3
IN_DIM = 3 * CONV_DIM + ATT_DIM + 2 * KV_DIM + 2 * D_MODEL
N_EXPERTS = 32
TOP_K = 4
D_FF = D_MODEL
SWIGLU_ALPHA = 1.702
SWIGLU_LIMIT = 7.0
MOE_BLOCK = 128
RMS_EPS = 1e-5

kernel_name = 'hybrid_shortconv_swa_moe_step'


def rmsnorm(x, g):
    xf = x.astype(jnp.float32)
    y = xf * lax.rsqrt(jnp.mean(xf * xf, axis=-1, keepdims=True) + RMS_EPS)
    return (y * g.astype(jnp.float32)).astype(x.dtype)


def alibi_slopes():
    h = jnp.arange(1, N_HEADS + 1, dtype=jnp.float32)
    return jnp.exp2(-8.0 * h / N_HEADS).reshape(N_KV_HEADS, GROUP, 1, 1)


def ada_modulation(c, w_ada, b_ada):
    mod = jax.nn.silu(c) @ w_ada + b_ada
    return jnp.split(mod[:, None, :], 6, axis=-1)


def split_in(z):
    sizes = [CONV_DIM, CONV_DIM, CONV_DIM, ATT_DIM, KV_DIM, KV_DIM, D_MODEL]
    return jnp.split(z, [int(o) for o in np.cumsum(sizes)], axis=-1)


def causal_conv(u_ext, conv_w, T):
    out = u_ext[:, 0:T] * conv_w[0]
    for tap in range(1, CONV_WIDTH):
        out = out + u_ext[:, tap:tap + T] * conv_w[tap]
    return out


def sink_attention(q, k, v, dist, valid, sinks):
    s = jnp.einsum('...qkgd,...skd->...kgqs', q, k).astype(jnp.float32) * (HEAD_DIM ** -0.5)
    s = s - alibi_slopes() * dist.astype(jnp.float32)
    s = jnp.where(valid, s, -jnp.inf)
    sink = sinks.astype(jnp.float32).reshape(N_KV_HEADS, GROUP, 1, 1)
    m = jnp.maximum(jnp.max(s, axis=-1, keepdims=True), sink)
    p = jnp.exp(s - m)
    p = p / (jnp.sum(p, axis=-1, keepdims=True) + jnp.exp(sink - m))
    return jnp.einsum('...kgqs,...skd->...qkgd', p.astype(v.dtype), v)


def prompt_window_attention(q, k, v, sinks):
    B, S = q.shape[0], q.shape[1]
    nb = S // Q_BLOCK
    qb = q.reshape(B, nb, Q_BLOCK, N_KV_HEADS, GROUP, HEAD_DIM)
    kb = k.reshape(B, nb, Q_BLOCK, N_KV_HEADS, HEAD_DIM)
    vb = v.reshape(B, nb, Q_BLOCK, N_KV_HEADS, HEAD_DIM)

    def with_prev(t):
        prev = jnp.concatenate([jnp.zeros_like(t[:, :1]), t[:, :-1]], axis=1)
        return jnp.concatenate([prev, t], axis=2)

    qi = jnp.arange(Q_BLOCK)[:, None]
    kj = jnp.arange(2 * Q_BLOCK)[None, :]
    dist = qi + Q_BLOCK - kj
    key_pos = jnp.arange(nb)[:, None, None] * Q_BLOCK + kj - Q_BLOCK
    valid = (dist >= 0) & (dist < WINDOW) & (key_pos >= 0)
    o = sink_attention(qb, with_prev(kb), with_prev(vb), dist, valid[:, None, None], sinks)
    return o.reshape(B, S, ATT_DIM)


def sample_window_attention(q, k, v, k_buf, v_buf, sinks):
    Bd, T = q.shape[0], q.shape[1]
    wb = k_buf.shape[1]
    keys = jnp.concatenate([k_buf, k], axis=1)
    vals = jnp.concatenate([v_buf, v], axis=1)
    qi = jnp.arange(T)[:, None]
    kj = jnp.arange(wb + T)[None, :]
    dist = qi + wb - kj
    valid = (dist >= 0) & (dist < WINDOW)
    o = sink_attention(q.reshape(Bd, T, N_KV_HEADS, GROUP, HEAD_DIM), keys, vals, dist, valid, sinks)
    return o.reshape(Bd, T, ATT_DIM), keys[:, T:], vals[:, T:]


def clamped_swiglu(u):
    glu = jnp.minimum(u[..., ::2], SWIGLU_LIMIT)
    lin = jnp.clip(u[..., 1::2], -SWIGLU_LIMIT, SWIGLU_LIMIT)
    return glu * jax.nn.sigmoid(SWIGLU_ALPHA * glu) * (lin + 1.0)


def moe(h, w_router, b_router, w_up, b_up, w_down, b_down):
    shp = h.shape
    x = h.reshape(-1, D_MODEL)
    T = x.shape[0]
    logits = (x @ w_router + b_router).astype(jnp.float32)
    top_vals, top_idx = lax.top_k(logits, TOP_K)
    gates = jax.nn.softmax(top_vals, axis=-1)
    A = T * TOP_K
    e = top_idx.reshape(A)
    g = gates.reshape(A)
    tok = jnp.repeat(jnp.arange(T, dtype=jnp.int32), TOP_K)
    order = jnp.argsort(e)
    e_s, g_s, tok_s = e[order], g[order], tok[order]
    counts = jnp.zeros((N_EXPERTS,), jnp.int32).at[e].add(1)
    padded = (counts + MOE_BLOCK - 1) // MOE_BLOCK * MOE_BLOCK
    start = jnp.cumsum(counts) - counts
    pend = jnp.cumsum(padded)
    pstart = pend - padded
    dest = pstart[e_s] + jnp.arange(A, dtype=jnp.int32) - start[e_s]
    n_blocks = -(-(A + N_EXPERTS * (MOE_BLOCK - 1)) // MOE_BLOCK)
    n_slots = n_blocks * MOE_BLOCK
    slot_tok = jnp.full((n_slots,), T, jnp.int32).at[dest].set(tok_s)
    slot_g = jnp.zeros((n_slots,), jnp.float32).at[dest].set(g_s)
    block_e = jnp.searchsorted(pend, jnp.arange(n_blocks, dtype=jnp.int32) * MOE_BLOCK, side='right')
    block_e = jnp.minimum(block_e, N_EXPERTS - 1).astype(jnp.int32)
    x_pad = jnp.concatenate([x, jnp.zeros((1, D_MODEL), x.dtype)], axis=0)
    xs = x_pad[slot_tok].reshape(n_blocks, MOE_BLOCK, D_MODEL)

    def expert_block(args):
        xb, eb = args
        u = xb @ w_up[eb] + b_up[eb]
        return clamped_swiglu(u) @ w_down[eb] + b_down[eb]

    ys = lax.map(expert_block, (xs, block_e)).reshape(n_slots, D_MODEL)
    ys = ys * slot_g[:, None].astype(ys.dtype)
    out = jax.ops.segment_sum(ys, slot_tok, num_segments=T + 1)[:T]
    return out.reshape(shp)


def decoder_layer(x, c, conv_prev, k_buf, v_buf, w_ada, b_ada, norm_mix, norm_ffn, w_in, b_in,
                  conv_w, sinks, w_branch_a, w_branch_b, w_out, b_out, w_router, b_router,
                  w_up, b_up, w_down, b_down):
    B, T = x.shape[0], x.shape[1]
    sh1, sc1, g1, sh2, sc2, g2 = ada_modulation(c, w_ada, b_ada)
    xm = rmsnorm(x, norm_mix) * (1.0 + sc1) + sh1
    z = xm @ w_in + b_in
    c_gate, b_gate, xc, q, k, v, ga, gb = split_in(z)
    u = c_gate * xc
    u_ext = jnp.concatenate([conv_prev.astype(u.dtype), u], axis=1)
    ya = b_gate * causal_conv(u_ext, conv_w, T)
    new_conv = u_ext[:, -(CONV_WIDTH - 1):]
    k4 = k.reshape(B, T, N_KV_HEADS, HEAD_DIM)
    v4 = v.reshape(B, T, N_KV_HEADS, HEAD_DIM)
    if k_buf is None:
        yb = prompt_window_attention(q, k4, v4, sinks)
        keep = min(WINDOW, T)
        new_k, new_v = k4[:, T - keep:], v4[:, T - keep:]
    else:
        yb, new_k, new_v = sample_window_attention(q, k4, v4, k_buf, v_buf, sinks)
    merged = jax.nn.sigmoid(ga) * (ya @ w_branch_a) + jax.nn.sigmoid(gb) * (yb @ w_branch_b)
    h = x + g1 * (merged @ w_out + b_out)
    hm = rmsnorm(h, norm_ffn) * (1.0 + sc2) + sh2
    y = h + g2 * moe(hm, w_router, b_router, w_up, b_up, w_down, b_down)
    return y, new_k, new_v, new_conv


def setup_inputs(seed: int = 0) -> dict:
    key = jax.random.key(seed)
    ks = jax.random.split(key, 26)
    f32 = jnp.float32

    def nrm(k, shape, scale):
        return jax.random.normal(k, shape, f32) * scale

    L = DEPTH
    wb = min(WINDOW, PAST_LEN)
    return {
        'x_prompt': nrm(ks[0], (BATCH, SEQ, D_MODEL), 1.0),
        'x_sample': nrm(ks[1], (DEC_BATCH, DEC_SEQ, D_MODEL), 1.0),
        'cache_k_win': nrm(ks[2], (L, DEC_BATCH, wb, N_KV_HEADS, HEAD_DIM), 1.0),
        'cache_v_win': nrm(ks[3], (L, DEC_BATCH, wb, N_KV_HEADS, HEAD_DIM), 1.0),
        'state_conv': nrm(ks[4], (L, DEC_BATCH, CONV_WIDTH - 1, CONV_DIM), 1.0),
        'c_prompt': nrm(ks[5], (BATCH, D_MODEL), 1.0),
        'c_sample': nrm(ks[6], (DEC_BATCH, D_MODEL), 1.0),
        'w_ada': nrm(ks[7], (L, D_MODEL, 6 * D_MODEL), 0.2 * D_MODEL ** -0.5),
        'b_ada': nrm(ks[8], (L, 6 * D_MODEL), 0.02),
        'norm_mix': 1.0 + nrm(ks[9], (L, D_MODEL), 0.05),
        'norm_ffn': 1.0 + nrm(ks[10], (L, D_MODEL), 0.05),
        'w_in': nrm(ks[11], (L, D_MODEL, IN_DIM), D_MODEL ** -0.5),
        'b_in': nrm(ks[12], (L, IN_DIM), 0.02),
        'conv_w': nrm(ks[13], (L, CONV_WIDTH, CONV_DIM), CONV_WIDTH ** -0.5),
        'sinks': nrm(ks[14], (L, N_HEADS), 0.5),
        'w_branch_a': nrm(ks[15], (L, CONV_DIM, D_MODEL), CONV_DIM ** -0.5),
        'w_branch_b': nrm(ks[16], (L, ATT_DIM, D_MODEL), ATT_DIM ** -0.5),
        'w_out': nrm(ks[17], (L, D_MODEL, D_MODEL), D_MODEL ** -0.5),
        'b_out': nrm(ks[18], (L, D_MODEL), 0.02),
        'w_router': nrm(ks[19], (L, D_MODEL, N_EXPERTS), D_MODEL ** -0.5),
        'b_router': nrm(ks[20], (L, N_EXPERTS), 0.01),
        'w_up': nrm(ks[21], (L, N_EXPERTS, D_MODEL, 2 * D_FF), D_MODEL ** -0.5),
        'b_up': nrm(ks[22], (L, N_EXPERTS, 2 * D_FF), 0.02),
        'w_down': nrm(ks[23], (L, N_EXPERTS, D_FF, D_MODEL), D_FF ** -0.5),
        'b_down': nrm(ks[24], (L, N_EXPERTS, D_MODEL), 0.02),
        'norm_final': 1.0 + nrm(ks[25], (D_MODEL,), 0.05),
    }


def reference(x_prompt, x_sample, cache_k_win, cache_v_win, state_conv, c_prompt, c_sample,
              w_ada, b_ada, norm_mix, norm_ffn, w_in, b_in, conv_w, sinks, w_branch_a, w_branch_b,
              w_out, b_out, w_router, b_router, w_up, b_up, w_down, b_down, norm_final):
    hp, hs = x_prompt, x_sample
    kp, vp, cp, ksm, vsm, csm = [], [], [], [], [], []
    for l in range(DEPTH):
        lw = (w_ada[l], b_ada[l], norm_mix[l], norm_ffn[l], w_in[l], b_in[l], conv_w[l], sinks[l],
              w_branch_a[l], w_branch_b[l], w_out[l], b_out[l], w_router[l], b_router[l],
              w_up[l], b_up[l], w_down[l], b_down[l])
        conv0 = jnp.zeros((hp.shape[0], CONV_WIDTH - 1, CONV_DIM), hp.dtype)
        hp, k_p, v_p, conv_p = decoder_layer(hp, c_prompt, conv0, None, None, *lw)
        hs, k_s, v_s, conv_s = decoder_layer(hs, c_sample, state_conv[l], cache_k_win[l],
                                             cache_v_win[l], *lw)
        kp.append(k_p); vp.append(v_p); cp.append(conv_p)
        ksm.append(k_s); vsm.append(v_s); csm.append(conv_s)
    y_prompt = rmsnorm(hp, norm_final)
    y_sample = rmsnorm(hs, norm_final)
    return (y_prompt, y_sample, jnp.stack(kp), jnp.stack(vp), jnp.stack(cp),
            jnp.stack(ksm), jnp.stack(vsm), jnp.stack(csm))
```

```python
import functools

import jax
import jax.numpy as jnp
from jax import lax
from jax.experimental import pallas as pl
from jax.experimental.pallas import tpu as pltpu

F32 = jnp.float32
BF16 = jnp.bfloat16
HIGHEST = lax.Precision.HIGHEST

N_HEADS = 16
N_KV_HEADS = 4
HEAD_DIM = 64
GROUP = N_HEADS // N_KV_HEADS
KV_DIM = N_KV_HEADS * HEAD_DIM
WINDOW = 128
N_EXPERTS = 32
TOP_K = 4
CONV_WIDTH = 3
SWIGLU_ALPHA = 1.702
SWIGLU_LIMIT = 7.0
RMS_EPS = 1e-5
ALIBI_SLOPES = tuple(2.0 ** (-8.0 * (h + 1) / N_HEADS) for h in range(N_HEADS))

SUBLANES = 8
VMEM_LIMIT_BYTES = 56 * 1024 * 1024
TOKEN_TILE = 512
ROW_DMA_TILE = 256
EXPERT_BLOCK = 256


def _cparams(*sem):
    return pltpu.CompilerParams(dimension_semantics=sem, vmem_limit_bytes=VMEM_LIMIT_BYTES)


def _rmsnorm(x, g):
    ms = jnp.mean(x * x, axis=-1, keepdims=True)
    return x * lax.rsqrt(ms + RMS_EPS) * g


def _ada_body(c_ref, w_ref, b_ref, o_ref):
    c = c_ref[...]
    a = c * jax.nn.sigmoid(c)
    o_ref[...] = jnp.dot(a, w_ref[...], preferred_element_type=F32, precision=HIGHEST) + b_ref[...]


def _ada(c_all, w_ada, b_ada):
    rows, d = c_all.shape
    n = w_ada.shape[1]
    return pl.pallas_call(
        _ada_body,
        grid=(n // d,),
        in_specs=[pl.BlockSpec((rows, d), lambda j: (0, 0)),
                  pl.BlockSpec((d, d), lambda j: (0, j)),
                  pl.BlockSpec((1, d), lambda j: (0, j))],
        out_specs=pl.BlockSpec((rows, d), lambda j: (0, j)),
        out_shape=jax.ShapeDtypeStruct((rows, n), F32),
        compiler_params=_cparams("arbitrary"),
        name="ada_modulation",
    )(c_all, w_ada, b_ada.reshape(1, n))


def _inproj_body(*refs, d, nc, tiles_per_seq):
    carry_mode = tiles_per_seq > 0
    if carry_mode:
        (x_ref, sh_ref, sc_ref, nw_ref, w_ref, b_ref, cw_ref,
         ya_ref, q_ref, k_ref, v_ref, ga_ref, gb_ref, ut_ref, carry_ref) = refs
    else:
        (x_ref, sh_ref, sc_ref, nw_ref, w_ref, b_ref, cw_ref, p1_ref, p2_ref,
         ya_ref, q_ref, k_ref, v_ref, ga_ref, gb_ref, ut_ref) = refs
    tm = x_ref.shape[0]
    xm = (_rmsnorm(x_ref[...], nw_ref[...]) * (1.0 + sc_ref[...]) + sh_ref[...]).astype(BF16)

    def proj(lo, width):
        return (jnp.dot(xm, w_ref[:, lo:lo + width], preferred_element_type=F32)
                + b_ref[:, lo:lo + width])

    if carry_mode:
        @pl.when(pl.program_id(0) % tiles_per_seq == 0)
        def _():
            carry_ref[...] = jnp.zeros_like(carry_ref)
    else:
        t_in_seq = lax.broadcasted_iota(jnp.int32, (tm, nc), 0) % SUBLANES

    for j in range(d // nc):
        c0 = j * nc
        u = proj(c0, nc) * proj(2 * d + c0, nc)
        if carry_mode:
            ue = jnp.concatenate([carry_ref[:, c0:c0 + nc], u], axis=0)
            s1 = pltpu.roll(ue, 1, axis=0)[SUBLANES:]
            s2 = pltpu.roll(ue, 2, axis=0)[SUBLANES:]
            carry_ref[:, c0:c0 + nc] = u[tm - SUBLANES:]
            ut_ref[:, c0:c0 + nc] = u[tm - SUBLANES:]
        else:
            s1 = jnp.where(t_in_seq >= 1, pltpu.roll(u, 1, axis=0), p1_ref[:, c0:c0 + nc])
            s2 = jnp.where(t_in_seq >= 2, pltpu.roll(u, 2, axis=0), p2_ref[:, c0:c0 + nc])
            ut_ref[:, c0:c0 + nc] = u
        cw = cw_ref[:, c0:c0 + nc]
        conv = s2 * cw[0:1] + s1 * cw[1:2] + u * cw[2:3]
        ya_ref[:, c0:c0 + nc] = (proj(d + c0, nc) * conv).astype(BF16)

    q_ref[...] = proj(3 * d, d).astype(BF16)
    k_ref[...] = proj(4 * d, KV_DIM)
    v_ref[...] = proj(4 * d + KV_DIM, KV_DIM)
    ga_ref[...] = jax.nn.sigmoid(proj(4 * d + 2 * KV_DIM, d)).astype(BF16)
    gb_ref[...] = jax.nn.sigmoid(proj(5 * d + 2 * KV_DIM, d)).astype(BF16)


def _inproj(x, shift, scale, shift_scale_specs, norm_w, w_in, b_in, conv_w, *, tm, tiles_per_seq,
            hist=None):
    t, d = x.shape
    n_in = w_in.shape[1]
    nc = 512
    carry_mode = tiles_per_seq > 0
    row = lambda width: pl.BlockSpec((tm, width), lambda i: (i, 0))
    full = lambda a: pl.BlockSpec(a.shape, lambda i: (0,) * a.ndim)
    in_specs = [row(d), shift_scale_specs[0], shift_scale_specs[1], full(norm_w), full(w_in),
                full(b_in), full(conv_w)]
    args = [x, shift, scale, norm_w, w_in, b_in, conv_w]
    if not carry_mode:
        in_specs += [row(d), row(d)]
        args += list(hist)
    ut_rows = SUBLANES if carry_mode else tm
    out_specs = [row(d), row(d), row(KV_DIM), row(KV_DIM), row(d), row(d),
                 pl.BlockSpec((ut_rows, d), lambda i: (i, 0))]
    out_shape = [jax.ShapeDtypeStruct((t, d), BF16), jax.ShapeDtypeStruct((t, d), BF16),
                 jax.ShapeDtypeStruct((t, KV_DIM), F32), jax.ShapeDtypeStruct((t, KV_DIM), F32),
                 jax.ShapeDtypeStruct((t, d), BF16), jax.ShapeDtypeStruct((t, d), BF16),
                 jax.ShapeDtypeStruct((t // tm * ut_rows, d), F32)]
    return pl.pallas_call(
        functools.partial(_inproj_body, d=d, nc=nc, tiles_per_seq=tiles_per_seq),
        grid=(t // tm,),
        in_specs=in_specs,
        out_specs=out_specs,
        out_shape=out_shape,
        scratch_shapes=[pltpu.VMEM((SUBLANES, d), F32)] if carry_mode else [],
        compiler_params=_cparams("arbitrary"),
        name="inproj_conv_long" if carry_mode else "inproj_conv_short",
    )(*args)


def _softmax_sink_pv(s, sink, v):
    m = jnp.maximum(jnp.max(s, axis=-1, keepdims=True), sink)
    p = jnp.exp(s - m)
    den = jnp.sum(p, axis=-1, keepdims=True) + jnp.exp(sink - m)
    return jnp.dot((p / den).astype(BF16), v, preferred_element_type=F32)


def _attn_prompt_body(sink_ref, q_ref, kc_ref, kp_ref, vc_ref, vp_ref, o_ref):
    blk = q_ref.shape[0]
    j = pl.program_id(1)
    kk = jnp.concatenate([kp_ref[...], kc_ref[...]], axis=0).astype(BF16)
    vv = jnp.concatenate([vp_ref[...], vc_ref[...]], axis=0).astype(BF16)
    qi = lax.broadcasted_iota(jnp.int32, (blk, 2 * blk), 0)
    kj = lax.broadcasted_iota(jnp.int32, (blk, 2 * blk), 1)
    dist = qi + blk - kj
    valid = (dist >= 0) & (dist < WINDOW) & ((kj >= blk) | (j > 0))
    distf = dist.astype(F32)
    outs = []
    for h in range(N_HEADS):
        kv = h // GROUP
        qh = q_ref[:, h * HEAD_DIM:(h + 1) * HEAD_DIM]
        kh = kk[:, kv * HEAD_DIM:(kv + 1) * HEAD_DIM]
        vh = vv[:, kv * HEAD_DIM:(kv + 1) * HEAD_DIM]
        s = lax.dot_general(qh, kh, (((1,), (1,)), ((), ())), preferred_element_type=F32)
        s = s * (HEAD_DIM ** -0.5) - ALIBI_SLOPES[h] * distf
        s = jnp.where(valid, s, -jnp.inf)
        outs.append(_softmax_sink_pv(s, sink_ref[h], vh))
    o_ref[...] = jnp.concatenate(outs, axis=1).astype(BF16)


def _attn_prompt(q, k, v, sinks, *, batch, seq):
    t, d = q.shape
    blk = WINDOW
    nb = seq // blk
    cur = lambda width: pl.BlockSpec((blk, width), lambda b, j: (b * nb + j, 0))
    prev = lambda width: pl.BlockSpec((blk, width), lambda b, j: (b * nb + jnp.maximum(j - 1, 0), 0))
    return pl.pallas_call(
        _attn_prompt_body,
        grid=(batch, nb),
        in_specs=[pl.BlockSpec(memory_space=pltpu.SMEM),
                  cur(d), cur(KV_DIM), prev(KV_DIM), cur(KV_DIM), prev(KV_DIM)],
        out_specs=cur(d),
        out_shape=jax.ShapeDtypeStruct((t, d), BF16),
        compiler_params=_cparams("arbitrary", "arbitrary"),
        name="attn_window_long",
    )(sinks, q, k, k, v, v)


def _attn_sample_body(sink_ref, q_ref, kn_ref, vn_ref, kb_ref, vb_ref, o_ref, nk_ref, nv_ref):
    bs, tq, _ = q_ref.shape
    wb = kb_ref.shape[1]
    rows = GROUP * tq
    ri = lax.broadcasted_iota(jnp.int32, (rows, wb + tq), 0)
    kj = lax.broadcasted_iota(jnp.int32, (rows, wb + tq), 1)
    dist = ri % tq + wb - kj
    valid = (dist >= 0) & (dist < WINDOW)
    distf = dist.astype(F32)
    g_col = lax.broadcasted_iota(jnp.int32, (rows, 1), 0) // tq
    for b in range(bs):
        keys = jnp.concatenate([kb_ref[b], kn_ref[b]], axis=0)
        vals = jnp.concatenate([vb_ref[b], vn_ref[b]], axis=0)
        nk_ref[b] = keys[tq:]
        nv_ref[b] = vals[tq:]
        keys = keys.astype(BF16)
        vals = vals.astype(BF16)
        qb = q_ref[b]
        outs = []
        for kv in range(N_KV_HEADS):
            h0 = kv * GROUP
            qh = jnp.concatenate(
                [qb[:, (h0 + g) * HEAD_DIM:(h0 + g + 1) * HEAD_DIM] for g in range(GROUP)], axis=0)
            kh = keys[:, kv * HEAD_DIM:(kv + 1) * HEAD_DIM]
            vh = vals[:, kv * HEAD_DIM:(kv + 1) * HEAD_DIM]
            slope = jnp.full((rows, 1), ALIBI_SLOPES[h0 + GROUP - 1], F32)
            sink = jnp.full((rows, 1), sink_ref[h0 + GROUP - 1], F32)
            for g in range(GROUP - 2, -1, -1):
                slope = jnp.where(g_col == g, ALIBI_SLOPES[h0 + g], slope)
                sink = jnp.where(g_col == g, sink_ref[h0 + g], sink)
            s = lax.dot_general(qh, kh, (((1,), (1,)), ((), ())), preferred_element_type=F32)
            s = s * (HEAD_DIM ** -0.5) - slope * distf
            s = jnp.where(valid, s, -jnp.inf)
            o = _softmax_sink_pv(s, sink, vh)
            outs += [o[g * tq:(g + 1) * tq] for g in range(GROUP)]
        o_ref[b] = jnp.concatenate(outs, axis=1).astype(BF16)


def _attn_sample(q, k_new, v_new, k_buf, v_buf, sinks, *, bs):
    nseq, tq, d = q.shape
    wb = k_buf.shape[1]
    blk3 = lambda a, b_: pl.BlockSpec((bs, a, b_), lambda i: (i, 0, 0))
    return pl.pallas_call(
        _attn_sample_body,
        grid=(nseq // bs,),
        in_specs=[pl.BlockSpec(memory_space=pltpu.SMEM),
                  blk3(tq, d), blk3(tq, KV_DIM), blk3(tq, KV_DIM), blk3(wb, KV_DIM), blk3(wb, KV_DIM)],
        out_specs=[blk3(tq, d), blk3(wb, KV_DIM), blk3(wb, KV_DIM)],
        out_shape=[jax.ShapeDtypeStruct((nseq, tq, d), BF16),
                   jax.ShapeDtypeStruct((nseq, wb, KV_DIM), F32),
                   jax.ShapeDtypeStruct((nseq, wb, KV_DIM), F32)],
        compiler_params=_cparams("arbitrary"),
        name="attn_window_short",
    )(sinks, q, k_new, v_new, k_buf, v_buf)


N_POST_ACTS = 8


def _post_body(*refs, n_first):
    first, second = refs[:N_POST_ACTS], refs[N_POST_ACTS:2 * N_POST_ACTS]
    rest = refs[2 * N_POST_ACTS:]
    i = pl.program_id(0)

    @pl.when(i < n_first)
    def _():
        _post_tile(*first, *rest)

    @pl.when(i >= n_first)
    def _():
        _post_tile(*second, *rest)


def _post_tile(x_ref, ya_ref, yb_ref, ga_ref, gb_ref, g1_ref, sh2_ref, sc2_ref, wa_ref, wb_ref,
               wo_ref, bo_ref, nf_ref, wr_ref, br_ref, h_ref, hm_ref, gm_ref, km_ref):
    a = jnp.dot(ya_ref[...], wa_ref[...], preferred_element_type=F32)
    b = jnp.dot(yb_ref[...], wb_ref[...], preferred_element_type=F32)
    merged = ga_ref[...].astype(F32) * a + gb_ref[...].astype(F32) * b
    o = jnp.dot(merged.astype(BF16), wo_ref[...], preferred_element_type=F32) + bo_ref[...]
    h = x_ref[...] + g1_ref[...] * o
    h_ref[...] = h
    hm = _rmsnorm(h, nf_ref[...]) * (1.0 + sc2_ref[...]) + sh2_ref[...]
    hm_ref[...] = hm
    logits = jnp.dot(hm, wr_ref[...], preferred_element_type=F32, precision=HIGHEST) + br_ref[...]
    tm, ne = logits.shape
    lane = lax.broadcasted_iota(jnp.int32, (tm, ne), 1)
    vals = logits
    km = jnp.zeros((tm, ne), F32)
    top = []
    for k in range(TOP_K):
        m = jnp.max(vals, axis=-1, keepdims=True)
        first = jnp.min(jnp.where(vals == m, lane, ne), axis=-1, keepdims=True)
        sel = lane == first
        km = jnp.where(sel, float(k + 1), km)
        vals = jnp.where(sel, -jnp.inf, vals)
        top.append(m)
    e = [jnp.exp(t - top[0]) for t in top]
    den = e[0] + e[1] + e[2] + e[3]
    gm = jnp.zeros((tm, ne), F32)
    for k in range(TOP_K):
        gm = jnp.where(km == float(k + 1), e[k] / den, gm)
    gm_ref[...] = gm
    km_ref[...] = km


def _post(first, first_mod_specs, second, second_mod_specs, wa, wb, wo, bo, nf, wr, br, *, tm):
    t1, d = first[0].shape
    t2 = second[0].shape[0]
    n1, n2 = t1 // tm, t2 // tm
    ne = wr.shape[1]
    row1 = pl.BlockSpec((tm, d), lambda i: (jnp.minimum(i, n1 - 1), 0))
    row2 = pl.BlockSpec((tm, d), lambda i: (jnp.maximum(i - n1, 0), 0))
    orow = lambda width: pl.BlockSpec((tm, width), lambda i: (i, 0))
    full = lambda a: pl.BlockSpec(a.shape, lambda i: (0,) * a.ndim)
    weights = [wa, wb, wo, bo, nf, wr, br]
    in_specs = ([row1] * 5 + list(first_mod_specs) + [row2] * 5 + list(second_mod_specs)
                + [full(w) for w in weights])
    return pl.pallas_call(
        functools.partial(_post_body, n_first=n1),
        grid=(n1 + n2,),
        in_specs=in_specs,
        out_specs=[orow(d), orow(d), orow(ne), orow(ne)],
        out_shape=[jax.ShapeDtypeStruct((t1 + t2, d), F32), jax.ShapeDtypeStruct((t1 + t2, d), F32),
                   jax.ShapeDtypeStruct((t1 + t2, ne), F32), jax.ShapeDtypeStruct((t1 + t2, ne), F32)],
        compiler_params=_cparams("arbitrary"),
        name="merge_outproj_router",
    )(*first, *second, *weights)


def _route_body(km_ref, dest_ref, cnt_ref, run_ref, start_ref, *, block):
    ph = pl.program_id(0)
    i = pl.program_id(1)
    km = km_ref[...]
    tm, ne = km.shape
    chosen = (km > 0.0).astype(F32)
    tile_cnt = jnp.sum(chosen, axis=0, keepdims=True)

    @pl.when((ph == 0) & (i == 0))
    def _():
        run_ref[...] = jnp.zeros_like(run_ref)

    @pl.when(ph == 0)
    def _():
        run_ref[...] += tile_cnt

    @pl.when((ph == 1) & (i == 0))
    def _():
        cnt = run_ref[...]
        cnt_ref[...] = cnt
        padded = jnp.ceil(cnt / block) * block
        r = lax.broadcasted_iota(jnp.int32, (ne, ne), 0)
        c = lax.broadcasted_iota(jnp.int32, (ne, ne), 1)
        before = (r < c).astype(F32)
        start_ref[...] = jnp.dot(padded, before, preferred_element_type=F32, precision=HIGHEST)
        run_ref[...] = jnp.zeros_like(run_ref)

    @pl.when(ph == 1)
    def _():
        r = lax.broadcasted_iota(jnp.int32, (tm, tm), 0)
        c = lax.broadcasted_iota(jnp.int32, (tm, tm), 1)
        earlier = (c < r).astype(BF16)
        rank = jnp.dot(earlier, chosen.astype(BF16), preferred_element_type=F32)
        slot = start_ref[...] + run_ref[...] + rank
        ones = jnp.ones((SUBLANES, ne), F32)
        for k in range(TOP_K):
            sk = jnp.where(km == float(k + 1), slot, 0.0)
            rowk = lax.dot_general(ones, sk, (((1,), (1,)), ((), ())),
                                   preferred_element_type=F32, precision=HIGHEST)
            dest_ref[k:k + 1, :] = rowk[0:1].astype(jnp.int32)
        run_ref[...] += tile_cnt


def _route(km, *, tm, block):
    t, ne = km.shape
    return pl.pallas_call(
        functools.partial(_route_body, block=block),
        grid=(2, t // tm),
        in_specs=[pl.BlockSpec((tm, ne), lambda p, i: (i, 0))],
        out_specs=[pl.BlockSpec((TOP_K, tm), lambda p, i: (0, i * p)),
                   pl.BlockSpec((1, ne), lambda p, i: (0, 0))],
        out_shape=[jax.ShapeDtypeStruct((TOP_K, t), jnp.int32),
                   jax.ShapeDtypeStruct((1, ne), F32)],
        scratch_shapes=[pltpu.VMEM((1, ne), F32), pltpu.VMEM((1, ne), F32)],
        compiler_params=_cparams("arbitrary", "arbitrary"),
        name="route_slots",
    )(km)


def _row_copy(src, src_row, dst, dst_row, sem):
    return pltpu.make_async_copy(src.at[pl.ds(src_row, 1), :], dst.at[pl.ds(dst_row, 1), :], sem)


def _dispatch_body(dest_ref, hm_ref, xs_in_ref, xs_ref, sem):
    del xs_in_ref
    tm = hm_ref.shape[0]

    def issue(t, carry):
        for k in range(TOP_K):
            _row_copy(hm_ref, t, xs_ref, dest_ref[k, t], sem).start()
        return carry

    lax.fori_loop(0, tm, issue, 0)

    def drain(t, carry):
        for k in range(TOP_K):
            _row_copy(hm_ref, t, xs_ref, 0, sem).wait()
        return carry

    lax.fori_loop(0, tm, drain, 0)


def _dispatch(dest, hm, xs_init, *, tm):
    t, d = hm.shape
    return pl.pallas_call(
        _dispatch_body,
        grid=(t // tm,),
        in_specs=[pl.BlockSpec((TOP_K, tm), lambda i: (0, i), memory_space=pltpu.SMEM),
                  pl.BlockSpec((tm, d), lambda i: (i, 0)),
                  pl.BlockSpec(memory_space=pl.ANY)],
        out_specs=pl.BlockSpec(memory_space=pl.ANY),
        out_shape=jax.ShapeDtypeStruct(xs_init.shape, xs_init.dtype),
        scratch_shapes=[pltpu.SemaphoreType.DMA],
        input_output_aliases={2: 0},
        compiler_params=_cparams("arbitrary"),
        name="moe_dispatch_rows",
    )(dest, hm, xs_init)


def _expert_body(be_ref, nb_ref, xs_ref, wg_ref, wl_ref, bg_ref, bl_ref, wd_ref, bd_ref, ys_ref):
    del be_ref
    i = pl.program_id(0)

    @pl.when(i < nb_ref[0])
    def _():
        x = xs_ref[...].astype(BF16)
        glu = jnp.dot(x, wg_ref[...], preferred_element_type=F32) + bg_ref[...]
        lin = jnp.dot(x, wl_ref[...], preferred_element_type=F32) + bl_ref[...]
        glu = jnp.minimum(glu, SWIGLU_LIMIT)
        lin = jnp.clip(lin, -SWIGLU_LIMIT, SWIGLU_LIMIT)
        act = glu * jax.nn.sigmoid(SWIGLU_ALPHA * glu) * (lin + 1.0)
        ys_ref[...] = (jnp.dot(act.astype(BF16), wd_ref[...], preferred_element_type=F32)
                       + bd_ref[...])

    @pl.when(i >= nb_ref[0])
    def _():
        ys_ref[...] = jnp.zeros_like(ys_ref)


def _experts(block_e, n_used, xs, wg, wl, bg, bl, wd, bd, *, block):
    n_slots, d = xs.shape
    f = wg.shape[2]
    n_blocks = n_slots // block
    xrow = pl.BlockSpec((block, d), lambda i, be, nb: (jnp.minimum(i, nb[0] - 1), 0))
    wspec = lambda a, b_: pl.BlockSpec((None, a, b_), lambda i, be, nb: (be[i], 0, 0))
    return pl.pallas_call(
        _expert_body,
        grid_spec=pltpu.PrefetchScalarGridSpec(
            num_scalar_prefetch=2,
            grid=(n_blocks,),
            in_specs=[xrow, wspec(d, f), wspec(d, f), wspec(1, f), wspec(1, f), wspec(f, d),
                      wspec(1, d)],
            out_specs=pl.BlockSpec((block, d), lambda i, be, nb: (i, 0)),
        ),
        out_shape=jax.ShapeDtypeStruct((n_slots, d), F32),
        compiler_params=_cparams("arbitrary"),
        name="moe_expert_mlp",
    )(block_e, n_used, xs, wg, wl, bg, bl, wd, bd)


def _final_body(dest_ref, h_ref, g2_ref, gm_ref, km_ref, nw_ref, ys_ref, o_ref, buf_ref, sem):
    tm = h_ref.shape[0]

    def issue(t, carry):
        for k in range(TOP_K):
            _row_copy(ys_ref, dest_ref[k, t], buf_ref.at[k], t, sem).start()
        return carry

    lax.fori_loop(0, tm, issue, 0)

    def drain(t, carry):
        for k in range(TOP_K):
            _row_copy(ys_ref, 0, buf_ref.at[k], t, sem).wait()
        return carry

    lax.fori_loop(0, tm, drain, 0)

    km = km_ref[...]
    gm = gm_ref[...]
    acc = jnp.zeros(h_ref.shape, F32)
    for k in range(TOP_K):
        gate = jnp.sum(jnp.where(km == float(k + 1), gm, 0.0), axis=-1, keepdims=True)
        acc = acc + gate * buf_ref[k]
    y = h_ref[...] + g2_ref[...] * acc
    o_ref[...] = _rmsnorm(y, nw_ref[...])


def _final(dest, h, g2, g2_spec, gm, km, norm_w, ys, *, tm, tile0, n_tiles):
    d = h.shape[1]
    ne = gm.shape[1]
    row = lambda width: pl.BlockSpec((tm, width), lambda i: (i + tile0, 0))
    return pl.pallas_call(
        _final_body,
        grid=(n_tiles,),
        in_specs=[pl.BlockSpec((TOP_K, tm), lambda i: (0, i + tile0), memory_space=pltpu.SMEM),
                  row(d), g2_spec, row(ne), row(ne),
                  pl.BlockSpec(norm_w.shape, lambda i: (0, 0)),
                  pl.BlockSpec(memory_space=pl.ANY)],
        out_specs=pl.BlockSpec((tm, d), lambda i: (i, 0)),
        out_shape=jax.ShapeDtypeStruct((n_tiles * tm, d), F32),
        scratch_shapes=[pltpu.VMEM((TOP_K, tm, d), F32), pltpu.SemaphoreType.DMA],
        compiler_params=_cparams("arbitrary"),
        name="moe_combine_final_norm",
    )(dest, h, g2, gm, km, norm_w, ys)


def kernel(x_prompt, x_sample, cache_k_win, cache_v_win, state_conv, c_prompt, c_sample, w_ada, b_ada, norm_mix, norm_ffn, w_in, b_in, conv_w, sinks, w_branch_a, w_branch_b, w_out, b_out, w_router, b_router, w_up, b_up, w_down, b_down, norm_final):
    batch, seq, d = x_prompt.shape
    nseq, tq, _ = x_sample.shape
    depth = w_ada.shape[0]
    assert depth == 1 and tq == SUBLANES and seq % TOKEN_TILE == 0
    wb = cache_k_win.shape[2]
    tp, ts = batch * seq, nseq * tq
    t_all = tp + ts
    tm_s = min(ts, TOKEN_TILE)
    assert ts % tm_s == 0 and tp % tm_s == 0 and t_all % ROW_DMA_TILE == 0
    nmod = w_ada.shape[2] // d

    n_c = batch + nseq
    c_rows = -(-n_c // SUBLANES) * SUBLANES
    c_all = jnp.concatenate([c_prompt, c_sample, jnp.zeros((c_rows - n_c, d), F32)], axis=0)
    mod = _ada(c_all, w_ada[0], b_ada[0])
    mod_p = mod[:batch].reshape(batch * nmod, 1, d)
    mod_s = jnp.repeat(mod[batch:n_c], tq, axis=0)
    tiles_per_seq = seq // TOKEN_TILE
    p_spec = lambda chunk: pl.BlockSpec((None, 1, d), lambda i: ((i // tiles_per_seq) * nmod + chunk, 0, 0))
    s_spec = lambda chunk: pl.BlockSpec((tm_s, d), lambda i: (i, chunk))

    w_in_b = w_in[0].astype(BF16)
    b_in2 = b_in[0].reshape(1, -1)
    nmix = norm_mix[0].reshape(1, d)
    cw = conv_w[0]

    xp = x_prompt.reshape(tp, d)
    ya_p, q_p, k_p, v_p, ga_p, gb_p, ut_p = _inproj(
        xp, mod_p, mod_p, (p_spec(0), p_spec(1)), nmix, w_in_b, b_in2, cw,
        tm=TOKEN_TILE, tiles_per_seq=tiles_per_seq)
    xs_ = x_sample.reshape(ts, d)
    st = state_conv[0]
    zpad = lambda a: jnp.pad(a, ((0, 0), (0, tq - a.shape[1]), (0, 0))).reshape(ts, d)
    hist1 = zpad(st[:, 1:2])
    hist2 = zpad(st)
    ya_s, q_s, k_s, v_s, ga_s, gb_s, u_s = _inproj(
        xs_, mod_s, mod_s, (s_spec(0), s_spec(1)), nmix, w_in_b, b_in2, cw,
        tm=tm_s, tiles_per_seq=0, hist=(hist1, hist2))

    sk = sinks[0]
    yb_p = _attn_prompt(q_p, k_p, v_p, sk, batch=batch, seq=seq)
    kbuf = cache_k_win[0].reshape(nseq, wb, KV_DIM)
    vbuf = cache_v_win[0].reshape(nseq, wb, KV_DIM)
    yb_s, nk_s, nv_s = _attn_sample(
        q_s.reshape(nseq, tq, d), k_s.reshape(nseq, tq, KV_DIM), v_s.reshape(nseq, tq, KV_DIM),
        kbuf, vbuf, sk, bs=SUBLANES)
    yb_s = yb_s.reshape(ts, d)

    wa = w_branch_a[0].astype(BF16)
    wbb = w_branch_b[0].astype(BF16)
    wo = w_out[0].astype(BF16)
    bo = b_out[0].reshape(1, d)
    nffn = norm_ffn[0].reshape(1, d)
    wr = w_router[0]
    br = b_router[0].reshape(1, -1)
    n_tiles_p = tp // tm_s
    pp_spec = lambda chunk: pl.BlockSpec(
        (None, 1, d),
        lambda i: ((jnp.minimum(i, n_tiles_p - 1) // (seq // tm_s)) * nmod + chunk, 0, 0))
    sp_spec = lambda chunk: pl.BlockSpec((tm_s, d), lambda i: (jnp.maximum(i - n_tiles_p, 0), chunk))
    h_all, hm_all, gm_all, km_all = _post(
        (xp, ya_p, yb_p, ga_p, gb_p, mod_p, mod_p, mod_p), (pp_spec(2), pp_spec(3), pp_spec(4)),
        (xs_, ya_s, yb_s, ga_s, gb_s, mod_s, mod_s, mod_s), (sp_spec(2), sp_spec(3), sp_spec(4)),
        wa, wbb, wo, bo, nffn, wr, br, tm=tm_s)

    block = EXPERT_BLOCK
    dest, counts = _route(km_all, tm=tm_s, block=block)
    n_assign = t_all * TOP_K
    n_blocks = -(-(n_assign + N_EXPERTS * (block - 1)) // block)
    cnt = counts[0].astype(jnp.int32)
    pend = jnp.cumsum((cnt + block - 1) // block * block)
    n_used = (pend[-1] // block).astype(jnp.int32)
    blk_ids = jnp.minimum(jnp.arange(n_blocks, dtype=jnp.int32), n_used - 1)
    block_e = jnp.minimum(jnp.searchsorted(pend, blk_ids * block, side='right'),
                          N_EXPERTS - 1).astype(jnp.int32)

    xs_init = jnp.zeros((n_blocks * block, d), F32)
    xs_rows = _dispatch(dest, hm_all, xs_init, tm=ROW_DMA_TILE)
    wu = w_up[0]
    f = wu.shape[2] // 2
    wg = wu[:, :, 0::2].astype(BF16)
    wl = wu[:, :, 1::2].astype(BF16)
    bu = b_up[0]
    bg = bu[:, 0::2].reshape(N_EXPERTS, 1, f)
    bl = bu[:, 1::2].reshape(N_EXPERTS, 1, f)
    wd = w_down[0].astype(BF16)
    bd = b_down[0].reshape(N_EXPERTS, 1, d)
    ys = _experts(block_e, n_used.reshape(1), xs_rows, wg, wl, bg, bl, wd, bd, block=block)

    nfin = norm_final.reshape(1, d)
    tiles_p = tp // ROW_DMA_TILE
    pf_spec = pl.BlockSpec((None, 1, d),
                           lambda i: ((i // (seq // ROW_DMA_TILE)) * nmod + 5, 0, 0))
    y_p = _final(dest, h_all, mod_p, pf_spec, gm_all, km_all, nfin, ys,
                 tm=ROW_DMA_TILE, tile0=0, n_tiles=tiles_p)
    sf_spec = pl.BlockSpec((ROW_DMA_TILE, d), lambda i: (i, 5))
    y_s = _final(dest, h_all, mod_s, sf_spec, gm_all, km_all, nfin, ys,
                 tm=ROW_DMA_TILE, tile0=tiles_p, n_tiles=ts // ROW_DMA_TILE)

    keep = min(WINDOW, seq)
    k4 = k_p.reshape(batch, seq, N_KV_HEADS, HEAD_DIM)[:, seq - keep:]
    v4 = v_p.reshape(batch, seq, N_KV_HEADS, HEAD_DIM)[:, seq - keep:]
    conv_p = ut_p.reshape(batch, tiles_per_seq, SUBLANES, d)[:, -1, SUBLANES - (CONV_WIDTH - 1):]
    conv_s = u_s.reshape(nseq, tq, d)[:, tq - (CONV_WIDTH - 1):]
    return (y_p.reshape(batch, seq, d), y_s.reshape(nseq, tq, d),
            k4[None], v4[None], conv_p[None],
            nk_s.reshape(1, nseq, wb, N_KV_HEADS, HEAD_DIM),
            nv_s.reshape(1, nseq, wb, N_KV_HEADS, HEAD_DIM), conv_s[None])
```

```python
import functools

import jax
import jax.numpy as jnp
from jax import lax
from jax.experimental import pallas as pl
from jax.experimental.pallas import tpu as pltpu

F32 = jnp.float32
BF16 = jnp.bfloat16
HIGHEST = lax.Precision.HIGHEST

N_HEADS = 16
N_KV_HEADS = 4
HEAD_DIM = 64
GROUP = N_HEADS // N_KV_HEADS
KV_DIM = N_KV_HEADS * HEAD_DIM
WINDOW = 128
N_EXPERTS = 32
TOP_K = 4
CONV_WIDTH = 3
SWIGLU_ALPHA = 1.702
SWIGLU_LIMIT = 7.0
RMS_EPS = 1e-5
ALIBI_SLOPES = tuple(2.0 ** (-8.0 * (h + 1) / N_HEADS) for h in range(N_HEADS))

SUBLANES = 8
LANES = 128
VMEM_LIMIT_BYTES = 56 * 1024 * 1024
TOKEN_TILE = 512
ROW_DMA_TILE = 256
EXPERT_BLOCK = 256


def _cparams(*sem):
    return pltpu.CompilerParams(dimension_semantics=sem, vmem_limit_bytes=VMEM_LIMIT_BYTES)


def _rmsnorm(x, g):
    ms = jnp.mean(x * x, axis=-1, keepdims=True)
    return x * lax.rsqrt(ms + RMS_EPS) * g


def _ada_body(c_ref, w_ref, b_ref, o_ref):
    c = c_ref[...]
    a = c * jax.nn.sigmoid(c)
    o_ref[...] = jnp.dot(a, w_ref[...], preferred_element_type=F32, precision=HIGHEST) + b_ref[...]


def _ada(c_all, w_ada, b_ada):
    rows, d = c_all.shape
    n = w_ada.shape[1]
    return pl.pallas_call(
        _ada_body,
        grid=(n // d,),
        in_specs=[pl.BlockSpec((rows, d), lambda j: (0, 0)),
                  pl.BlockSpec((d, d), lambda j: (0, j)),
                  pl.BlockSpec((1, d), lambda j: (0, j))],
        out_specs=pl.BlockSpec((rows, d), lambda j: (0, j)),
        out_shape=jax.ShapeDtypeStruct((rows, n), F32),
        compiler_params=_cparams("arbitrary"),
        name="ada_modulation",
    )(c_all, w_ada, b_ada.reshape(1, n))


def _inproj_body(*refs, d, nc, tiles_per_seq):
    carry_mode = tiles_per_seq > 0
    if carry_mode:
        (x_ref, sh_ref, sc_ref, nw_ref, w_ref, b_ref, cw_ref,
         ya_ref, q_ref, k_ref, v_ref, ga_ref, gb_ref, ut_ref, carry_ref) = refs
    else:
        (x_ref, sh_ref, sc_ref, nw_ref, w_ref, b_ref, cw_ref, p1_ref, p2_ref,
         ya_ref, q_ref, k_ref, v_ref, ga_ref, gb_ref, ut_ref) = refs
    tm = x_ref.shape[0]
    xm = (_rmsnorm(x_ref[...], nw_ref[...]) * (1.0 + sc_ref[...]) + sh_ref[...]).astype(BF16)

    def proj(lo, width):
        return (jnp.dot(xm, w_ref[:, lo:lo + width], preferred_element_type=F32)
                + b_ref[:, lo:lo + width])

    if carry_mode:
        @pl.when(pl.program_id(0) % tiles_per_seq == 0)
        def _():
            carry_ref[...] = jnp.zeros_like(carry_ref)
    else:
        t_in_seq = lax.broadcasted_iota(jnp.int32, (tm, nc), 0) % SUBLANES

    for j in range(d // nc):
        c0 = j * nc
        u = proj(c0, nc) * proj(2 * d + c0, nc)
        if carry_mode:
            ue = jnp.concatenate([carry_ref[:, c0:c0 + nc], u], axis=0)
            s1 = pltpu.roll(ue, 1, axis=0)[SUBLANES:]
            s2 = pltpu.roll(ue, 2, axis=0)[SUBLANES:]
            carry_ref[:, c0:c0 + nc] = u[tm - SUBLANES:]
            ut_ref[:, c0:c0 + nc] = u[tm - SUBLANES:]
        else:
            s1 = jnp.where(t_in_seq >= 1, pltpu.roll(u, 1, axis=0), p1_ref[:, c0:c0 + nc])
            s2 = jnp.where(t_in_seq >= 2, pltpu.roll(u, 2, axis=0), p2_ref[:, c0:c0 + nc])
            ut_ref[:, c0:c0 + nc] = u
        cw = cw_ref[:, c0:c0 + nc]
        conv = s2 * cw[0:1] + s1 * cw[1:2] + u * cw[2:3]
        ya_ref[:, c0:c0 + nc] = (proj(d + c0, nc) * conv).astype(BF16)

    q_ref[...] = proj(3 * d, d).astype(BF16)
    k_ref[...] = proj(4 * d, KV_DIM)
    v_ref[...] = proj(4 * d + KV_DIM, KV_DIM)
    ga_ref[...] = jax.nn.sigmoid(proj(4 * d + 2 * KV_DIM, d)).astype(BF16)
    gb_ref[...] = jax.nn.sigmoid(proj(5 * d + 2 * KV_DIM, d)).astype(BF16)


def _inproj(x, shift, scale, shift_scale_specs, norm_w, w_in, b_in, conv_w, *, tm, tiles_per_seq,
            hist=None):
    t, d = x.shape
    n_in = w_in.shape[1]
    nc = 512
    carry_mode = tiles_per_seq > 0
    row = lambda width: pl.BlockSpec((tm, width), lambda i: (i, 0))
    full = lambda a: pl.BlockSpec(a.shape, lambda i: (0,) * a.ndim)
    in_specs = [row(d), shift_scale_specs[0], shift_scale_specs[1], full(norm_w), full(w_in),
                full(b_in), full(conv_w)]
    args = [x, shift, scale, norm_w, w_in, b_in, conv_w]
    if not carry_mode:
        in_specs += [row(d), row(d)]
        args += list(hist)
    ut_rows = SUBLANES if carry_mode else tm
    out_specs = [row(d), row(d), row(KV_DIM), row(KV_DIM), row(d), row(d),
                 pl.BlockSpec((ut_rows, d), lambda i: (i, 0))]
    out_shape = [jax.ShapeDtypeStruct((t, d), BF16), jax.ShapeDtypeStruct((t, d), BF16),
                 jax.ShapeDtypeStruct((t, KV_DIM), F32), jax.ShapeDtypeStruct((t, KV_DIM), F32),
                 jax.ShapeDtypeStruct((t, d), BF16), jax.ShapeDtypeStruct((t, d), BF16),
                 jax.ShapeDtypeStruct((t // tm * ut_rows, d), F32)]
    return pl.pallas_call(
        functools.partial(_inproj_body, d=d, nc=nc, tiles_per_seq=tiles_per_seq),
        grid=(t // tm,),
        in_specs=in_specs,
        out_specs=out_specs,
        out_shape=out_shape,
        scratch_shapes=[pltpu.VMEM((SUBLANES, d), F32)] if carry_mode else [],
        compiler_params=_cparams("arbitrary"),
        name="inproj_conv_long" if carry_mode else "inproj_conv_short",
    )(*args)


def _softmax_sink_pv(s, sink, v):
    m = jnp.maximum(jnp.max(s, axis=-1, keepdims=True), sink)
    p = jnp.exp(s - m)
    den = jnp.sum(p, axis=-1, keepdims=True) + jnp.exp(sink - m)
    return jnp.dot((p / den).astype(BF16), v, preferred_element_type=F32)


def _attn_prompt_body(sink_ref, q_ref, kc_ref, kp_ref, vc_ref, vp_ref, o_ref):
    blk = q_ref.shape[0]
    j = pl.program_id(1)
    kk = jnp.concatenate([kp_ref[...], kc_ref[...]], axis=0).astype(BF16)
    vv = jnp.concatenate([vp_ref[...], vc_ref[...]], axis=0).astype(BF16)
    qi = lax.broadcasted_iota(jnp.int32, (blk, 2 * blk), 0)
    kj = lax.broadcasted_iota(jnp.int32, (blk, 2 * blk), 1)
    dist = qi + blk - kj
    valid = (dist >= 0) & (dist < WINDOW) & ((kj >= blk) | (j > 0))
    distf = dist.astype(F32)
    outs = []
    for h in range(N_HEADS):
        kv = h // GROUP
        qh = q_ref[:, h * HEAD_DIM:(h + 1) * HEAD_DIM]
        kh = kk[:, kv * HEAD_DIM:(kv + 1) * HEAD_DIM]
        vh = vv[:, kv * HEAD_DIM:(kv + 1) * HEAD_DIM]
        s = lax.dot_general(qh, kh, (((1,), (1,)), ((), ())), preferred_element_type=F32)
        s = s * (HEAD_DIM ** -0.5) - ALIBI_SLOPES[h] * distf
        s = jnp.where(valid, s, -jnp.inf)
        outs.append(_softmax_sink_pv(s, sink_ref[h], vh))
    o_ref[...] = jnp.concatenate(outs, axis=1).astype(BF16)


def _attn_prompt(q, k, v, sinks, *, batch, seq):
    t, d = q.shape
    blk = WINDOW
    nb = seq // blk
    cur = lambda width: pl.BlockSpec((blk, width), lambda b, j: (b * nb + j, 0))
    prev = lambda width: pl.BlockSpec((blk, width), lambda b, j: (b * nb + jnp.maximum(j - 1, 0), 0))
    return pl.pallas_call(
        _attn_prompt_body,
        grid=(batch, nb),
        in_specs=[pl.BlockSpec(memory_space=pltpu.SMEM),
                  cur(d), cur(KV_DIM), prev(KV_DIM), cur(KV_DIM), prev(KV_DIM)],
        out_specs=cur(d),
        out_shape=jax.ShapeDtypeStruct((t, d), BF16),
        compiler_params=_cparams("arbitrary", "arbitrary"),
        name="attn_window_long",
    )(sinks, q, k, k, v, v)


def _attn_sample_body(sink_ref, q_ref, kn_ref, vn_ref, kb_ref, vb_ref, o_ref, nk_ref, nv_ref):
    bs, tq, _ = q_ref.shape
    wb = kb_ref.shape[1]
    rows = GROUP * tq
    ri = lax.broadcasted_iota(jnp.int32, (rows, wb + tq), 0)
    kj = lax.broadcasted_iota(jnp.int32, (rows, wb + tq), 1)
    dist = ri % tq + wb - kj
    valid = (dist >= 0) & (dist < WINDOW)
    distf = dist.astype(F32)
    g_col = lax.broadcasted_iota(jnp.int32, (rows, 1), 0) // tq
    for b in range(bs):
        keys = jnp.concatenate([kb_ref[b], kn_ref[b]], axis=0)
        vals = jnp.concatenate([vb_ref[b], vn_ref[b]], axis=0)
        nk_ref[b] = keys[tq:]
        nv_ref[b] = vals[tq:]
        keys = keys.astype(BF16)
        vals = vals.astype(BF16)
        qb = q_ref[b]
        outs = []
        for kv in range(N_KV_HEADS):
            h0 = kv * GROUP
            qh = jnp.concatenate(
                [qb[:, (h0 + g) * HEAD_DIM:(h0 + g + 1) * HEAD_DIM] for g in range(GROUP)], axis=0)
            kh = keys[:, kv * HEAD_DIM:(kv + 1) * HEAD_DIM]
            vh = vals[:, kv * HEAD_DIM:(kv + 1) * HEAD_DIM]
            slope = jnp.full((rows, 1), ALIBI_SLOPES[h0 + GROUP - 1], F32)
            sink = jnp.full((rows, 1), sink_ref[h0 + GROUP - 1], F32)
            for g in range(GROUP - 2, -1, -1):
                slope = jnp.where(g_col == g, ALIBI_SLOPES[h0 + g], slope)
                sink = jnp.where(g_col == g, sink_ref[h0 + g], sink)
            s = lax.dot_general(qh, kh, (((1,), (1,)), ((), ())), preferred_element_type=F32)
            s = s * (HEAD_DIM ** -0.5) - slope * distf
            s = jnp.where(valid, s, -jnp.inf)
            o = _softmax_sink_pv(s, sink, vh)
            outs += [o[g * tq:(g + 1) * tq] for g in range(GROUP)]
        o_ref[b] = jnp.concatenate(outs, axis=1).astype(BF16)


def _attn_sample(q, k_new, v_new, k_buf, v_buf, sinks, *, bs):
    nseq, tq, d = q.shape
    wb = k_buf.shape[1]
    blk3 = lambda a, b_: pl.BlockSpec((bs, a, b_), lambda i: (i, 0, 0))
    return pl.pallas_call(
        _attn_sample_body,
        grid=(nseq // bs,),
        in_specs=[pl.BlockSpec(memory_space=pltpu.SMEM),
                  blk3(tq, d), blk3(tq, KV_DIM), blk3(tq, KV_DIM), blk3(wb, KV_DIM), blk3(wb, KV_DIM)],
        out_specs=[blk3(tq, d), blk3(wb, KV_DIM), blk3(wb, KV_DIM)],
        out_shape=[jax.ShapeDtypeStruct((nseq, tq, d), BF16),
                   jax.ShapeDtypeStruct((nseq, wb, KV_DIM), F32),
                   jax.ShapeDtypeStruct((nseq, wb, KV_DIM), F32)],
        compiler_params=_cparams("arbitrary"),
        name="attn_window_short",
    )(sinks, q, k_new, v_new, k_buf, v_buf)


N_POST_ACTS = 8


def _post_body(*refs, n_first):
    first, second = refs[:N_POST_ACTS], refs[N_POST_ACTS:2 * N_POST_ACTS]
    rest = refs[2 * N_POST_ACTS:]
    i = pl.program_id(0)

    @pl.when(i < n_first)
    def _():
        _post_tile(*first, *rest)

    @pl.when(i >= n_first)
    def _():
        _post_tile(*second, *rest)


def _post_tile(x_ref, ya_ref, yb_ref, ga_ref, gb_ref, g1_ref, sh2_ref, sc2_ref, wa_ref, wb_ref,
               wo_ref, bo_ref, nf_ref, wr_ref, br_ref, h_ref, hm_ref, gm_ref, km_ref):
    a = jnp.dot(ya_ref[...], wa_ref[...], preferred_element_type=F32)
    b = jnp.dot(yb_ref[...], wb_ref[...], preferred_element_type=F32)
    merged = ga_ref[...].astype(F32) * a + gb_ref[...].astype(F32) * b
    o = jnp.dot(merged.astype(BF16), wo_ref[...], preferred_element_type=F32) + bo_ref[...]
    h = x_ref[...] + g1_ref[...] * o
    h_ref[...] = h
    hm = _rmsnorm(h, nf_ref[...]) * (1.0 + sc2_ref[...]) + sh2_ref[...]
    hm_ref[...] = hm
    ne = br_ref.shape[1]
    hm_hi = hm.astype(BF16)
    hm_lo = (hm - hm_hi.astype(F32)).astype(BF16)
    wr = wr_ref[...]
    p_hi = jnp.dot(hm_hi, wr, preferred_element_type=F32)
    p_lo = jnp.dot(hm_lo, wr[:, :2 * ne], preferred_element_type=F32)
    logits = (p_hi[:, :ne] + (p_hi[:, ne:2 * ne] + p_lo[:, :ne])
              + (p_hi[:, 2 * ne:] + p_lo[:, ne:]) + br_ref[...])
    tm = logits.shape[0]
    lane = lax.broadcasted_iota(jnp.int32, (tm, ne), 1)
    vals = logits
    km = jnp.zeros((tm, ne), F32)
    top = []
    for k in range(TOP_K):
        m = jnp.max(vals, axis=-1, keepdims=True)
        first = jnp.min(jnp.where(vals == m, lane, ne), axis=-1, keepdims=True)
        sel = lane == first
        km = jnp.where(sel, float(k + 1), km)
        vals = jnp.where(sel, -jnp.inf, vals)
        top.append(m)
    e = [jnp.exp(t - top[0]) for t in top]
    den = e[0] + e[1] + e[2] + e[3]
    gm = jnp.zeros((tm, ne), F32)
    for k in range(TOP_K):
        gm = jnp.where(km == float(k + 1), e[k] / den, gm)
    gm_ref[...] = gm
    km_ref[...] = km


def _post(first, first_mod_specs, second, second_mod_specs, wa, wb, wo, bo, nf, wr, br, *, tm):
    t1, d = first[0].shape
    t2 = second[0].shape[0]
    n1, n2 = t1 // tm, t2 // tm
    ne = br.shape[1]
    row1 = pl.BlockSpec((tm, d), lambda i: (jnp.minimum(i, n1 - 1), 0))
    row2 = pl.BlockSpec((tm, d), lambda i: (jnp.maximum(i - n1, 0), 0))
    orow = lambda width: pl.BlockSpec((tm, width), lambda i: (i, 0))
    full = lambda a: pl.BlockSpec(a.shape, lambda i: (0,) * a.ndim)
    weights = [wa, wb, wo, bo, nf, wr, br]
    in_specs = ([row1] * 5 + list(first_mod_specs) + [row2] * 5 + list(second_mod_specs)
                + [full(w) for w in weights])
    return pl.pallas_call(
        functools.partial(_post_body, n_first=n1),
        grid=(n1 + n2,),
        in_specs=in_specs,
        out_specs=[orow(d), orow(d), orow(ne), orow(ne)],
        out_shape=[jax.ShapeDtypeStruct((t1 + t2, d), F32), jax.ShapeDtypeStruct((t1 + t2, d), F32),
                   jax.ShapeDtypeStruct((t1 + t2, ne), F32), jax.ShapeDtypeStruct((t1 + t2, ne), F32)],
        compiler_params=_cparams("arbitrary"),
        name="merge_outproj_router",
    )(*first, *second, *weights)


def _route_body(km_ref, dest_ref, cnt_ref, run_ref, start_ref, *, block):
    ph = pl.program_id(0)
    i = pl.program_id(1)
    km = km_ref[...]
    tm, ne = km.shape
    chosen = (km > 0.0).astype(F32)
    tile_cnt = jnp.sum(chosen, axis=0, keepdims=True)

    @pl.when((ph == 0) & (i == 0))
    def _():
        run_ref[...] = jnp.zeros_like(run_ref)

    @pl.when(ph == 0)
    def _():
        run_ref[...] += tile_cnt

    @pl.when((ph == 1) & (i == 0))
    def _():
        cnt = run_ref[...]
        cnt_ref[...] = cnt
        padded = jnp.ceil(cnt / block) * block
        r = lax.broadcasted_iota(jnp.int32, (ne, ne), 0)
        c = lax.broadcasted_iota(jnp.int32, (ne, ne), 1)
        before = (r < c).astype(F32)
        start_ref[...] = jnp.dot(padded, before, preferred_element_type=F32, precision=HIGHEST)
        run_ref[...] = jnp.zeros_like(run_ref)

    @pl.when(ph == 1)
    def _():
        r = lax.broadcasted_iota(jnp.int32, (tm, tm), 0)
        c = lax.broadcasted_iota(jnp.int32, (tm, tm), 1)
        earlier = (c < r).astype(BF16)
        rank = jnp.dot(earlier, chosen.astype(BF16), preferred_element_type=F32)
        slot = start_ref[...] + run_ref[...] + rank
        ones = jnp.ones((SUBLANES, ne), F32)
        for k in range(TOP_K):
            sk = jnp.where(km == float(k + 1), slot, 0.0)
            rowk = lax.dot_general(ones, sk, (((1,), (1,)), ((), ())),
                                   preferred_element_type=F32, precision=HIGHEST)
            dest_ref[k:k + 1, :] = rowk[0:1].astype(jnp.int32)
        run_ref[...] += tile_cnt


def _route(km, *, tm, block):
    t, ne = km.shape
    return pl.pallas_call(
        functools.partial(_route_body, block=block),
        grid=(2, t // tm),
        in_specs=[pl.BlockSpec((tm, ne), lambda p, i: (i, 0))],
        out_specs=[pl.BlockSpec((TOP_K, tm), lambda p, i: (0, i * p)),
                   pl.BlockSpec((1, ne), lambda p, i: (0, 0))],
        out_shape=[jax.ShapeDtypeStruct((TOP_K, t), jnp.int32),
                   jax.ShapeDtypeStruct((1, ne), F32)],
        scratch_shapes=[pltpu.VMEM((1, ne), F32), pltpu.VMEM((1, ne), F32)],
        compiler_params=_cparams("arbitrary", "arbitrary"),
        name="route_slots",
    )(km)


def _row_copy(src, src_row, dst, dst_row, sem):
    return pltpu.make_async_copy(src.at[pl.ds(src_row, 1), :], dst.at[pl.ds(dst_row, 1), :], sem)


def _dispatch_body(pend_ref, dest_ref, hm_ref, xs_ref, zero_ref, sem, zsem):
    tm = hm_ref.shape[0]
    block = zero_ref.shape[0]

    @pl.when(pl.program_id(0) == 0)
    def _():
        zero_ref[...] = jnp.zeros_like(zero_ref)
        n_exp = pend_ref.shape[0]
        n_slots = xs_ref.shape[0]
        fills = []
        for e in range(n_exp):
            end = pend_ref[e]
            begin = pend_ref[e - 1] if e > 0 else 0
            fills.append((end > begin, end - block))
        for j in range(n_exp):
            begin = pend_ref[n_exp - 1] + j * block
            fills.append((begin < n_slots, begin))
        for phase in ("start", "wait"):
            for cond, begin in fills:
                @pl.when(cond)
                def _():
                    fill = pltpu.make_async_copy(
                        zero_ref, xs_ref.at[pl.ds(pl.multiple_of(begin, block), block), :], zsem)
                    fill.start() if phase == "start" else fill.wait()

    def issue(t, carry):
        for k in range(TOP_K):
            _row_copy(hm_ref, t, xs_ref, dest_ref[k, t], sem).start(priority=k % 2)
        return carry

    lax.fori_loop(0, tm, issue, 0)

    def drain(t, carry):
        for k in range(TOP_K):
            _row_copy(hm_ref, t, xs_ref, 0, sem).wait()
        return carry

    lax.fori_loop(0, tm, drain, 0)


def _dispatch(pend, dest, hm, *, tm, n_slots, block):
    t, d = hm.shape
    return pl.pallas_call(
        _dispatch_body,
        grid_spec=pltpu.PrefetchScalarGridSpec(
            num_scalar_prefetch=1,
            grid=(t // tm,),
            in_specs=[pl.BlockSpec((TOP_K, tm), lambda i, pe: (0, i), memory_space=pltpu.SMEM),
                      pl.BlockSpec((tm, d), lambda i, pe: (i, 0))],
            out_specs=pl.BlockSpec(memory_space=pl.ANY),
            scratch_shapes=[pltpu.VMEM((block, d), F32), pltpu.SemaphoreType.DMA,
                            pltpu.SemaphoreType.DMA],
        ),
        out_shape=jax.ShapeDtypeStruct((n_slots, d), F32),
        compiler_params=_cparams("arbitrary"),
        name="moe_dispatch_rows",
    )(pend, dest, hm)


PAIR_GROUP = 2 * LANES


def _expert_body(be_ref, nb_ref, xs_ref, wu_ref, bu_ref, wd_ref, bd_ref, ys_ref, wu_bf, wd_bf):
    i = pl.program_id(0)
    f2 = wu_ref.shape[1]
    new_expert = (i == 0) | (be_ref[i] != be_ref[jnp.maximum(i - 1, 0)])

    @pl.when((i < nb_ref[0]) & new_expert)
    def _():
        r = lax.broadcasted_iota(jnp.int32, (PAIR_GROUP, PAIR_GROUP), 0)
        c = lax.broadcasted_iota(jnp.int32, (PAIR_GROUP, PAIR_GROUP), 1)
        src = jnp.where(c < LANES, 2 * c, 2 * (c - LANES) + 1)
        perm = (r == src).astype(BF16)
        for g in range(f2 // PAIR_GROUP):
            cols = slice(g * PAIR_GROUP, (g + 1) * PAIR_GROUP)
            wu_bf[:, cols] = jnp.dot(wu_ref[:, cols].astype(BF16), perm,
                                     preferred_element_type=F32).astype(BF16)
        wd_bf[...] = wd_ref[...].astype(BF16)

    @pl.when(i < nb_ref[0])
    def _():
        x = xs_ref[...].astype(BF16)
        u = jnp.dot(x, wu_bf[...], preferred_element_type=F32) + bu_ref[...]
        acts = []
        for g in range(f2 // PAIR_GROUP):
            glu = jnp.minimum(u[:, g * PAIR_GROUP:g * PAIR_GROUP + LANES], SWIGLU_LIMIT)
            lin = jnp.clip(u[:, g * PAIR_GROUP + LANES:(g + 1) * PAIR_GROUP],
                           -SWIGLU_LIMIT, SWIGLU_LIMIT)
            acts.append((glu * jax.nn.sigmoid(SWIGLU_ALPHA * glu) * (lin + 1.0)).astype(BF16))
        act = jnp.concatenate(acts, axis=1)
        ys_ref[...] = jnp.dot(act, wd_bf[...], preferred_element_type=F32) + bd_ref[...]

    @pl.when(i >= nb_ref[0])
    def _():
        ys_ref[...] = jnp.zeros_like(ys_ref)


def _experts(block_e, n_used, xs, wu, bu, wd, bd, *, block):
    n_slots, d = xs.shape
    f2 = wu.shape[2]
    n_blocks = n_slots // block
    xrow = pl.BlockSpec((block, d), lambda i, be, nb: (jnp.minimum(i, nb[0] - 1), 0))
    wspec = lambda a, b_: pl.BlockSpec((None, a, b_), lambda i, be, nb: (be[i], 0, 0))
    return pl.pallas_call(
        _expert_body,
        grid_spec=pltpu.PrefetchScalarGridSpec(
            num_scalar_prefetch=2,
            grid=(n_blocks,),
            in_specs=[xrow, wspec(d, f2), wspec(1, f2), wspec(f2 // 2, d), wspec(1, d)],
            out_specs=pl.BlockSpec((block, d), lambda i, be, nb: (i, 0)),
            scratch_shapes=[pltpu.VMEM((d, f2), BF16), pltpu.VMEM((f2 // 2, d), BF16)],
        ),
        out_shape=jax.ShapeDtypeStruct((n_slots, d), F32),
        compiler_params=_cparams("arbitrary"),
        name="moe_expert_mlp",
    )(block_e, n_used, xs, wu, bu, wd, bd)


def _final_body(dest_ref, h_ref, g2_ref, gm_ref, km_ref, nw_ref, ys_ref, o_ref, buf_ref, sem):
    tm = h_ref.shape[0]

    def issue(t, carry):
        for k in range(TOP_K):
            _row_copy(ys_ref, dest_ref[k, t], buf_ref.at[k], t, sem).start(priority=k % 2)
        return carry

    lax.fori_loop(0, tm, issue, 0)

    def drain(t, carry):
        for k in range(TOP_K):
            _row_copy(ys_ref, 0, buf_ref.at[k], t, sem).wait()
        return carry

    lax.fori_loop(0, tm, drain, 0)

    km = km_ref[...]
    gm = gm_ref[...]
    acc = jnp.zeros(h_ref.shape, F32)
    for k in range(TOP_K):
        gate = jnp.sum(jnp.where(km == float(k + 1), gm, 0.0), axis=-1, keepdims=True)
        acc = acc + gate * buf_ref[k]
    y = h_ref[...] + g2_ref[...] * acc
    o_ref[...] = _rmsnorm(y, nw_ref[...])


def _final(dest, h, g2, g2_spec, gm, km, norm_w, ys, *, tm, tile0, n_tiles):
    d = h.shape[1]
    ne = gm.shape[1]
    row = lambda width: pl.BlockSpec((tm, width), lambda i: (i + tile0, 0))
    return pl.pallas_call(
        _final_body,
        grid=(n_tiles,),
        in_specs=[pl.BlockSpec((TOP_K, tm), lambda i: (0, i + tile0), memory_space=pltpu.SMEM),
                  row(d), g2_spec, row(ne), row(ne),
                  pl.BlockSpec(norm_w.shape, lambda i: (0, 0)),
                  pl.BlockSpec(memory_space=pl.ANY)],
        out_specs=pl.BlockSpec((tm, d), lambda i: (i, 0)),
        out_shape=jax.ShapeDtypeStruct((n_tiles * tm, d), F32),
        scratch_shapes=[pltpu.VMEM((TOP_K, tm, d), F32), pltpu.SemaphoreType.DMA],
        compiler_params=_cparams("arbitrary"),
        name="moe_combine_final_norm",
    )(dest, h, g2, gm, km, norm_w, ys)


def kernel(x_prompt, x_sample, cache_k_win, cache_v_win, state_conv, c_prompt, c_sample, w_ada, b_ada, norm_mix, norm_ffn, w_in, b_in, conv_w, sinks, w_branch_a, w_branch_b, w_out, b_out, w_router, b_router, w_up, b_up, w_down, b_down, norm_final):
    batch, seq, d = x_prompt.shape
    nseq, tq, _ = x_sample.shape
    depth = w_ada.shape[0]
    assert depth == 1 and tq == SUBLANES and seq % TOKEN_TILE == 0
    wb = cache_k_win.shape[2]
    tp, ts = batch * seq, nseq * tq
    t_all = tp + ts
    tm_s = min(ts, TOKEN_TILE)
    assert ts % tm_s == 0 and tp % tm_s == 0 and t_all % ROW_DMA_TILE == 0
    nmod = w_ada.shape[2] // d

    n_c = batch + nseq
    c_rows = -(-n_c // SUBLANES) * SUBLANES
    c_all = jnp.concatenate([c_prompt, c_sample, jnp.zeros((c_rows - n_c, d), F32)], axis=0)
    mod = _ada(c_all, w_ada[0], b_ada[0])
    mod_p = mod[:batch].reshape(batch * nmod, 1, d)
    mod_s = jnp.repeat(mod[batch:n_c], tq, axis=0)
    tiles_per_seq = seq // TOKEN_TILE
    p_spec = lambda chunk: pl.BlockSpec((None, 1, d), lambda i: ((i // tiles_per_seq) * nmod + chunk, 0, 0))
    s_spec = lambda chunk: pl.BlockSpec((tm_s, d), lambda i: (i, chunk))

    w_in_b = w_in[0].astype(BF16)
    b_in2 = b_in[0].reshape(1, -1)
    nmix = norm_mix[0].reshape(1, d)
    cw = conv_w[0]

    xp = x_prompt.reshape(tp, d)
    ya_p, q_p, k_p, v_p, ga_p, gb_p, ut_p = _inproj(
        xp, mod_p, mod_p, (p_spec(0), p_spec(1)), nmix, w_in_b, b_in2, cw,
        tm=TOKEN_TILE, tiles_per_seq=tiles_per_seq)
    xs_ = x_sample.reshape(ts, d)
    st = state_conv[0]
    zpad = lambda a: jnp.pad(a, ((0, 0), (0, tq - a.shape[1]), (0, 0))).reshape(ts, d)
    hist1 = zpad(st[:, 1:2])
    hist2 = zpad(st)
    ya_s, q_s, k_s, v_s, ga_s, gb_s, u_s = _inproj(
        xs_, mod_s, mod_s, (s_spec(0), s_spec(1)), nmix, w_in_b, b_in2, cw,
        tm=tm_s, tiles_per_seq=0, hist=(hist1, hist2))

    sk = sinks[0]
    yb_p = _attn_prompt(q_p, k_p, v_p, sk, batch=batch, seq=seq)
    kbuf = cache_k_win[0].reshape(nseq, wb, KV_DIM)
    vbuf = cache_v_win[0].reshape(nseq, wb, KV_DIM)
    yb_s, nk_s, nv_s = _attn_sample(
        q_s.reshape(nseq, tq, d), k_s.reshape(nseq, tq, KV_DIM), v_s.reshape(nseq, tq, KV_DIM),
        kbuf, vbuf, sk, bs=SUBLANES)
    yb_s = yb_s.reshape(ts, d)

    wa = w_branch_a[0].astype(BF16)
    wbb = w_branch_b[0].astype(BF16)
    wo = w_out[0].astype(BF16)
    bo = b_out[0].reshape(1, d)
    nffn = norm_ffn[0].reshape(1, d)
    wr0 = w_router[0].astype(BF16)
    wr1 = (w_router[0] - wr0.astype(F32)).astype(BF16)
    wr2 = (w_router[0] - wr0.astype(F32) - wr1.astype(F32)).astype(BF16)
    wr = jnp.concatenate([wr0, wr1, wr2], axis=1)
    br = b_router[0].reshape(1, -1)
    n_tiles_p = tp // tm_s
    pp_spec = lambda chunk: pl.BlockSpec(
        (None, 1, d),
        lambda i: ((jnp.minimum(i, n_tiles_p - 1) // (seq // tm_s)) * nmod + chunk, 0, 0))
    sp_spec = lambda chunk: pl.BlockSpec((tm_s, d), lambda i: (jnp.maximum(i - n_tiles_p, 0), chunk))
    h_all, hm_all, gm_all, km_all = _post(
        (xp, ya_p, yb_p, ga_p, gb_p, mod_p, mod_p, mod_p), (pp_spec(2), pp_spec(3), pp_spec(4)),
        (xs_, ya_s, yb_s, ga_s, gb_s, mod_s, mod_s, mod_s), (sp_spec(2), sp_spec(3), sp_spec(4)),
        wa, wbb, wo, bo, nffn, wr, br, tm=tm_s)

    block = EXPERT_BLOCK
    dest, counts = _route(km_all, tm=tm_s, block=block)
    n_assign = t_all * TOP_K
    n_blocks = -(-(n_assign + N_EXPERTS * (block - 1)) // block)
    cnt = counts[0].astype(jnp.int32)
    pend = jnp.cumsum((cnt + block - 1) // block * block)
    n_used = (pend[-1] // block).astype(jnp.int32)
    blk_ids = jnp.minimum(jnp.arange(n_blocks, dtype=jnp.int32), n_used - 1)
    block_e = jnp.sum((pend[None, :] <= (blk_ids * block)[:, None]).astype(jnp.int32), axis=1)
    block_e = jnp.minimum(block_e, N_EXPERTS - 1)

    xs_rows = _dispatch(pend.astype(jnp.int32), dest, hm_all, tm=ROW_DMA_TILE,
                        n_slots=n_blocks * block, block=block)
    f2 = w_up.shape[3]
    bu = b_up[0].reshape(N_EXPERTS, f2 // PAIR_GROUP, LANES, 2).transpose(0, 1, 3, 2)
    bu = bu.reshape(N_EXPERTS, 1, f2)
    bd = b_down[0].reshape(N_EXPERTS, 1, d)
    ys = _experts(block_e, n_used.reshape(1), xs_rows, w_up[0], bu, w_down[0], bd, block=block)

    nfin = norm_final.reshape(1, d)
    tiles_p = tp // ROW_DMA_TILE
    pf_spec = pl.BlockSpec((None, 1, d),
                           lambda i: ((i // (seq // ROW_DMA_TILE)) * nmod + 5, 0, 0))
    y_p = _final(dest, h_all, mod_p, pf_spec, gm_all, km_all, nfin, ys,
                 tm=ROW_DMA_TILE, tile0=0, n_tiles=tiles_p)
    sf_spec = pl.BlockSpec((ROW_DMA_TILE, d), lambda i: (i, 5))
    y_s = _final(dest, h_all, mod_s, sf_spec, gm_all, km_all, nfin, ys,
                 tm=ROW_DMA_TILE, tile0=tiles_p, n_tiles=ts // ROW_DMA_TILE)

    keep = min(WINDOW, seq)
    k4 = k_p.reshape(batch, seq, N_KV_HEADS, HEAD_DIM)[:, seq - keep:]
    v4 = v_p.reshape(batch, seq, N_KV_HEADS, HEAD_DIM)[:, seq - keep:]
    conv_p = ut_p.reshape(batch, tiles_per_seq, SUBLANES, d)[:, -1, SUBLANES - (CONV_WIDTH - 1):]
    conv_s = u_s.reshape(nseq, tq, d)[:, tq - (CONV_WIDTH - 1):]
    return (y_p.reshape(batch, seq, d), y_s.reshape(nseq, tq, d),
            k4[None], v4[None], conv_p[None],
            nk_s.reshape(1, nseq, wb, N_KV_HEADS, HEAD_DIM),
            nv_s.reshape(1, nseq, wb, N_KV_HEADS, HEAD_DIM), conv_s[None])
```

```python
import functools

import jax
import jax.numpy as jnp
from jax import lax
from jax.experimental import pallas as pl
from jax.experimental.pallas import tpu as pltpu

F32 = jnp.float32
BF16 = jnp.bfloat16
HIGHEST = lax.Precision.HIGHEST

N_HEADS = 16
N_KV_HEADS = 4
HEAD_DIM = 64
GROUP = N_HEADS // N_KV_HEADS
KV_DIM = N_KV_HEADS * HEAD_DIM
WINDOW = 128
N_EXPERTS = 32
TOP_K = 4
CONV_WIDTH = 3
SWIGLU_ALPHA = 1.702
SWIGLU_LIMIT = 7.0
RMS_EPS = 1e-5
ALIBI_SLOPES = tuple(2.0 ** (-8.0 * (h + 1) / N_HEADS) for h in range(N_HEADS))

SUBLANES = 8
LANES = 128
VMEM_LIMIT_BYTES = 56 * 1024 * 1024
TOKEN_TILE = 512
ROW_DMA_TILE = 256
ROW_DMA_UNROLL = 8
EXPERT_BLOCK = 512


def _cparams(*sem):
    return pltpu.CompilerParams(dimension_semantics=sem, vmem_limit_bytes=VMEM_LIMIT_BYTES)


def _rmsnorm(x, g):
    ms = jnp.mean(x * x, axis=-1, keepdims=True)
    return x * lax.rsqrt(ms + RMS_EPS) * g


def _ada_body(c_ref, w_ref, b_ref, o_ref):
    c = c_ref[...]
    a = c * jax.nn.sigmoid(c)
    o_ref[...] = jnp.dot(a, w_ref[...], preferred_element_type=F32, precision=HIGHEST) + b_ref[...]


def _ada(c_all, w_ada, b_ada):
    rows, d = c_all.shape
    n = w_ada.shape[1]
    return pl.pallas_call(
        _ada_body,
        grid=(n // d,),
        in_specs=[pl.BlockSpec((rows, d), lambda j: (0, 0)),
                  pl.BlockSpec((d, d), lambda j: (0, j)),
                  pl.BlockSpec((1, d), lambda j: (0, j))],
        out_specs=pl.BlockSpec((rows, d), lambda j: (0, j)),
        out_shape=jax.ShapeDtypeStruct((rows, n), F32),
        compiler_params=_cparams("arbitrary"),
        name="ada_modulation",
    )(c_all, w_ada, b_ada.reshape(1, n))


def _inproj_body(*refs, d, nc, tiles_per_seq):
    carry_mode = tiles_per_seq > 0
    if carry_mode:
        (x_ref, sh_ref, sc_ref, nw_ref, w_ref, b_ref, cw_ref,
         ya_ref, q_ref, k_ref, v_ref, ga_ref, gb_ref, ut_ref, carry_ref) = refs
    else:
        (x_ref, sh_ref, sc_ref, nw_ref, w_ref, b_ref, cw_ref, p1_ref, p2_ref,
         ya_ref, q_ref, k_ref, v_ref, ga_ref, gb_ref, ut_ref) = refs
    tm = x_ref.shape[0]
    xm = (_rmsnorm(x_ref[...], nw_ref[...]) * (1.0 + sc_ref[...]) + sh_ref[...]).astype(BF16)

    def proj(lo, width):
        return (jnp.dot(xm, w_ref[:, lo:lo + width], preferred_element_type=F32)
                + b_ref[:, lo:lo + width])

    if carry_mode:
        @pl.when(pl.program_id(0) % tiles_per_seq == 0)
        def _():
            carry_ref[...] = jnp.zeros_like(carry_ref)
    else:
        t_in_seq = lax.broadcasted_iota(jnp.int32, (tm, nc), 0) % SUBLANES

    for j in range(d // nc):
        c0 = j * nc
        u = proj(c0, nc) * proj(2 * d + c0, nc)
        if carry_mode:
            ue = jnp.concatenate([carry_ref[:, c0:c0 + nc], u], axis=0)
            s1 = pltpu.roll(ue, 1, axis=0)[SUBLANES:]
            s2 = pltpu.roll(ue, 2, axis=0)[SUBLANES:]
            carry_ref[:, c0:c0 + nc] = u[tm - SUBLANES:]
            ut_ref[:, c0:c0 + nc] = u[tm - SUBLANES:]
        else:
            s1 = jnp.where(t_in_seq >= 1, pltpu.roll(u, 1, axis=0), p1_ref[:, c0:c0 + nc])
            s2 = jnp.where(t_in_seq >= 2, pltpu.roll(u, 2, axis=0), p2_ref[:, c0:c0 + nc])
            ut_ref[:, c0:c0 + nc] = u
        cw = cw_ref[:, c0:c0 + nc]
        conv = s2 * cw[0:1] + s1 * cw[1:2] + u * cw[2:3]
        ya_ref[:, c0:c0 + nc] = (proj(d + c0, nc) * conv).astype(BF16)

    q_ref[...] = proj(3 * d, d).astype(BF16)
    k_ref[...] = proj(4 * d, KV_DIM)
    v_ref[...] = proj(4 * d + KV_DIM, KV_DIM)
    ga_ref[...] = jax.nn.sigmoid(proj(4 * d + 2 * KV_DIM, d)).astype(BF16)
    gb_ref[...] = jax.nn.sigmoid(proj(5 * d + 2 * KV_DIM, d)).astype(BF16)


def _inproj(x, shift, scale, shift_scale_specs, norm_w, w_in, b_in, conv_w, *, tm, tiles_per_seq,
            hist=None):
    t, d = x.shape
    n_in = w_in.shape[1]
    nc = 512
    carry_mode = tiles_per_seq > 0
    row = lambda width: pl.BlockSpec((tm, width), lambda i: (i, 0))
    full = lambda a: pl.BlockSpec(a.shape, lambda i: (0,) * a.ndim)
    in_specs = [row(d), shift_scale_specs[0], shift_scale_specs[1], full(norm_w), full(w_in),
                full(b_in), full(conv_w)]
    args = [x, shift, scale, norm_w, w_in, b_in, conv_w]
    if not carry_mode:
        in_specs += [row(d), row(d)]
        args += list(hist)
    ut_rows = SUBLANES if carry_mode else tm
    out_specs = [row(d), row(d), row(KV_DIM), row(KV_DIM), row(d), row(d),
                 pl.BlockSpec((ut_rows, d), lambda i: (i, 0))]
    out_shape = [jax.ShapeDtypeStruct((t, d), BF16), jax.ShapeDtypeStruct((t, d), BF16),
                 jax.ShapeDtypeStruct((t, KV_DIM), F32), jax.ShapeDtypeStruct((t, KV_DIM), F32),
                 jax.ShapeDtypeStruct((t, d), BF16), jax.ShapeDtypeStruct((t, d), BF16),
                 jax.ShapeDtypeStruct((t // tm * ut_rows, d), F32)]
    return pl.pallas_call(
        functools.partial(_inproj_body, d=d, nc=nc, tiles_per_seq=tiles_per_seq),
        grid=(t // tm,),
        in_specs=in_specs,
        out_specs=out_specs,
        out_shape=out_shape,
        scratch_shapes=[pltpu.VMEM((SUBLANES, d), F32)] if carry_mode else [],
        compiler_params=_cparams("arbitrary"),
        name="inproj_conv_long" if carry_mode else "inproj_conv_short",
    )(*args)


LOG2_E = 1.4426950408889634


def _softmax_rows(s):
    m = jnp.max(s, axis=-1, keepdims=True)
    p = jnp.exp2(s - m)
    den = jnp.sum(p, axis=-1, keepdims=True)
    return (p * (1.0 / den)).astype(BF16)


def _own_block(rows, rows_per_kv):
    r = lax.broadcasted_iota(jnp.int32, (rows, KV_DIM), 0)
    c = lax.broadcasted_iota(jnp.int32, (rows, KV_DIM), 1)
    return (c // HEAD_DIM) == (r // rows_per_kv) % N_KV_HEADS


def _attn_prompt_body(q_ref, kc_ref, kp_ref, vc_ref, vp_ref, bias_ref, o_ref):
    blk = q_ref.shape[0]
    kk = jnp.concatenate([kp_ref[...], kc_ref[...]], axis=0)
    vv = jnp.concatenate([vp_ref[...], vc_ref[...]], axis=0)
    col_kv = lax.broadcasted_iota(jnp.int32, kk.shape, 1) // HEAD_DIM
    not_sink = lax.broadcasted_iota(jnp.int32, kk.shape, 0) > 0
    kk_kv = [jnp.where((col_kv == kv) & not_sink, kk, 0.0).astype(BF16)
             for kv in range(N_KV_HEADS)]
    vv_kv = [jnp.where((col_kv == kv) & not_sink, vv, 0.0).astype(BF16)
             for kv in range(N_KV_HEADS)]

    def scores(g):
        a = q_ref[:, g * KV_DIM:(g + 1) * KV_DIM]
        return [lax.dot_general(a, kk_kv[kv], (((1,), (1,)), ((), ())),
                                preferred_element_type=F32)
                + bias_ref[g, kv * blk:(kv + 1) * blk, :] for kv in range(N_KV_HEADS)]

    def weighted_values(g, probs):
        out = jnp.dot(probs[0], vv_kv[0], preferred_element_type=F32)
        for kv in range(1, N_KV_HEADS):
            out = out + jnp.dot(probs[kv], vv_kv[kv], preferred_element_type=F32)
        o_ref[:, g * KV_DIM:(g + 1) * KV_DIM] = out.astype(BF16)

    s_next = scores(0)
    p_prev = None
    for g in range(GROUP):
        s_cur = s_next
        if g + 1 < GROUP:
            s_next = scores(g + 1)
        p_cur = [_softmax_rows(s) for s in s_cur]
        if p_prev is not None:
            weighted_values(g - 1, p_prev)
        p_prev = p_cur
    weighted_values(GROUP - 1, p_prev)


def _attn_table(sinks, tq_rows, n_keys, key_offset, first_block):
    assert key_offset >= WINDOW
    kv = jnp.arange(N_KV_HEADS)[None, :, None, None]
    g = jnp.arange(GROUP)[:, None, None, None]
    qi = jnp.arange(tq_rows)[None, None, :, None]
    kj = jnp.arange(n_keys)[None, None, None, :]
    slope = jnp.exp2(-8.0 * (kv * GROUP + g + 1).astype(F32) / N_HEADS)
    dist = qi + key_offset - kj
    valid = (dist >= 0) & (dist < WINDOW)
    if first_block:
        valid = valid & (kj >= key_offset)
    bias = jnp.where(valid, -slope * dist.astype(F32), -jnp.inf)
    sink = sinks.reshape(N_KV_HEADS, GROUP).T[:, :, None, None]
    bias = jnp.where(kj == 0, sink, bias) * LOG2_E
    return bias.reshape(GROUP, N_KV_HEADS * tq_rows, n_keys)


def _attn_prompt(q, k, v, sinks, *, batch, seq):
    t, d = q.shape
    blk = WINDOW
    nb = seq // blk
    rows = N_KV_HEADS * blk
    bias = jnp.stack([_attn_table(sinks, blk, 2 * blk, blk, True),
                      _attn_table(sinks, blk, 2 * blk, blk, False)])
    cur = lambda width: pl.BlockSpec((blk, width), lambda b, j: (b * nb + j, 0))
    prev = lambda width: pl.BlockSpec((blk, width), lambda b, j: (b * nb + jnp.maximum(j - 1, 0), 0))
    return pl.pallas_call(
        _attn_prompt_body,
        grid=(batch, nb),
        in_specs=[cur(d), cur(KV_DIM), prev(KV_DIM), cur(KV_DIM), prev(KV_DIM),
                  pl.BlockSpec((None, GROUP, rows, 2 * blk), lambda b, j: (jnp.minimum(j, 1), 0, 0, 0))],
        out_specs=cur(d),
        out_shape=jax.ShapeDtypeStruct((t, d), BF16),
        compiler_params=_cparams("arbitrary", "arbitrary"),
        name="attn_window_long",
    )(q, k, k, v, v, bias)


def _attn_sample_body(q_ref, kn_ref, vn_ref, kb_ref, vb_ref, bias_ref, o_ref, nk_ref, nv_ref):
    bs, tq, _ = q_ref.shape
    rows_g = N_KV_HEADS * tq
    rows = GROUP * rows_g
    own = _own_block(rows, tq)
    bias = bias_ref[...]
    not_sink = lax.broadcasted_iota(jnp.int32, (kb_ref.shape[1] + tq, KV_DIM), 0) > 0
    for b in range(bs):
        keys = jnp.concatenate([kb_ref[b], kn_ref[b]], axis=0)
        vals = jnp.concatenate([vb_ref[b], vn_ref[b]], axis=0)
        nk_ref[b] = keys[tq:]
        nv_ref[b] = vals[tq:]
        keys = jnp.where(not_sink, keys, 0.0).astype(BF16)
        vals = jnp.where(not_sink, vals, 0.0).astype(BF16)
        qb = q_ref[b].astype(F32)
        qbd = jnp.concatenate([qb[:, g * KV_DIM:(g + 1) * KV_DIM]
                               for g in range(GROUP) for _ in range(N_KV_HEADS)], axis=0)
        qbd = jnp.where(own, qbd, 0.0).astype(BF16)
        s = lax.dot_general(qbd, keys, (((1,), (1,)), ((), ())), preferred_element_type=F32)
        o = jnp.dot(_softmax_rows(s + bias), vals, preferred_element_type=F32)
        o = jnp.where(own, o, 0.0)
        outs = []
        for g in range(GROUP):
            out = o[g * rows_g:g * rows_g + tq]
            for kv in range(1, N_KV_HEADS):
                out = out + o[g * rows_g + kv * tq:g * rows_g + (kv + 1) * tq]
            outs.append(out)
        o_ref[b] = jnp.concatenate(outs, axis=1).astype(BF16)


def _attn_sample(q, k_new, v_new, k_buf, v_buf, sinks, *, bs):
    nseq, tq, d = q.shape
    wb = k_buf.shape[1]
    rows = N_HEADS * tq
    bias = _attn_table(sinks, tq, wb + tq, wb, False).reshape(rows, wb + tq)
    blk3 = lambda a, b_: pl.BlockSpec((bs, a, b_), lambda i: (i, 0, 0))
    return pl.pallas_call(
        _attn_sample_body,
        grid=(nseq // bs,),
        in_specs=[blk3(tq, d), blk3(tq, KV_DIM), blk3(tq, KV_DIM), blk3(wb, KV_DIM), blk3(wb, KV_DIM),
                  pl.BlockSpec((rows, wb + tq), lambda i: (0, 0))],
        out_specs=[blk3(tq, d), blk3(wb, KV_DIM), blk3(wb, KV_DIM)],
        out_shape=[jax.ShapeDtypeStruct((nseq, tq, d), BF16),
                   jax.ShapeDtypeStruct((nseq, wb, KV_DIM), F32),
                   jax.ShapeDtypeStruct((nseq, wb, KV_DIM), F32)],
        compiler_params=_cparams("arbitrary"),
        name="attn_window_short",
    )(q, k_new, v_new, k_buf, v_buf, bias)


N_POST_ACTS = 8


def _post_body(*refs, n_first):
    first, second = refs[:N_POST_ACTS], refs[N_POST_ACTS:2 * N_POST_ACTS]
    rest = refs[2 * N_POST_ACTS:]
    i = pl.program_id(0)

    @pl.when(i < n_first)
    def _():
        _post_tile(*first, *rest)

    @pl.when(i >= n_first)
    def _():
        _post_tile(*second, *rest)


def _post_tile(x_ref, ya_ref, yb_ref, ga_ref, gb_ref, g1_ref, sh2_ref, sc2_ref, wa_ref, wb_ref,
               wo_ref, bo_ref, nf_ref, wr_ref, br_ref, h_ref, hm_ref, gm_ref, km_ref):
    a = jnp.dot(ya_ref[...], wa_ref[...], preferred_element_type=F32)
    b = jnp.dot(yb_ref[...], wb_ref[...], preferred_element_type=F32)
    merged = ga_ref[...].astype(F32) * a + gb_ref[...].astype(F32) * b
    o = jnp.dot(merged.astype(BF16), wo_ref[...], preferred_element_type=F32) + bo_ref[...]
    h = x_ref[...] + g1_ref[...] * o
    h_ref[...] = h
    hm = _rmsnorm(h, nf_ref[...]) * (1.0 + sc2_ref[...]) + sh2_ref[...]
    hm_ref[...] = hm
    ne = br_ref.shape[1]
    hm_hi = hm.astype(BF16)
    hm_lo = (hm - hm_hi.astype(F32)).astype(BF16)
    wr = wr_ref[...]
    p_hi = jnp.dot(hm_hi, wr, preferred_element_type=F32)
    p_lo = jnp.dot(hm_lo, wr[:, :2 * ne], preferred_element_type=F32)
    logits = (p_hi[:, :ne] + (p_hi[:, ne:2 * ne] + p_lo[:, :ne])
              + (p_hi[:, 2 * ne:] + p_lo[:, ne:]) + br_ref[...])
    tm = logits.shape[0]
    lane = lax.broadcasted_iota(jnp.int32, (tm, ne), 1)
    vals = logits
    km = jnp.zeros((tm, ne), F32)
    top = []
    for k in range(TOP_K):
        m = jnp.max(vals, axis=-1, keepdims=True)
        first = jnp.min(jnp.where(vals == m, lane, ne), axis=-1, keepdims=True)
        sel = lane == first
        km = jnp.where(sel, float(k + 1), km)
        vals = jnp.where(sel, -jnp.inf, vals)
        top.append(m)
    e = [jnp.exp(t - top[0]) for t in top]
    den = e[0] + e[1] + e[2] + e[3]
    gm = jnp.zeros((tm, ne), F32)
    for k in range(TOP_K):
        gm = jnp.where(km == float(k + 1), e[k] / den, gm)
    gm_ref[...] = gm
    km_ref[...] = km


def _post(first, first_mod_specs, second, second_mod_specs, wa, wb, wo, bo, nf, wr, br, *, tm):
    t1, d = first[0].shape
    t2 = second[0].shape[0]
    n1, n2 = t1 // tm, t2 // tm
    ne = br.shape[1]
    row1 = pl.BlockSpec((tm, d), lambda i: (jnp.minimum(i, n1 - 1), 0))
    row2 = pl.BlockSpec((tm, d), lambda i: (jnp.maximum(i - n1, 0), 0))
    orow = lambda width: pl.BlockSpec((tm, width), lambda i: (i, 0))
    full = lambda a: pl.BlockSpec(a.shape, lambda i: (0,) * a.ndim)
    weights = [wa, wb, wo, bo, nf, wr, br]
    in_specs = ([row1] * 5 + list(first_mod_specs) + [row2] * 5 + list(second_mod_specs)
                + [full(w) for w in weights])
    return pl.pallas_call(
        functools.partial(_post_body, n_first=n1),
        grid=(n1 + n2,),
        in_specs=in_specs,
        out_specs=[orow(d), orow(d), orow(ne), orow(ne)],
        out_shape=[jax.ShapeDtypeStruct((t1 + t2, d), F32), jax.ShapeDtypeStruct((t1 + t2, d), F32),
                   jax.ShapeDtypeStruct((t1 + t2, ne), F32), jax.ShapeDtypeStruct((t1 + t2, ne), F32)],
        compiler_params=_cparams("arbitrary"),
        name="merge_outproj_router",
    )(*first, *second, *weights)


def _route_body(km_ref, dest_ref, cnt_ref, run_ref, start_ref, *, block):
    ph = pl.program_id(0)
    i = pl.program_id(1)
    km = km_ref[...]
    tm, ne = km.shape
    chosen = (km > 0.0).astype(F32)
    tile_cnt = jnp.sum(chosen, axis=0, keepdims=True)

    @pl.when((ph == 0) & (i == 0))
    def _():
        run_ref[...] = jnp.zeros_like(run_ref)

    @pl.when(ph == 0)
    def _():
        run_ref[...] += tile_cnt

    @pl.when((ph == 1) & (i == 0))
    def _():
        cnt = run_ref[...]
        cnt_ref[...] = cnt
        padded = jnp.ceil(cnt / block) * block
        r = lax.broadcasted_iota(jnp.int32, (ne, ne), 0)
        c = lax.broadcasted_iota(jnp.int32, (ne, ne), 1)
        before = (r < c).astype(F32)
        start_ref[...] = jnp.dot(padded, before, preferred_element_type=F32, precision=HIGHEST)
        run_ref[...] = jnp.zeros_like(run_ref)

    @pl.when(ph == 1)
    def _():
        r = lax.broadcasted_iota(jnp.int32, (tm, tm), 0)
        c = lax.broadcasted_iota(jnp.int32, (tm, tm), 1)
        earlier = (c < r).astype(BF16)
        rank = jnp.dot(earlier, chosen.astype(BF16), preferred_element_type=F32)
        slot = start_ref[...] + run_ref[...] + rank
        ones = jnp.ones((SUBLANES, ne), F32)
        for k in range(TOP_K):
            sk = jnp.where(km == float(k + 1), slot, 0.0)
            rowk = lax.dot_general(ones, sk, (((1,), (1,)), ((), ())),
                                   preferred_element_type=F32, precision=HIGHEST)
            dest_ref[k:k + 1, :] = rowk[0:1].astype(jnp.int32)
        run_ref[...] += tile_cnt


def _route(km, *, tm, block):
    t, ne = km.shape
    return pl.pallas_call(
        functools.partial(_route_body, block=block),
        grid=(2, t // tm),
        in_specs=[pl.BlockSpec((tm, ne), lambda p, i: (i, 0))],
        out_specs=[pl.BlockSpec((TOP_K, tm), lambda p, i: (0, i * p)),
                   pl.BlockSpec((1, ne), lambda p, i: (0, 0))],
        out_shape=[jax.ShapeDtypeStruct((TOP_K, t), jnp.int32),
                   jax.ShapeDtypeStruct((1, ne), F32)],
        scratch_shapes=[pltpu.VMEM((1, ne), F32), pltpu.VMEM((1, ne), F32)],
        compiler_params=_cparams("arbitrary", "arbitrary"),
        name="route_slots",
    )(km)


def _row_copy(src, src_row, dst, dst_row, sem):
    return pltpu.make_async_copy(src.at[pl.ds(src_row, 1), :], dst.at[pl.ds(dst_row, 1), :], sem)


def _dispatch_body(pend_ref, dest_ref, hm_ref, xs_ref, zero_ref, sem, zsem):
    tm = hm_ref.shape[0]
    block = zero_ref.shape[0]

    @pl.when(pl.program_id(0) == 0)
    def _():
        zero_ref[...] = jnp.zeros_like(zero_ref)
        n_exp = pend_ref.shape[0]
        n_slots = xs_ref.shape[0]
        fills = []
        for e in range(n_exp):
            end = pend_ref[e]
            begin = pend_ref[e - 1] if e > 0 else 0
            fills.append((end > begin, end - block))
        for j in range(n_exp):
            begin = pend_ref[n_exp - 1] + j * block
            fills.append((begin < n_slots, begin))
        for phase in ("start", "wait"):
            for cond, begin in fills:
                @pl.when(cond)
                def _():
                    fill = pltpu.make_async_copy(
                        zero_ref, xs_ref.at[pl.ds(pl.multiple_of(begin, block), block), :], zsem)
                    fill.start() if phase == "start" else fill.wait()

    def issue(c, carry):
        for u in range(ROW_DMA_UNROLL):
            t = c * ROW_DMA_UNROLL + u
            for k in range(TOP_K):
                _row_copy(hm_ref, t, xs_ref, dest_ref[k, t], sem).start(priority=k % 2)
        return carry

    lax.fori_loop(0, tm // ROW_DMA_UNROLL, issue, 0)

    def drain(c, carry):
        for u in range(ROW_DMA_UNROLL):
            for k in range(TOP_K):
                _row_copy(hm_ref, c * ROW_DMA_UNROLL + u, xs_ref, 0, sem).wait()
        return carry

    lax.fori_loop(0, tm // ROW_DMA_UNROLL, drain, 0)


def _dispatch(pend, dest, hm, *, tm, n_slots, block):
    t, d = hm.shape
    return pl.pallas_call(
        _dispatch_body,
        grid_spec=pltpu.PrefetchScalarGridSpec(
            num_scalar_prefetch=1,
            grid=(t // tm,),
            in_specs=[pl.BlockSpec((TOP_K, tm), lambda i, pe: (0, i), memory_space=pltpu.SMEM),
                      pl.BlockSpec((tm, d), lambda i, pe: (i, 0))],
            out_specs=pl.BlockSpec(memory_space=pl.ANY),
            scratch_shapes=[pltpu.VMEM((block, d), F32), pltpu.SemaphoreType.DMA,
                            pltpu.SemaphoreType.DMA],
        ),
        out_shape=jax.ShapeDtypeStruct((n_slots, d), F32),
        compiler_params=_cparams("arbitrary"),
        name="moe_dispatch_rows",
    )(pend, dest, hm)


PAIR_GROUP = 2 * LANES


def _expert_body(be_ref, nb_ref, xs_ref, wu_ref, bu_ref, wd_ref, bd_ref, ys_ref, wu_bf, wd_bf):
    i = pl.program_id(0)
    f2 = wu_ref.shape[1]
    new_expert = (i == 0) | (be_ref[i] != be_ref[jnp.maximum(i - 1, 0)])

    @pl.when((i < nb_ref[0]) & new_expert)
    def _():
        r = lax.broadcasted_iota(jnp.int32, (PAIR_GROUP, PAIR_GROUP), 0)
        c = lax.broadcasted_iota(jnp.int32, (PAIR_GROUP, PAIR_GROUP), 1)
        src = jnp.where(c < LANES, 2 * c, 2 * (c - LANES) + 1)
        perm = (r == src).astype(BF16)
        for g in range(f2 // PAIR_GROUP):
            cols = slice(g * PAIR_GROUP, (g + 1) * PAIR_GROUP)
            wu_bf[:, cols] = jnp.dot(wu_ref[:, cols].astype(BF16), perm,
                                     preferred_element_type=F32).astype(BF16)
        wd_bf[...] = wd_ref[...].astype(BF16)

    @pl.when(i < nb_ref[0])
    def _():
        x = xs_ref[...].astype(BF16)
        u = jnp.dot(x, wu_bf[...], preferred_element_type=F32) + bu_ref[...]
        acts = []
        for g in range(f2 // PAIR_GROUP):
            glu = jnp.minimum(u[:, g * PAIR_GROUP:g * PAIR_GROUP + LANES], SWIGLU_LIMIT)
            lin = jnp.clip(u[:, g * PAIR_GROUP + LANES:(g + 1) * PAIR_GROUP],
                           -SWIGLU_LIMIT, SWIGLU_LIMIT)
            acts.append((glu * jax.nn.sigmoid(SWIGLU_ALPHA * glu) * (lin + 1.0)).astype(BF16))
        act = jnp.concatenate(acts, axis=1)
        ys_ref[...] = jnp.dot(act, wd_bf[...], preferred_element_type=F32) + bd_ref[...]

    @pl.when(i >= nb_ref[0])
    def _():
        ys_ref[...] = jnp.zeros_like(ys_ref)


def _experts(block_e, n_used, xs, wu, bu, wd, bd, *, block):
    n_slots, d = xs.shape
    f2 = wu.shape[2]
    n_blocks = n_slots // block
    xrow = pl.BlockSpec((block, d), lambda i, be, nb: (jnp.maximum(jnp.minimum(i, nb[0] - 1), 0), 0))
    wspec = lambda a, b_: pl.BlockSpec((None, a, b_), lambda i, be, nb: (be[i], 0, 0))
    return pl.pallas_call(
        _expert_body,
        grid_spec=pltpu.PrefetchScalarGridSpec(
            num_scalar_prefetch=2,
            grid=(n_blocks,),
            in_specs=[xrow, wspec(d, f2), wspec(1, f2), wspec(f2 // 2, d), wspec(1, d)],
            out_specs=pl.BlockSpec((block, d), lambda i, be, nb: (i, 0)),
            scratch_shapes=[pltpu.VMEM((d, f2), BF16), pltpu.VMEM((f2 // 2, d), BF16)],
        ),
        out_shape=jax.ShapeDtypeStruct((n_slots, d), F32),
        compiler_params=_cparams("arbitrary"),
        name="moe_expert_mlp",
    )(block_e, n_used, xs, wu, bu, wd, bd)


def _final_body(dest_ref, h_ref, g2_ref, gm_ref, km_ref, nw_ref, ys_ref, o_ref, buf_ref, sem):
    tm = h_ref.shape[0]

    def issue(c, carry):
        for u in range(ROW_DMA_UNROLL):
            t = c * ROW_DMA_UNROLL + u
            for k in range(TOP_K):
                _row_copy(ys_ref, dest_ref[k, t], buf_ref.at[k], t, sem).start(priority=k % 2)
        return carry

    lax.fori_loop(0, tm // ROW_DMA_UNROLL, issue, 0)

    def drain(c, carry):
        for u in range(ROW_DMA_UNROLL):
            for k in range(TOP_K):
                _row_copy(ys_ref, 0, buf_ref.at[k], c * ROW_DMA_UNROLL + u, sem).wait()
        return carry

    lax.fori_loop(0, tm // ROW_DMA_UNROLL, drain, 0)

    km = km_ref[...]
    gm = gm_ref[...]
    acc = jnp.zeros(h_ref.shape, F32)
    for k in range(TOP_K):
        gate = jnp.sum(jnp.where(km == float(k + 1), gm, 0.0), axis=-1, keepdims=True)
        acc = acc + gate * buf_ref[k]
    y = h_ref[...] + g2_ref[...] * acc
    o_ref[...] = _rmsnorm(y, nw_ref[...])


def _final(dest, h, g2, g2_spec, gm, km, norm_w, ys, *, tm, tile0, n_tiles):
    d = h.shape[1]
    ne = gm.shape[1]
    row = lambda width: pl.BlockSpec((tm, width), lambda i: (i + tile0, 0))
    return pl.pallas_call(
        _final_body,
        grid=(n_tiles,),
        in_specs=[pl.BlockSpec((TOP_K, tm), lambda i: (0, i + tile0), memory_space=pltpu.SMEM),
                  row(d), g2_spec, row(ne), row(ne),
                  pl.BlockSpec(norm_w.shape, lambda i: (0, 0)),
                  pl.BlockSpec(memory_space=pl.ANY)],
        out_specs=pl.BlockSpec((tm, d), lambda i: (i, 0)),
        out_shape=jax.ShapeDtypeStruct((n_tiles * tm, d), F32),
        scratch_shapes=[pltpu.VMEM((TOP_K, tm, d), F32), pltpu.SemaphoreType.DMA],
        compiler_params=_cparams("arbitrary"),
        name="moe_combine_final_norm",
    )(dest, h, g2, gm, km, norm_w, ys)


def kernel(x_prompt, x_sample, cache_k_win, cache_v_win, state_conv, c_prompt, c_sample, w_ada, b_ada, norm_mix, norm_ffn, w_in, b_in, conv_w, sinks, w_branch_a, w_branch_b, w_out, b_out, w_router, b_router, w_up, b_up, w_down, b_down, norm_final):
    batch, seq, d = x_prompt.shape
    nseq, tq, _ = x_sample.shape
    depth = w_ada.shape[0]
    assert depth == 1 and tq == SUBLANES and seq % TOKEN_TILE == 0
    wb = cache_k_win.shape[2]
    tp, ts = batch * seq, nseq * tq
    t_all = tp + ts
    tm_s = min(ts, TOKEN_TILE)
    assert ts % tm_s == 0 and tp % tm_s == 0 and t_all % ROW_DMA_TILE == 0
    nmod = w_ada.shape[2] // d

    n_c = batch + nseq
    c_rows = -(-n_c // SUBLANES) * SUBLANES
    c_all = jnp.concatenate([c_prompt, c_sample, jnp.zeros((c_rows - n_c, d), F32)], axis=0)
    mod = _ada(c_all, w_ada[0], b_ada[0])
    mod_p = mod[:batch].reshape(batch * nmod, 1, d)
    mod_s = jnp.repeat(mod[batch:n_c], tq, axis=0)
    tiles_per_seq = seq // TOKEN_TILE
    p_spec = lambda chunk: pl.BlockSpec((None, 1, d), lambda i: ((i // tiles_per_seq) * nmod + chunk, 0, 0))
    s_spec = lambda chunk: pl.BlockSpec((tm_s, d), lambda i: (i, chunk))

    q_lo = 3 * d
    q_hi = q_lo + N_HEADS * HEAD_DIM

    def q_cols(a):
        lead = a.shape[0]
        qa = a[:, q_lo:q_hi].reshape(lead, N_KV_HEADS, GROUP, HEAD_DIM).transpose(0, 2, 1, 3)
        qa = qa.reshape(lead, q_hi - q_lo) * (HEAD_DIM ** -0.5 * LOG2_E)
        return jnp.concatenate([a[:, :q_lo], qa, a[:, q_hi:]], axis=1)

    w_in_b = q_cols(w_in[0]).astype(BF16)
    b_in2 = q_cols(b_in[0].reshape(1, -1))
    nmix = norm_mix[0].reshape(1, d)
    cw = conv_w[0]

    xp = x_prompt.reshape(tp, d)
    ya_p, q_p, k_p, v_p, ga_p, gb_p, ut_p = _inproj(
        xp, mod_p, mod_p, (p_spec(0), p_spec(1)), nmix, w_in_b, b_in2, cw,
        tm=TOKEN_TILE, tiles_per_seq=tiles_per_seq)
    xs_ = x_sample.reshape(ts, d)
    st = state_conv[0]
    zpad = lambda a: jnp.pad(a, ((0, 0), (0, tq - a.shape[1]), (0, 0))).reshape(ts, d)
    hist1 = zpad(st[:, 1:2])
    hist2 = zpad(st)
    ya_s, q_s, k_s, v_s, ga_s, gb_s, u_s = _inproj(
        xs_, mod_s, mod_s, (s_spec(0), s_spec(1)), nmix, w_in_b, b_in2, cw,
        tm=tm_s, tiles_per_seq=0, hist=(hist1, hist2))

    sk = sinks[0]
    yb_p = _attn_prompt(q_p, k_p, v_p, sk, batch=batch, seq=seq)
    kbuf = cache_k_win[0].reshape(nseq, wb, KV_DIM)
    vbuf = cache_v_win[0].reshape(nseq, wb, KV_DIM)
    yb_s, nk_s, nv_s = _attn_sample(
        q_s.reshape(nseq, tq, d), k_s.reshape(nseq, tq, KV_DIM), v_s.reshape(nseq, tq, KV_DIM),
        kbuf, vbuf, sk, bs=SUBLANES)
    yb_s = yb_s.reshape(ts, d)

    wa = w_branch_a[0].astype(BF16)
    wbb = w_branch_b[0].reshape(N_KV_HEADS, GROUP, HEAD_DIM, d).transpose(1, 0, 2, 3)
    wbb = wbb.reshape(N_HEADS * HEAD_DIM, d).astype(BF16)
    wo = w_out[0].astype(BF16)
    bo = b_out[0].reshape(1, d)
    nffn = norm_ffn[0].reshape(1, d)
    wr0 = w_router[0].astype(BF16)
    wr1 = (w_router[0] - wr0.astype(F32)).astype(BF16)
    wr2 = (w_router[0] - wr0.astype(F32) - wr1.astype(F32)).astype(BF16)
    wr = jnp.concatenate([wr0, wr1, wr2], axis=1)
    br = b_router[0].reshape(1, -1)
    n_tiles_p = tp // tm_s
    pp_spec = lambda chunk: pl.BlockSpec(
        (None, 1, d),
        lambda i: ((jnp.minimum(i, n_tiles_p - 1) // (seq // tm_s)) * nmod + chunk, 0, 0))
    sp_spec = lambda chunk: pl.BlockSpec((tm_s, d), lambda i: (jnp.maximum(i - n_tiles_p, 0), chunk))
    h_all, hm_all, gm_all, km_all = _post(
        (xp, ya_p, yb_p, ga_p, gb_p, mod_p, mod_p, mod_p), (pp_spec(2), pp_spec(3), pp_spec(4)),
        (xs_, ya_s, yb_s, ga_s, gb_s, mod_s, mod_s, mod_s), (sp_spec(2), sp_spec(3), sp_spec(4)),
        wa, wbb, wo, bo, nffn, wr, br, tm=tm_s)

    block = EXPERT_BLOCK
    dest, counts = _route(km_all, tm=tm_s, block=block)
    n_assign = t_all * TOP_K
    n_blocks = -(-(n_assign + N_EXPERTS * (block - 1)) // block)
    cnt = counts[0].astype(jnp.int32)
    pend = jnp.cumsum((cnt + block - 1) // block * block)
    n_used = (pend[-1] // block).astype(jnp.int32)
    blk_ids = jnp.minimum(jnp.arange(n_blocks, dtype=jnp.int32), n_used - 1)
    block_e = jnp.sum((pend[None, :] <= (blk_ids * block)[:, None]).astype(jnp.int32), axis=1)
    block_e = jnp.minimum(block_e, N_EXPERTS - 1)

    xs_rows = _dispatch(pend.astype(jnp.int32), dest, hm_all, tm=ROW_DMA_TILE,
                        n_slots=n_blocks * block, block=block)
    f2 = w_up.shape[3]
    bu = b_up[0].reshape(N_EXPERTS, f2 // PAIR_GROUP, LANES, 2).transpose(0, 1, 3, 2)
    bu = bu.reshape(N_EXPERTS, 1, f2)
    bd = b_down[0].reshape(N_EXPERTS, 1, d)
    ys = _experts(block_e, n_used.reshape(1), xs_rows, w_up[0], bu, w_down[0], bd, block=block)

    nfin = norm_final.reshape(1, d)
    tiles_p = tp // ROW_DMA_TILE
    pf_spec = pl.BlockSpec((None, 1, d),
                           lambda i: ((i // (seq // ROW_DMA_TILE)) * nmod + 5, 0, 0))
    y_p = _final(dest, h_all, mod_p, pf_spec, gm_all, km_all, nfin, ys,
                 tm=ROW_DMA_TILE, tile0=0, n_tiles=tiles_p)
    sf_spec = pl.BlockSpec((ROW_DMA_TILE, d), lambda i: (i, 5))
    y_s = _final(dest, h_all, mod_s, sf_spec, gm_all, km_all, nfin, ys,
                 tm=ROW_DMA_TILE, tile0=tiles_p, n_tiles=ts // ROW_DMA_TILE)

    keep = min(WINDOW, seq)
    k4 = k_p.reshape(batch, seq, N_KV_HEADS, HEAD_DIM)[:, seq - keep:]
    v4 = v_p.reshape(batch, seq, N_KV_HEADS, HEAD_DIM)[:, seq - keep:]
    conv_p = ut_p.reshape(batch, tiles_per_seq, SUBLANES, d)[:, -1, SUBLANES - (CONV_WIDTH - 1):]
    conv_s = u_s.reshape(nseq, tq, d)[:, tq - (CONV_WIDTH - 1):]
    return (y_p.reshape(batch, seq, d), y_s.reshape(nseq, tq, d),
            k4[None], v4[None], conv_p[None],
            nk_s.reshape(1, nseq, wb, N_KV_HEADS, HEAD_DIM),
            nv_s.reshape(1, nseq, wb, N_KV_HEADS, HEAD_DIM), conv_s[None])
```

```python
import functools

import jax
import jax.numpy as jnp
from jax import lax
from jax.experimental import pallas as pl
from jax.experimental.pallas import tpu as pltpu

F32 = jnp.float32
BF16 = jnp.bfloat16
HIGHEST = lax.Precision.HIGHEST

N_HEADS = 16
N_KV_HEADS = 4
HEAD_DIM = 64
GROUP = N_HEADS // N_KV_HEADS
KV_DIM = N_KV_HEADS * HEAD_DIM
WINDOW = 128
N_EXPERTS = 32
TOP_K = 4
CONV_WIDTH = 3
SWIGLU_ALPHA = 1.702
SWIGLU_LIMIT = 7.0
RMS_EPS = 1e-5
ALIBI_SLOPES = tuple(2.0 ** (-8.0 * (h + 1) / N_HEADS) for h in range(N_HEADS))

SUBLANES = 8
LANES = 128
VMEM_LIMIT_BYTES = 56 * 1024 * 1024
TOKEN_TILE = 512
ROW_DMA_TILE = 256
ROW_DMA_UNROLL = 8
EXPERT_BLOCK = 512


def _cparams(*sem):
    return pltpu.CompilerParams(dimension_semantics=sem, vmem_limit_bytes=VMEM_LIMIT_BYTES)


def _rmsnorm(x, g):
    ms = jnp.mean(x * x, axis=-1, keepdims=True)
    return x * lax.rsqrt(ms + RMS_EPS) * g


def _ada_body(c_ref, w_ref, b_ref, o_ref):
    c = c_ref[...]
    a = c * jax.nn.sigmoid(c)
    o_ref[...] = jnp.dot(a, w_ref[...], preferred_element_type=F32, precision=HIGHEST) + b_ref[...]


def _ada(c_all, w_ada, b_ada):
    rows, d = c_all.shape
    n = w_ada.shape[1]
    return pl.pallas_call(
        _ada_body,
        grid=(n // d,),
        in_specs=[pl.BlockSpec((rows, d), lambda j: (0, 0)),
                  pl.BlockSpec((d, d), lambda j: (0, j)),
                  pl.BlockSpec((1, d), lambda j: (0, j))],
        out_specs=pl.BlockSpec((rows, d), lambda j: (0, j)),
        out_shape=jax.ShapeDtypeStruct((rows, n), F32),
        compiler_params=_cparams("arbitrary"),
        name="ada_modulation",
    )(c_all, w_ada, b_ada.reshape(1, n))


def _inproj_body(*refs, d, nc, tiles_per_seq):
    carry_mode = tiles_per_seq > 0
    if carry_mode:
        (x_ref, sh_ref, sc_ref, nw_ref, w_ref, b_ref, cw_ref,
         ya_ref, q_ref, k_ref, v_ref, ga_ref, gb_ref, ut_ref, carry_ref) = refs
    else:
        (x_ref, sh_ref, sc_ref, nw_ref, w_ref, b_ref, cw_ref, p1_ref, p2_ref,
         ya_ref, q_ref, k_ref, v_ref, ga_ref, gb_ref, ut_ref) = refs
    tm = x_ref.shape[0]
    xm = (_rmsnorm(x_ref[...], nw_ref[...]) * (1.0 + sc_ref[...]) + sh_ref[...]).astype(BF16)

    def proj(lo, width):
        return (jnp.dot(xm, w_ref[:, lo:lo + width], preferred_element_type=F32)
                + b_ref[:, lo:lo + width])

    if carry_mode:
        @pl.when(pl.program_id(0) % tiles_per_seq == 0)
        def _():
            carry_ref[...] = jnp.zeros_like(carry_ref)
    else:
        t_in_seq = lax.broadcasted_iota(jnp.int32, (tm, nc), 0) % SUBLANES

    for j in range(d // nc):
        c0 = j * nc
        u = proj(c0, nc) * proj(2 * d + c0, nc)
        if carry_mode:
            ue = jnp.concatenate([carry_ref[:, c0:c0 + nc], u], axis=0)
            s1 = pltpu.roll(ue, 1, axis=0)[SUBLANES:]
            s2 = pltpu.roll(ue, 2, axis=0)[SUBLANES:]
            carry_ref[:, c0:c0 + nc] = u[tm - SUBLANES:]
            ut_ref[:, c0:c0 + nc] = u[tm - SUBLANES:]
        else:
            s1 = jnp.where(t_in_seq >= 1, pltpu.roll(u, 1, axis=0), p1_ref[:, c0:c0 + nc])
            s2 = jnp.where(t_in_seq >= 2, pltpu.roll(u, 2, axis=0), p2_ref[:, c0:c0 + nc])
            ut_ref[:, c0:c0 + nc] = u
        cw = cw_ref[:, c0:c0 + nc]
        conv = s2 * cw[0:1] + s1 * cw[1:2] + u * cw[2:3]
        ya_ref[:, c0:c0 + nc] = (proj(d + c0, nc) * conv).astype(BF16)

    q_ref[...] = proj(3 * d, d).astype(BF16)
    k_ref[...] = proj(4 * d, KV_DIM)
    v_ref[...] = proj(4 * d + KV_DIM, KV_DIM)
    ga_ref[...] = jax.nn.sigmoid(proj(4 * d + 2 * KV_DIM, d)).astype(BF16)
    gb_ref[...] = jax.nn.sigmoid(proj(5 * d + 2 * KV_DIM, d)).astype(BF16)


def _inproj(x, shift, scale, shift_scale_specs, norm_w, w_in, b_in, conv_w, *, tm, tiles_per_seq,
            hist=None):
    t, d = x.shape
    n_in = w_in.shape[1]
    nc = 512
    carry_mode = tiles_per_seq > 0
    row = lambda width: pl.BlockSpec((tm, width), lambda i: (i, 0))
    full = lambda a: pl.BlockSpec(a.shape, lambda i: (0,) * a.ndim)
    in_specs = [row(d), shift_scale_specs[0], shift_scale_specs[1], full(norm_w), full(w_in),
                full(b_in), full(conv_w)]
    args = [x, shift, scale, norm_w, w_in, b_in, conv_w]
    if not carry_mode:
        in_specs += [row(d), row(d)]
        args += list(hist)
    ut_rows = SUBLANES if carry_mode else tm
    out_specs = [row(d), row(d), row(KV_DIM), row(KV_DIM), row(d), row(d),
                 pl.BlockSpec((ut_rows, d), lambda i: (i, 0))]
    out_shape = [jax.ShapeDtypeStruct((t, d), BF16), jax.ShapeDtypeStruct((t, d), BF16),
                 jax.ShapeDtypeStruct((t, KV_DIM), F32), jax.ShapeDtypeStruct((t, KV_DIM), F32),
                 jax.ShapeDtypeStruct((t, d), BF16), jax.ShapeDtypeStruct((t, d), BF16),
                 jax.ShapeDtypeStruct((t // tm * ut_rows, d), F32)]
    return pl.pallas_call(
        functools.partial(_inproj_body, d=d, nc=nc, tiles_per_seq=tiles_per_seq),
        grid=(t // tm,),
        in_specs=in_specs,
        out_specs=out_specs,
        out_shape=out_shape,
        scratch_shapes=[pltpu.VMEM((SUBLANES, d), F32)] if carry_mode else [],
        compiler_params=_cparams("arbitrary"),
        name="inproj_conv_long" if carry_mode else "inproj_conv_short",
    )(*args)


LOG2_E = 1.4426950408889634


def _softmax_rows(s):
    m = jnp.max(s, axis=-1, keepdims=True)
    p = jnp.exp2(s - m)
    den = jnp.sum(p, axis=-1, keepdims=True)
    return (p * (1.0 / den)).astype(BF16)


def _own_block(rows, rows_per_kv):
    r = lax.broadcasted_iota(jnp.int32, (rows, KV_DIM), 0)
    c = lax.broadcasted_iota(jnp.int32, (rows, KV_DIM), 1)
    return (c // HEAD_DIM) == (r // rows_per_kv) % N_KV_HEADS


def _attn_prompt_body(q_ref, kc_ref, kp_ref, vc_ref, vp_ref, bias_ref, o_ref):
    blk = q_ref.shape[0]
    kk = jnp.concatenate([kp_ref[...], kc_ref[...]], axis=0)
    vv = jnp.concatenate([vp_ref[...], vc_ref[...]], axis=0)
    col_kv = lax.broadcasted_iota(jnp.int32, kk.shape, 1) // HEAD_DIM
    not_sink = lax.broadcasted_iota(jnp.int32, kk.shape, 0) > 0
    kk_kv = [jnp.where((col_kv == kv) & not_sink, kk, 0.0).astype(BF16)
             for kv in range(N_KV_HEADS)]
    vv_kv = [jnp.where((col_kv == kv) & not_sink, vv, 0.0).astype(BF16)
             for kv in range(N_KV_HEADS)]

    def scores(g):
        a = q_ref[:, g * KV_DIM:(g + 1) * KV_DIM]
        return [lax.dot_general(a, kk_kv[kv], (((1,), (1,)), ((), ())),
                                preferred_element_type=F32)
                + bias_ref[g, kv * blk:(kv + 1) * blk, :] for kv in range(N_KV_HEADS)]

    def weighted_values(g, probs):
        out = jnp.dot(probs[0], vv_kv[0], preferred_element_type=F32)
        for kv in range(1, N_KV_HEADS):
            out = out + jnp.dot(probs[kv], vv_kv[kv], preferred_element_type=F32)
        o_ref[:, g * KV_DIM:(g + 1) * KV_DIM] = out.astype(BF16)

    s_next = scores(0)
    p_prev = None
    for g in range(GROUP):
        s_cur = s_next
        if g + 1 < GROUP:
            s_next = scores(g + 1)
        p_cur = [_softmax_rows(s) for s in s_cur]
        if p_prev is not None:
            weighted_values(g - 1, p_prev)
        p_prev = p_cur
    weighted_values(GROUP - 1, p_prev)


def _attn_table(sinks, tq_rows, n_keys, key_offset, first_block):
    assert key_offset >= WINDOW
    kv = jnp.arange(N_KV_HEADS)[None, :, None, None]
    g = jnp.arange(GROUP)[:, None, None, None]
    qi = jnp.arange(tq_rows)[None, None, :, None]
    kj = jnp.arange(n_keys)[None, None, None, :]
    slope = jnp.exp2(-8.0 * (kv * GROUP + g + 1).astype(F32) / N_HEADS)
    dist = qi + key_offset - kj
    valid = (dist >= 0) & (dist < WINDOW)
    if first_block:
        valid = valid & (kj >= key_offset)
    bias = jnp.where(valid, -slope * dist.astype(F32), -jnp.inf)
    sink = sinks.reshape(N_KV_HEADS, GROUP).T[:, :, None, None]
    bias = jnp.where(kj == 0, sink, bias) * LOG2_E
    return bias.reshape(GROUP, N_KV_HEADS * tq_rows, n_keys)


def _attn_prompt(q, k, v, sinks, *, batch, seq):
    t, d = q.shape
    blk = WINDOW
    nb = seq // blk
    rows = N_KV_HEADS * blk
    bias = jnp.stack([_attn_table(sinks, blk, 2 * blk, blk, True),
                      _attn_table(sinks, blk, 2 * blk, blk, False)])
    cur = lambda width: pl.BlockSpec((blk, width), lambda b, j: (b * nb + j, 0))
    prev = lambda width: pl.BlockSpec((blk, width), lambda b, j: (b * nb + jnp.maximum(j - 1, 0), 0))
    return pl.pallas_call(
        _attn_prompt_body,
        grid=(batch, nb),
        in_specs=[cur(d), cur(KV_DIM), prev(KV_DIM), cur(KV_DIM), prev(KV_DIM),
                  pl.BlockSpec((None, GROUP, rows, 2 * blk), lambda b, j: (jnp.minimum(j, 1), 0, 0, 0))],
        out_specs=cur(d),
        out_shape=jax.ShapeDtypeStruct((t, d), BF16),
        compiler_params=_cparams("arbitrary", "arbitrary"),
        name="attn_window_long",
    )(q, k, k, v, v, bias)


def _attn_sample_body(q_ref, kn_ref, vn_ref, kb_ref, vb_ref, bias_ref, o_ref, nk_ref, nv_ref):
    bs, tq, _ = q_ref.shape
    rows_g = N_KV_HEADS * tq
    rows = GROUP * rows_g
    own = _own_block(rows, tq)
    bias = bias_ref[...]
    not_sink = lax.broadcasted_iota(jnp.int32, (kb_ref.shape[1] + tq, KV_DIM), 0) > 0
    for b in range(bs):
        keys = jnp.concatenate([kb_ref[b], kn_ref[b]], axis=0)
        vals = jnp.concatenate([vb_ref[b], vn_ref[b]], axis=0)
        nk_ref[b] = keys[tq:]
        nv_ref[b] = vals[tq:]
        keys = jnp.where(not_sink, keys, 0.0).astype(BF16)
        vals = jnp.where(not_sink, vals, 0.0).astype(BF16)
        qb = q_ref[b].astype(F32)
        qbd = jnp.concatenate([qb[:, g * KV_DIM:(g + 1) * KV_DIM]
                               for g in range(GROUP) for _ in range(N_KV_HEADS)], axis=0)
        qbd = jnp.where(own, qbd, 0.0).astype(BF16)
        s = lax.dot_general(qbd, keys, (((1,), (1,)), ((), ())), preferred_element_type=F32)
        o = jnp.dot(_softmax_rows(s + bias), vals, preferred_element_type=F32)
        o = jnp.where(own, o, 0.0)
        outs = []
        for g in range(GROUP):
            out = o[g * rows_g:g * rows_g + tq]
            for kv in range(1, N_KV_HEADS):
                out = out + o[g * rows_g + kv * tq:g * rows_g + (kv + 1) * tq]
            outs.append(out)
        o_ref[b] = jnp.concatenate(outs, axis=1).astype(BF16)


def _attn_sample(q, k_new, v_new, k_buf, v_buf, sinks, *, bs):
    nseq, tq, d = q.shape
    wb = k_buf.shape[1]
    rows = N_HEADS * tq
    bias = _attn_table(sinks, tq, wb + tq, wb, False).reshape(rows, wb + tq)
    blk3 = lambda a, b_: pl.BlockSpec((bs, a, b_), lambda i: (i, 0, 0))
    return pl.pallas_call(
        _attn_sample_body,
        grid=(nseq // bs,),
        in_specs=[blk3(tq, d), blk3(tq, KV_DIM), blk3(tq, KV_DIM), blk3(wb, KV_DIM), blk3(wb, KV_DIM),
                  pl.BlockSpec((rows, wb + tq), lambda i: (0, 0))],
        out_specs=[blk3(tq, d), blk3(wb, KV_DIM), blk3(wb, KV_DIM)],
        out_shape=[jax.ShapeDtypeStruct((nseq, tq, d), BF16),
                   jax.ShapeDtypeStruct((nseq, wb, KV_DIM), F32),
                   jax.ShapeDtypeStruct((nseq, wb, KV_DIM), F32)],
        compiler_params=_cparams("arbitrary"),
        name="attn_window_short",
    )(q, k_new, v_new, k_buf, v_buf, bias)


N_POST_ACTS = 8


def _post_body(*refs, n_first):
    first, second = refs[:N_POST_ACTS], refs[N_POST_ACTS:2 * N_POST_ACTS]
    rest = refs[2 * N_POST_ACTS:]
    i = pl.program_id(0)

    @pl.when(i < n_first)
    def _():
        _post_tile(*first, *rest)

    @pl.when(i >= n_first)
    def _():
        _post_tile(*second, *rest)


def _post_tile(x_ref, ya_ref, yb_ref, ga_ref, gb_ref, g1_ref, sh2_ref, sc2_ref, wa_ref, wb_ref,
               wo_ref, bo_ref, nf_ref, wr_ref, br_ref, h_ref, hm_ref, gm_ref, km_ref):
    a = jnp.dot(ya_ref[...], wa_ref[...], preferred_element_type=F32)
    b = jnp.dot(yb_ref[...], wb_ref[...], preferred_element_type=F32)
    merged = ga_ref[...].astype(F32) * a + gb_ref[...].astype(F32) * b
    o = jnp.dot(merged.astype(BF16), wo_ref[...], preferred_element_type=F32) + bo_ref[...]
    h = x_ref[...] + g1_ref[...] * o
    h_ref[...] = h
    hm = _rmsnorm(h, nf_ref[...]) * (1.0 + sc2_ref[...]) + sh2_ref[...]
    for c in range(hm.shape[1] // LANES):
        hm_ref[_chunk(c, hm.shape[0]), :] = hm[:, c * LANES:(c + 1) * LANES]
    ne = br_ref.shape[1]
    hm_hi = hm.astype(BF16)
    hm_lo = (hm - hm_hi.astype(F32)).astype(BF16)
    wr = wr_ref[...]
    p_hi = jnp.dot(hm_hi, wr, preferred_element_type=F32)
    p_lo = jnp.dot(hm_lo, wr[:, :2 * ne], preferred_element_type=F32)
    logits = (p_hi[:, :ne] + (p_hi[:, ne:2 * ne] + p_lo[:, :ne])
              + (p_hi[:, 2 * ne:] + p_lo[:, ne:]) + br_ref[...])
    tm = logits.shape[0]
    lane = lax.broadcasted_iota(jnp.int32, (tm, ne), 1)
    vals = logits
    km = jnp.zeros((tm, ne), F32)
    top = []
    for k in range(TOP_K):
        m = jnp.max(vals, axis=-1, keepdims=True)
        first = jnp.min(jnp.where(vals == m, lane, ne), axis=-1, keepdims=True)
        sel = lane == first
        km = jnp.where(sel, float(k + 1), km)
        vals = jnp.where(sel, -jnp.inf, vals)
        top.append(m)
    e = [jnp.exp(t - top[0]) for t in top]
    den = e[0] + e[1] + e[2] + e[3]
    gm = jnp.zeros((tm, ne), F32)
    for k in range(TOP_K):
        gm = jnp.where(km == float(k + 1), e[k] / den, gm)
    gm_ref[...] = gm
    km_ref[...] = km


def _post(first, first_mod_specs, second, second_mod_specs, wa, wb, wo, bo, nf, wr, br, *, tm):
    t1, d = first[0].shape
    t2 = second[0].shape[0]
    n1, n2 = t1 // tm, t2 // tm
    ne = br.shape[1]
    row1 = pl.BlockSpec((tm, d), lambda i: (jnp.minimum(i, n1 - 1), 0))
    row2 = pl.BlockSpec((tm, d), lambda i: (jnp.maximum(i - n1, 0), 0))
    orow = lambda width: pl.BlockSpec((tm, width), lambda i: (i, 0))
    full = lambda a: pl.BlockSpec(a.shape, lambda i: (0,) * a.ndim)
    weights = [wa, wb, wo, bo, nf, wr, br]
    in_specs = ([row1] * 5 + list(first_mod_specs) + [row2] * 5 + list(second_mod_specs)
                + [full(w) for w in weights])
    return pl.pallas_call(
        functools.partial(_post_body, n_first=n1),
        grid=(n1 + n2,),
        in_specs=in_specs,
        out_specs=[orow(d), pl.BlockSpec((tm * d // LANES, LANES), lambda i: (i, 0)),
                   orow(ne), orow(ne)],
        out_shape=[jax.ShapeDtypeStruct((t1 + t2, d), F32),
                   jax.ShapeDtypeStruct(((t1 + t2) * d // LANES, LANES), F32),
                   jax.ShapeDtypeStruct((t1 + t2, ne), F32), jax.ShapeDtypeStruct((t1 + t2, ne), F32)],
        compiler_params=_cparams("arbitrary"),
        name="merge_outproj_router",
    )(*first, *second, *weights)


def _route_body(km_ref, dest_ref, cnt_ref, run_ref, start_ref, *, block):
    ph = pl.program_id(0)
    i = pl.program_id(1)
    km = km_ref[...]
    tm, ne = km.shape
    chosen = (km > 0.0).astype(F32)
    tile_cnt = jnp.sum(chosen, axis=0, keepdims=True)

    @pl.when((ph == 0) & (i == 0))
    def _():
        run_ref[...] = jnp.zeros_like(run_ref)

    @pl.when(ph == 0)
    def _():
        run_ref[...] += tile_cnt

    @pl.when((ph == 1) & (i == 0))
    def _():
        cnt = run_ref[...]
        cnt_ref[...] = cnt
        padded = jnp.ceil(cnt / block) * block
        r = lax.broadcasted_iota(jnp.int32, (ne, ne), 0)
        c = lax.broadcasted_iota(jnp.int32, (ne, ne), 1)
        before = (r < c).astype(F32)
        start_ref[...] = jnp.dot(padded, before, preferred_element_type=F32, precision=HIGHEST)
        run_ref[...] = jnp.zeros_like(run_ref)

    @pl.when(ph == 1)
    def _():
        r = lax.broadcasted_iota(jnp.int32, (tm, tm), 0)
        c = lax.broadcasted_iota(jnp.int32, (tm, tm), 1)
        earlier = (c < r).astype(BF16)
        rank = jnp.dot(earlier, chosen.astype(BF16), preferred_element_type=F32)
        slot = start_ref[...] + run_ref[...] + rank
        ones = jnp.ones((SUBLANES, ne), F32)
        for k in range(TOP_K):
            sk = jnp.where(km == float(k + 1), slot, 0.0)
            rowk = lax.dot_general(ones, sk, (((1,), (1,)), ((), ())),
                                   preferred_element_type=F32, precision=HIGHEST)
            dest_ref[k:k + 1, :] = rowk[0:1].astype(jnp.int32)
        run_ref[...] += tile_cnt


def _route(km, *, tm, block):
    t, ne = km.shape
    return pl.pallas_call(
        functools.partial(_route_body, block=block),
        grid=(2, t // tm),
        in_specs=[pl.BlockSpec((tm, ne), lambda p, i: (i, 0))],
        out_specs=[pl.BlockSpec((TOP_K, tm), lambda p, i: (0, i * p)),
                   pl.BlockSpec((1, ne), lambda p, i: (0, 0))],
        out_shape=[jax.ShapeDtypeStruct((TOP_K, t), jnp.int32),
                   jax.ShapeDtypeStruct((1, ne), F32)],
        scratch_shapes=[pltpu.VMEM((1, ne), F32), pltpu.VMEM((1, ne), F32)],
        compiler_params=_cparams("arbitrary", "arbitrary"),
        name="route_slots",
    )(km)


ROW_TILES = SUBLANES


def _row_copy(src, src_row, dst, dst_row, sem):
    src_at = pl.ds(pl.multiple_of(src_row * ROW_TILES, ROW_TILES), ROW_TILES)
    dst_at = pl.ds(pl.multiple_of(dst_row * ROW_TILES, ROW_TILES), ROW_TILES)
    return pltpu.make_async_copy(src.at[src_at, :], dst.at[dst_at, :], sem)


def _chunk(c, rows):
    return pl.ds(c, rows, stride=ROW_TILES)


def _dispatch_body(pend_ref, dest_ref, hm_ref, xs_ref, zero_ref, sem, zsem):
    tm = hm_ref.shape[0] // ROW_TILES
    block = zero_ref.shape[0] // ROW_TILES

    @pl.when(pl.program_id(0) == 0)
    def _():
        zero_ref[...] = jnp.zeros_like(zero_ref)
        n_exp = pend_ref.shape[0]
        n_slots = xs_ref.shape[0] // ROW_TILES
        fills = []
        for e in range(n_exp):
            end = pend_ref[e]
            begin = pend_ref[e - 1] if e > 0 else 0
            fills.append((end > begin, end - block))
        for j in range(n_exp):
            begin = pend_ref[n_exp - 1] + j * block
            fills.append((begin < n_slots, begin))
        for phase in ("start", "wait"):
            for cond, begin in fills:
                @pl.when(cond)
                def _():
                    rows = block * ROW_TILES
                    fill = pltpu.make_async_copy(
                        zero_ref,
                        xs_ref.at[pl.ds(pl.multiple_of(begin * ROW_TILES, rows), rows), :], zsem)
                    fill.start() if phase == "start" else fill.wait()

    def issue(c, carry):
        for u in range(ROW_DMA_UNROLL):
            t = c * ROW_DMA_UNROLL + u
            for k in range(TOP_K):
                _row_copy(hm_ref, t, xs_ref, dest_ref[k * tm + t], sem).start(priority=k % 2)
        return carry

    lax.fori_loop(0, tm // ROW_DMA_UNROLL, issue, 0)

    def drain(c, carry):
        for u in range(ROW_DMA_UNROLL):
            for k in range(TOP_K):
                _row_copy(hm_ref, c * ROW_DMA_UNROLL + u, xs_ref, 0, sem).wait()
        return carry

    lax.fori_loop(0, tm // ROW_DMA_UNROLL, drain, 0)


def _dispatch(pend, dest, hm, *, tm, n_slots, block):
    t = hm.shape[0] // ROW_TILES
    return pl.pallas_call(
        _dispatch_body,
        grid_spec=pltpu.PrefetchScalarGridSpec(
            num_scalar_prefetch=1,
            grid=(t // tm,),
            in_specs=[pl.BlockSpec((TOP_K * tm,), lambda i, pe: (i,), memory_space=pltpu.SMEM),
                      pl.BlockSpec((tm * ROW_TILES, LANES), lambda i, pe: (i, 0))],
            out_specs=pl.BlockSpec(memory_space=pl.ANY),
            scratch_shapes=[pltpu.VMEM((block * ROW_TILES, LANES), F32), pltpu.SemaphoreType.DMA,
                            pltpu.SemaphoreType.DMA],
        ),
        out_shape=jax.ShapeDtypeStruct((n_slots * ROW_TILES, LANES), F32),
        compiler_params=_cparams("arbitrary"),
        name="moe_dispatch_rows",
    )(pend, dest, hm)


PAIR_GROUP = 2 * LANES


def _expert_body(be_ref, nb_ref, xs_ref, wu_ref, bu_ref, wd_ref, bd_ref, ys_ref, wu_bf, wd_bf):
    i = pl.program_id(0)
    f2 = wu_ref.shape[1]
    new_expert = (i == 0) | (be_ref[i] != be_ref[jnp.maximum(i - 1, 0)])

    @pl.when((i < nb_ref[0]) & new_expert)
    def _():
        r = lax.broadcasted_iota(jnp.int32, (PAIR_GROUP, PAIR_GROUP), 0)
        c = lax.broadcasted_iota(jnp.int32, (PAIR_GROUP, PAIR_GROUP), 1)
        src = jnp.where(c < LANES, 2 * c, 2 * (c - LANES) + 1)
        perm = (r == src).astype(BF16)
        for g in range(f2 // PAIR_GROUP):
            cols = slice(g * PAIR_GROUP, (g + 1) * PAIR_GROUP)
            wu_bf[:, cols] = jnp.dot(wu_ref[:, cols].astype(BF16), perm,
                                     preferred_element_type=F32).astype(BF16)
        wd_bf[...] = wd_ref[...].astype(BF16)

    block = xs_ref.shape[0] // ROW_TILES

    @pl.when(i < nb_ref[0])
    def _():
        x = jnp.concatenate([xs_ref[_chunk(c, block), :] for c in range(ROW_TILES)],
                            axis=1).astype(BF16)
        u = jnp.dot(x, wu_bf[...], preferred_element_type=F32) + bu_ref[...]
        acts = []
        for g in range(f2 // PAIR_GROUP):
            glu = jnp.minimum(u[:, g * PAIR_GROUP:g * PAIR_GROUP + LANES], SWIGLU_LIMIT)
            lin = jnp.clip(u[:, g * PAIR_GROUP + LANES:(g + 1) * PAIR_GROUP],
                           -SWIGLU_LIMIT, SWIGLU_LIMIT)
            acts.append((glu * jax.nn.sigmoid(SWIGLU_ALPHA * glu) * (lin + 1.0)).astype(BF16))
        act = jnp.concatenate(acts, axis=1)
        y = jnp.dot(act, wd_bf[...], preferred_element_type=F32) + bd_ref[...]
        for c in range(ROW_TILES):
            ys_ref[_chunk(c, block), :] = y[:, c * LANES:(c + 1) * LANES]

    @pl.when(i >= nb_ref[0])
    def _():
        ys_ref[...] = jnp.zeros_like(ys_ref)


def _experts(block_e, n_used, xs, wu, bu, wd, bd, *, block):
    n_slots = xs.shape[0] // ROW_TILES
    d, f2 = wu.shape[1:]
    n_blocks = n_slots // block
    rc_block = (block * ROW_TILES, LANES)
    xrow = pl.BlockSpec(rc_block, lambda i, be, nb: (jnp.maximum(jnp.minimum(i, nb[0] - 1), 0), 0))
    wspec = lambda a, b_: pl.BlockSpec((None, a, b_), lambda i, be, nb: (be[i], 0, 0))
    return pl.pallas_call(
        _expert_body,
        grid_spec=pltpu.PrefetchScalarGridSpec(
            num_scalar_prefetch=2,
            grid=(n_blocks,),
            in_specs=[xrow, wspec(d, f2), wspec(1, f2), wspec(f2 // 2, d), wspec(1, d)],
            out_specs=pl.BlockSpec(rc_block, lambda i, be, nb: (i, 0)),
            scratch_shapes=[pltpu.VMEM((d, f2), BF16), pltpu.VMEM((f2 // 2, d), BF16)],
        ),
        out_shape=jax.ShapeDtypeStruct(xs.shape, F32),
        compiler_params=_cparams("arbitrary"),
        name="moe_expert_mlp",
    )(block_e, n_used, xs, wu, bu, wd, bd)


def _final_body(dest_ref, h_ref, g2_ref, gm_ref, km_ref, nw_ref, ys_ref, o_ref, buf_ref, sem):
    tm = h_ref.shape[0]

    def issue(c, carry):
        for u in range(ROW_DMA_UNROLL):
            t = c * ROW_DMA_UNROLL + u
            for k in range(TOP_K):
                _row_copy(ys_ref, dest_ref[k * tm + t], buf_ref.at[k], t, sem).start(priority=k % 2)
        return carry

    lax.fori_loop(0, tm // ROW_DMA_UNROLL, issue, 0)

    def drain(c, carry):
        for u in range(ROW_DMA_UNROLL):
            for k in range(TOP_K):
                _row_copy(ys_ref, 0, buf_ref.at[k], c * ROW_DMA_UNROLL + u, sem).wait()
        return carry

    lax.fori_loop(0, tm // ROW_DMA_UNROLL, drain, 0)

    km = km_ref[...]
    gm = gm_ref[...]
    gates = [jnp.sum(jnp.where(km == float(k + 1), gm, 0.0), axis=-1, keepdims=True)
             for k in range(TOP_K)]
    ssq = jnp.zeros((tm, 1), F32)
    for c in range(ROW_TILES):
        cols = slice(c * LANES, (c + 1) * LANES)
        acc = gates[0] * buf_ref[0, _chunk(c, tm), :]
        for k in range(1, TOP_K):
            acc = acc + gates[k] * buf_ref[k, _chunk(c, tm), :]
        y = h_ref[:, cols] + g2_ref[:, cols] * acc
        ssq = ssq + jnp.sum(y * y, axis=-1, keepdims=True)
        o_ref[:, cols] = y
    inv = lax.rsqrt(ssq * (1.0 / (ROW_TILES * LANES)) + RMS_EPS)
    o_ref[...] = o_ref[...] * inv * nw_ref[...]


def _final(dest, h, g2, g2_spec, gm, km, norm_w, ys, *, tm, tile0, n_tiles):
    d = h.shape[1]
    ne = gm.shape[1]
    row = lambda width: pl.BlockSpec((tm, width), lambda i: (i + tile0, 0))
    return pl.pallas_call(
        _final_body,
        grid=(n_tiles,),
        in_specs=[pl.BlockSpec((TOP_K * tm,), lambda i: (i + tile0,), memory_space=pltpu.SMEM),
                  row(d), g2_spec, row(ne), row(ne),
                  pl.BlockSpec(norm_w.shape, lambda i: (0, 0)),
                  pl.BlockSpec(memory_space=pl.ANY)],
        out_specs=pl.BlockSpec((tm, d), lambda i: (i, 0)),
        out_shape=jax.ShapeDtypeStruct((n_tiles * tm, d), F32),
        scratch_shapes=[pltpu.VMEM((TOP_K, tm * ROW_TILES, LANES), F32), pltpu.SemaphoreType.DMA],
        compiler_params=_cparams("arbitrary"),
        name="moe_combine_final_norm",
    )(dest, h, g2, gm, km, norm_w, ys)


def kernel(x_prompt, x_sample, cache_k_win, cache_v_win, state_conv, c_prompt, c_sample, w_ada, b_ada, norm_mix, norm_ffn, w_in, b_in, conv_w, sinks, w_branch_a, w_branch_b, w_out, b_out, w_router, b_router, w_up, b_up, w_down, b_down, norm_final):
    batch, seq, d = x_prompt.shape
    nseq, tq, _ = x_sample.shape
    depth = w_ada.shape[0]
    assert depth == 1 and tq == SUBLANES and seq % TOKEN_TILE == 0 and d == ROW_TILES * LANES
    wb = cache_k_win.shape[2]
    tp, ts = batch * seq, nseq * tq
    t_all = tp + ts
    tm_s = min(ts, TOKEN_TILE)
    assert ts % tm_s == 0 and tp % tm_s == 0 and t_all % ROW_DMA_TILE == 0
    nmod = w_ada.shape[2] // d

    n_c = batch + nseq
    c_rows = -(-n_c // SUBLANES) * SUBLANES
    c_all = jnp.concatenate([c_prompt, c_sample, jnp.zeros((c_rows - n_c, d), F32)], axis=0)
    mod = _ada(c_all, w_ada[0], b_ada[0])
    mod_p = mod[:batch].reshape(batch * nmod, 1, d)
    mod_s = jnp.repeat(mod[batch:n_c], tq, axis=0)
    tiles_per_seq = seq // TOKEN_TILE
    p_spec = lambda chunk: pl.BlockSpec((None, 1, d), lambda i: ((i // tiles_per_seq) * nmod + chunk, 0, 0))
    s_spec = lambda chunk: pl.BlockSpec((tm_s, d), lambda i: (i, chunk))

    q_lo = 3 * d
    q_hi = q_lo + N_HEADS * HEAD_DIM

    def q_cols(a):
        lead = a.shape[0]
        qa = a[:, q_lo:q_hi].reshape(lead, N_KV_HEADS, GROUP, HEAD_DIM).transpose(0, 2, 1, 3)
        qa = qa.reshape(lead, q_hi - q_lo) * (HEAD_DIM ** -0.5 * LOG2_E)
        return jnp.concatenate([a[:, :q_lo], qa, a[:, q_hi:]], axis=1)

    w_in_b = q_cols(w_in[0]).astype(BF16)
    b_in2 = q_cols(b_in[0].reshape(1, -1))
    nmix = norm_mix[0].reshape(1, d)
    cw = conv_w[0]

    xp = x_prompt.reshape(tp, d)
    ya_p, q_p, k_p, v_p, ga_p, gb_p, ut_p = _inproj(
        xp, mod_p, mod_p, (p_spec(0), p_spec(1)), nmix, w_in_b, b_in2, cw,
        tm=TOKEN_TILE, tiles_per_seq=tiles_per_seq)
    xs_ = x_sample.reshape(ts, d)
    st = state_conv[0]
    zpad = lambda a: jnp.pad(a, ((0, 0), (0, tq - a.shape[1]), (0, 0))).reshape(ts, d)
    hist1 = zpad(st[:, 1:2])
    hist2 = zpad(st)
    ya_s, q_s, k_s, v_s, ga_s, gb_s, u_s = _inproj(
        xs_, mod_s, mod_s, (s_spec(0), s_spec(1)), nmix, w_in_b, b_in2, cw,
        tm=tm_s, tiles_per_seq=0, hist=(hist1, hist2))

    sk = sinks[0]
    yb_p = _attn_prompt(q_p, k_p, v_p, sk, batch=batch, seq=seq)
    kbuf = cache_k_win[0].reshape(nseq, wb, KV_DIM)
    vbuf = cache_v_win[0].reshape(nseq, wb, KV_DIM)
    yb_s, nk_s, nv_s = _attn_sample(
        q_s.reshape(nseq, tq, d), k_s.reshape(nseq, tq, KV_DIM), v_s.reshape(nseq, tq, KV_DIM),
        kbuf, vbuf, sk, bs=SUBLANES)
    yb_s = yb_s.reshape(ts, d)

    wa = w_branch_a[0].astype(BF16)
    wbb = w_branch_b[0].reshape(N_KV_HEADS, GROUP, HEAD_DIM, d).transpose(1, 0, 2, 3)
    wbb = wbb.reshape(N_HEADS * HEAD_DIM, d).astype(BF16)
    wo = w_out[0].astype(BF16)
    bo = b_out[0].reshape(1, d)
    nffn = norm_ffn[0].reshape(1, d)
    wr0 = w_router[0].astype(BF16)
    wr1 = (w_router[0] - wr0.astype(F32)).astype(BF16)
    wr2 = (w_router[0] - wr0.astype(F32) - wr1.astype(F32)).astype(BF16)
    wr = jnp.concatenate([wr0, wr1, wr2], axis=1)
    br = b_router[0].reshape(1, -1)
    n_tiles_p = tp // tm_s
    pp_spec = lambda chunk: pl.BlockSpec(
        (None, 1, d),
        lambda i: ((jnp.minimum(i, n_tiles_p - 1) // (seq // tm_s)) * nmod + chunk, 0, 0))
    sp_spec = lambda chunk: pl.BlockSpec((tm_s, d), lambda i: (jnp.maximum(i - n_tiles_p, 0), chunk))
    h_all, hm_all, gm_all, km_all = _post(
        (xp, ya_p, yb_p, ga_p, gb_p, mod_p, mod_p, mod_p), (pp_spec(2), pp_spec(3), pp_spec(4)),
        (xs_, ya_s, yb_s, ga_s, gb_s, mod_s, mod_s, mod_s), (sp_spec(2), sp_spec(3), sp_spec(4)),
        wa, wbb, wo, bo, nffn, wr, br, tm=tm_s)

    block = EXPERT_BLOCK
    dest, counts = _route(km_all, tm=tm_s, block=block)
    n_assign = t_all * TOP_K
    n_blocks = -(-(n_assign + N_EXPERTS * (block - 1)) // block)
    cnt = counts[0].astype(jnp.int32)
    pend = jnp.cumsum((cnt + block - 1) // block * block)
    n_used = (pend[-1] // block).astype(jnp.int32)
    blk_ids = jnp.minimum(jnp.arange(n_blocks, dtype=jnp.int32), n_used - 1)
    block_e = jnp.sum((pend[None, :] <= (blk_ids * block)[:, None]).astype(jnp.int32), axis=1)
    block_e = jnp.minimum(block_e, N_EXPERTS - 1)

    dest = dest.reshape(TOP_K, t_all // ROW_DMA_TILE, ROW_DMA_TILE).transpose(1, 0, 2).reshape(-1)
    xs_rows = _dispatch(pend.astype(jnp.int32), dest, hm_all, tm=ROW_DMA_TILE,
                        n_slots=n_blocks * block, block=block)
    f2 = w_up.shape[3]
    bu = b_up[0].reshape(N_EXPERTS, f2 // PAIR_GROUP, LANES, 2).transpose(0, 1, 3, 2)
    bu = bu.reshape(N_EXPERTS, 1, f2)
    bd = b_down[0].reshape(N_EXPERTS, 1, d)
    ys = _experts(block_e, n_used.reshape(1), xs_rows, w_up[0], bu, w_down[0], bd, block=block)

    nfin = norm_final.reshape(1, d)
    tiles_p = tp // ROW_DMA_TILE
    pf_spec = pl.BlockSpec((None, 1, d),
                           lambda i: ((i // (seq // ROW_DMA_TILE)) * nmod + 5, 0, 0))
    y_p = _final(dest, h_all, mod_p, pf_spec, gm_all, km_all, nfin, ys,
                 tm=ROW_DMA_TILE, tile0=0, n_tiles=tiles_p)
    sf_spec = pl.BlockSpec((ROW_DMA_TILE, d), lambda i: (i, 5))
    y_s = _final(dest, h_all, mod_s, sf_spec, gm_all, km_all, nfin, ys,
                 tm=ROW_DMA_TILE, tile0=tiles_p, n_tiles=ts // ROW_DMA_TILE)

    keep = min(WINDOW, seq)
    k4 = k_p.reshape(batch, seq, N_KV_HEADS, HEAD_DIM)[:, seq - keep:]
    v4 = v_p.reshape(batch, seq, N_KV_HEADS, HEAD_DIM)[:, seq - keep:]
    conv_p = ut_p.reshape(batch, tiles_per_seq, SUBLANES, d)[:, -1, SUBLANES - (CONV_WIDTH - 1):]
    conv_s = u_s.reshape(nseq, tq, d)[:, tq - (CONV_WIDTH - 1):]
    return (y_p.reshape(batch, seq, d), y_s.reshape(nseq, tq, d),
            k4[None], v4[None], conv_p[None],
            nk_s.reshape(1, nseq, wb, N_KV_HEADS, HEAD_DIM),
            nv_s.reshape(1, nseq, wb, N_KV_HEADS, HEAD_DIM), conv_s[None])
```

```python
import functools

import jax
import jax.numpy as jnp
from jax import lax
from jax.experimental import pallas as pl
from jax.experimental.pallas import tpu as pltpu

F32 = jnp.float32
BF16 = jnp.bfloat16
HIGHEST = lax.Precision.HIGHEST

N_HEADS = 16
N_KV_HEADS = 4
HEAD_DIM = 64
GROUP = N_HEADS // N_KV_HEADS
KV_DIM = N_KV_HEADS * HEAD_DIM
WINDOW = 128
N_EXPERTS = 32
TOP_K = 4
CONV_WIDTH = 3
SWIGLU_ALPHA = 1.702
SWIGLU_LIMIT = 7.0
RMS_EPS = 1e-5
ALIBI_SLOPES = tuple(2.0 ** (-8.0 * (h + 1) / N_HEADS) for h in range(N_HEADS))

SUBLANES = 8
LANES = 128
VMEM_LIMIT_BYTES = 56 * 1024 * 1024
TOKEN_TILE = 512
ROW_DMA_TILE = 256
ROW_DMA_UNROLL = 8
EXPERT_BLOCK = 512


def _cparams(*sem):
    return pltpu.CompilerParams(dimension_semantics=sem, vmem_limit_bytes=VMEM_LIMIT_BYTES)


def _rmsnorm(x, g):
    ms = jnp.mean(x * x, axis=-1, keepdims=True)
    return x * lax.rsqrt(ms + RMS_EPS) * g


def _ada_body(c_ref, w_ref, b_ref, o_ref):
    c = c_ref[...]
    a = c * jax.nn.sigmoid(c)
    o_ref[...] = jnp.dot(a, w_ref[...], preferred_element_type=F32, precision=HIGHEST) + b_ref[...]


def _ada(c_all, w_ada, b_ada):
    rows, d = c_all.shape
    n = w_ada.shape[1]
    return pl.pallas_call(
        _ada_body,
        grid=(n // d,),
        in_specs=[pl.BlockSpec((rows, d), lambda j: (0, 0)),
                  pl.BlockSpec((d, d), lambda j: (0, j)),
                  pl.BlockSpec((1, d), lambda j: (0, j))],
        out_specs=pl.BlockSpec((rows, d), lambda j: (0, j)),
        out_shape=jax.ShapeDtypeStruct((rows, n), F32),
        compiler_params=_cparams("arbitrary"),
        name="ada_modulation",
    )(c_all, w_ada, b_ada.reshape(1, n))


def _inproj_body(*refs, d, nc, tiles_per_seq):
    carry_mode = tiles_per_seq > 0
    if carry_mode:
        (x_ref, sh_ref, sc_ref, nw_ref, w_ref, b_ref, cw_ref,
         ya_ref, q_ref, k_ref, v_ref, ga_ref, gb_ref, ut_ref, carry_ref) = refs
    else:
        (x_ref, sh_ref, sc_ref, nw_ref, w_ref, b_ref, cw_ref, p1_ref, p2_ref,
         ya_ref, q_ref, k_ref, v_ref, ga_ref, gb_ref, ut_ref) = refs
    tm = x_ref.shape[0]
    xm = (_rmsnorm(x_ref[...], nw_ref[...]) * (1.0 + sc_ref[...]) + sh_ref[...]).astype(BF16)

    def proj(lo, width):
        return (jnp.dot(xm, w_ref[:, lo:lo + width], preferred_element_type=F32)
                + b_ref[:, lo:lo + width])

    if carry_mode:
        @pl.when(pl.program_id(0) % tiles_per_seq == 0)
        def _():
            carry_ref[...] = jnp.zeros_like(carry_ref)
    else:
        t_in_seq = lax.broadcasted_iota(jnp.int32, (tm, nc), 0) % SUBLANES

    for j in range(d // nc):
        c0 = j * nc
        u = proj(c0, nc) * proj(2 * d + c0, nc)
        if carry_mode:
            ue = jnp.concatenate([carry_ref[:, c0:c0 + nc], u], axis=0)
            s1 = pltpu.roll(ue, 1, axis=0)[SUBLANES:]
            s2 = pltpu.roll(ue, 2, axis=0)[SUBLANES:]
            carry_ref[:, c0:c0 + nc] = u[tm - SUBLANES:]
            ut_ref[:, c0:c0 + nc] = u[tm - SUBLANES:]
        else:
            s1 = jnp.where(t_in_seq >= 1, pltpu.roll(u, 1, axis=0), p1_ref[:, c0:c0 + nc])
            s2 = jnp.where(t_in_seq >= 2, pltpu.roll(u, 2, axis=0), p2_ref[:, c0:c0 + nc])
            ut_ref[:, c0:c0 + nc] = u
        cw = cw_ref[:, c0:c0 + nc]
        conv = s2 * cw[0:1] + s1 * cw[1:2] + u * cw[2:3]
        ya_ref[:, c0:c0 + nc] = (proj(d + c0, nc) * conv).astype(BF16)

    q_ref[...] = proj(3 * d, d).astype(BF16)
    k_ref[...] = proj(4 * d, KV_DIM)
    v_ref[...] = proj(4 * d + KV_DIM, KV_DIM)
    ga_ref[...] = jax.nn.sigmoid(proj(4 * d + 2 * KV_DIM, d)).astype(BF16)
    gb_ref[...] = jax.nn.sigmoid(proj(5 * d + 2 * KV_DIM, d)).astype(BF16)


def _inproj(x, shift, scale, shift_scale_specs, norm_w, w_in, b_in, conv_w, *, tm, tiles_per_seq,
            hist=None):
    t, d = x.shape
    n_in = w_in.shape[1]
    nc = 512
    carry_mode = tiles_per_seq > 0
    row = lambda width: pl.BlockSpec((tm, width), lambda i: (i, 0))
    full = lambda a: pl.BlockSpec(a.shape, lambda i: (0,) * a.ndim)
    in_specs = [row(d), shift_scale_specs[0], shift_scale_specs[1], full(norm_w), full(w_in),
                full(b_in), full(conv_w)]
    args = [x, shift, scale, norm_w, w_in, b_in, conv_w]
    if not carry_mode:
        in_specs += [row(d), row(d)]
        args += list(hist)
    ut_rows = SUBLANES if carry_mode else tm
    out_specs = [row(d), row(d), row(KV_DIM), row(KV_DIM), row(d), row(d),
                 pl.BlockSpec((ut_rows, d), lambda i: (i, 0))]
    out_shape = [jax.ShapeDtypeStruct((t, d), BF16), jax.ShapeDtypeStruct((t, d), BF16),
                 jax.ShapeDtypeStruct((t, KV_DIM), F32), jax.ShapeDtypeStruct((t, KV_DIM), F32),
                 jax.ShapeDtypeStruct((t, d), BF16), jax.ShapeDtypeStruct((t, d), BF16),
                 jax.ShapeDtypeStruct((t // tm * ut_rows, d), F32)]
    return pl.pallas_call(
        functools.partial(_inproj_body, d=d, nc=nc, tiles_per_seq=tiles_per_seq),
        grid=(t // tm,),
        in_specs=in_specs,
        out_specs=out_specs,
        out_shape=out_shape,
        scratch_shapes=[pltpu.VMEM((SUBLANES, d), F32)] if carry_mode else [],
        compiler_params=_cparams("arbitrary"),
        name="inproj_conv_long" if carry_mode else "inproj_conv_short",
    )(*args)


LOG2_E = 1.4426950408889634


def _softmax_rows(s):
    m = jnp.max(s, axis=-1, keepdims=True)
    p = jnp.exp2(s - m)
    den = jnp.sum(p, axis=-1, keepdims=True)
    return (p * (1.0 / den)).astype(BF16)


def _own_block(rows, rows_per_kv):
    r = lax.broadcasted_iota(jnp.int32, (rows, KV_DIM), 0)
    c = lax.broadcasted_iota(jnp.int32, (rows, KV_DIM), 1)
    return (c // HEAD_DIM) == (r // rows_per_kv) % N_KV_HEADS


def _attn_prompt_body(q_ref, kc_ref, kp_ref, vc_ref, vp_ref, bias_ref, o_ref):
    blk = q_ref.shape[0]
    kk = jnp.concatenate([kp_ref[...], kc_ref[...]], axis=0)
    vv = jnp.concatenate([vp_ref[...], vc_ref[...]], axis=0)
    col_kv = lax.broadcasted_iota(jnp.int32, kk.shape, 1) // HEAD_DIM
    not_sink = lax.broadcasted_iota(jnp.int32, kk.shape, 0) > 0
    kk_kv = [jnp.where((col_kv == kv) & not_sink, kk, 0.0).astype(BF16)
             for kv in range(N_KV_HEADS)]
    vv_kv = [jnp.where((col_kv == kv) & not_sink, vv, 0.0).astype(BF16)
             for kv in range(N_KV_HEADS)]

    def scores(g):
        a = q_ref[:, g * KV_DIM:(g + 1) * KV_DIM]
        return [lax.dot_general(a, kk_kv[kv], (((1,), (1,)), ((), ())),
                                preferred_element_type=F32)
                + bias_ref[g, kv * blk:(kv + 1) * blk, :] for kv in range(N_KV_HEADS)]

    def weighted_values(g, probs):
        out = jnp.dot(probs[0], vv_kv[0], preferred_element_type=F32)
        for kv in range(1, N_KV_HEADS):
            out = out + jnp.dot(probs[kv], vv_kv[kv], preferred_element_type=F32)
        o_ref[:, g * KV_DIM:(g + 1) * KV_DIM] = out.astype(BF16)

    s_next = scores(0)
    p_prev = None
    for g in range(GROUP):
        s_cur = s_next
        if g + 1 < GROUP:
            s_next = scores(g + 1)
        p_cur = [_softmax_rows(s) for s in s_cur]
        if p_prev is not None:
            weighted_values(g - 1, p_prev)
        p_prev = p_cur
    weighted_values(GROUP - 1, p_prev)


def _attn_table(sinks, tq_rows, n_keys, key_offset, first_block):
    assert key_offset >= WINDOW
    kv = jnp.arange(N_KV_HEADS)[None, :, None, None]
    g = jnp.arange(GROUP)[:, None, None, None]
    qi = jnp.arange(tq_rows)[None, None, :, None]
    kj = jnp.arange(n_keys)[None, None, None, :]
    slope = jnp.exp2(-8.0 * (kv * GROUP + g + 1).astype(F32) / N_HEADS)
    dist = qi + key_offset - kj
    valid = (dist >= 0) & (dist < WINDOW)
    if first_block:
        valid = valid & (kj >= key_offset)
    bias = jnp.where(valid, -slope * dist.astype(F32), -jnp.inf)
    sink = sinks.reshape(N_KV_HEADS, GROUP).T[:, :, None, None]
    bias = jnp.where(kj == 0, sink, bias) * LOG2_E
    return bias.reshape(GROUP, N_KV_HEADS * tq_rows, n_keys)


def _attn_prompt(q, k, v, sinks, *, batch, seq):
    t, d = q.shape
    blk = WINDOW
    nb = seq // blk
    rows = N_KV_HEADS * blk
    bias = jnp.stack([_attn_table(sinks, blk, 2 * blk, blk, True),
                      _attn_table(sinks, blk, 2 * blk, blk, False)])
    cur = lambda width: pl.BlockSpec((blk, width), lambda b, j: (b * nb + j, 0))
    prev = lambda width: pl.BlockSpec((blk, width), lambda b, j: (b * nb + jnp.maximum(j - 1, 0), 0))
    return pl.pallas_call(
        _attn_prompt_body,
        grid=(batch, nb),
        in_specs=[cur(d), cur(KV_DIM), prev(KV_DIM), cur(KV_DIM), prev(KV_DIM),
                  pl.BlockSpec((None, GROUP, rows, 2 * blk), lambda b, j: (jnp.minimum(j, 1), 0, 0, 0))],
        out_specs=cur(d),
        out_shape=jax.ShapeDtypeStruct((t, d), BF16),
        compiler_params=_cparams("arbitrary", "arbitrary"),
        name="attn_window_long",
    )(q, k, k, v, v, bias)


def _attn_sample_body(q_ref, kn_ref, vn_ref, kb_ref, vb_ref, bias_ref, o_ref, nk_ref, nv_ref):
    bs, tq, _ = q_ref.shape
    rows_g = N_KV_HEADS * tq
    rows = GROUP * rows_g
    own = _own_block(rows, tq)
    bias = bias_ref[...]
    not_sink = lax.broadcasted_iota(jnp.int32, (kb_ref.shape[1] + tq, KV_DIM), 0) > 0
    for b in range(bs):
        keys = jnp.concatenate([kb_ref[b], kn_ref[b]], axis=0)
        vals = jnp.concatenate([vb_ref[b], vn_ref[b]], axis=0)
        nk_ref[b] = keys[tq:]
        nv_ref[b] = vals[tq:]
        keys = jnp.where(not_sink, keys, 0.0).astype(BF16)
        vals = jnp.where(not_sink, vals, 0.0).astype(BF16)
        qb = q_ref[b].astype(F32)
        qbd = jnp.concatenate([qb[:, g * KV_DIM:(g + 1) * KV_DIM]
                               for g in range(GROUP) for _ in range(N_KV_HEADS)], axis=0)
        qbd = jnp.where(own, qbd, 0.0).astype(BF16)
        s = lax.dot_general(qbd, keys, (((1,), (1,)), ((), ())), preferred_element_type=F32)
        o = jnp.dot(_softmax_rows(s + bias), vals, preferred_element_type=F32)
        o = jnp.where(own, o, 0.0)
        outs = []
        for g in range(GROUP):
            out = o[g * rows_g:g * rows_g + tq]
            for kv in range(1, N_KV_HEADS):
                out = out + o[g * rows_g + kv * tq:g * rows_g + (kv + 1) * tq]
            outs.append(out)
        o_ref[b] = jnp.concatenate(outs, axis=1).astype(BF16)


def _attn_sample(q, k_new, v_new, k_buf, v_buf, sinks, *, bs):
    nseq, tq, d = q.shape
    wb = k_buf.shape[1]
    rows = N_HEADS * tq
    bias = _attn_table(sinks, tq, wb + tq, wb, False).reshape(rows, wb + tq)
    blk3 = lambda a, b_: pl.BlockSpec((bs, a, b_), lambda i: (i, 0, 0))
    return pl.pallas_call(
        _attn_sample_body,
        grid=(nseq // bs,),
        in_specs=[blk3(tq, d), blk3(tq, KV_DIM), blk3(tq, KV_DIM), blk3(wb, KV_DIM), blk3(wb, KV_DIM),
                  pl.BlockSpec((rows, wb + tq), lambda i: (0, 0))],
        out_specs=[blk3(tq, d), blk3(wb, KV_DIM), blk3(wb, KV_DIM)],
        out_shape=[jax.ShapeDtypeStruct((nseq, tq, d), BF16),
                   jax.ShapeDtypeStruct((nseq, wb, KV_DIM), F32),
                   jax.ShapeDtypeStruct((nseq, wb, KV_DIM), F32)],
        compiler_params=_cparams("arbitrary"),
        name="attn_window_short",
    )(q, k_new, v_new, k_buf, v_buf, bias)


N_POST_ACTS = 8


def _post_body(*refs, n_first):
    first, second = refs[:N_POST_ACTS], refs[N_POST_ACTS:2 * N_POST_ACTS]
    rest = refs[2 * N_POST_ACTS:]
    i = pl.program_id(0)

    @pl.when(i < n_first)
    def _():
        _post_tile(*first, *rest)

    @pl.when(i >= n_first)
    def _():
        _post_tile(*second, *rest)


def _post_tile(x_ref, ya_ref, yb_ref, ga_ref, gb_ref, g1_ref, sh2_ref, sc2_ref, wa_ref, wb_ref,
               wo_ref, bo_ref, nf_ref, wr_ref, br_ref, h_ref, hm_ref, gm_ref, km_ref):
    a = jnp.dot(ya_ref[...], wa_ref[...], preferred_element_type=F32)
    b = jnp.dot(yb_ref[...], wb_ref[...], preferred_element_type=F32)
    merged = ga_ref[...].astype(F32) * a + gb_ref[...].astype(F32) * b
    o = jnp.dot(merged.astype(BF16), wo_ref[...], preferred_element_type=F32) + bo_ref[...]
    h = x_ref[...] + g1_ref[...] * o
    h_ref[...] = h
    hm = _rmsnorm(h, nf_ref[...]) * (1.0 + sc2_ref[...]) + sh2_ref[...]
    for j, words in enumerate(_pack_rows(hm)):
        hm_ref[_chunk(j, hm.shape[0]), :] = words
    ne = br_ref.shape[1]
    hm_hi = hm.astype(BF16)
    hm_lo = (hm - hm_hi.astype(F32)).astype(BF16)
    wr = wr_ref[...]
    p_hi = jnp.dot(hm_hi, wr, preferred_element_type=F32)
    p_lo = jnp.dot(hm_lo, wr[:, :2 * ne], preferred_element_type=F32)
    logits = (p_hi[:, :ne] + (p_hi[:, ne:2 * ne] + p_lo[:, :ne])
              + (p_hi[:, 2 * ne:] + p_lo[:, ne:]) + br_ref[...])
    tm = logits.shape[0]
    lane = lax.broadcasted_iota(jnp.int32, (tm, ne), 1)
    vals = logits
    km = jnp.zeros((tm, ne), F32)
    top = []
    for k in range(TOP_K):
        m = jnp.max(vals, axis=-1, keepdims=True)
        first = jnp.min(jnp.where(vals == m, lane, ne), axis=-1, keepdims=True)
        sel = lane == first
        km = jnp.where(sel, float(k + 1), km)
        vals = jnp.where(sel, -jnp.inf, vals)
        top.append(m)
    e = [jnp.exp(t - top[0]) for t in top]
    den = e[0] + e[1] + e[2] + e[3]
    gm = jnp.zeros((tm, ne), F32)
    for k in range(TOP_K):
        gm = jnp.where(km == float(k + 1), e[k] / den, gm)
    gm_ref[...] = gm
    km_ref[...] = km


def _post(first, first_mod_specs, second, second_mod_specs, wa, wb, wo, bo, nf, wr, br, *, tm):
    t1, d = first[0].shape
    t2 = second[0].shape[0]
    n1, n2 = t1 // tm, t2 // tm
    ne = br.shape[1]
    row1 = pl.BlockSpec((tm, d), lambda i: (jnp.minimum(i, n1 - 1), 0))
    row2 = pl.BlockSpec((tm, d), lambda i: (jnp.maximum(i - n1, 0), 0))
    orow = lambda width: pl.BlockSpec((tm, width), lambda i: (i, 0))
    full = lambda a: pl.BlockSpec(a.shape, lambda i: (0,) * a.ndim)
    weights = [wa, wb, wo, bo, nf, wr, br]
    in_specs = ([row1] * 5 + list(first_mod_specs) + [row2] * 5 + list(second_mod_specs)
                + [full(w) for w in weights])
    return pl.pallas_call(
        functools.partial(_post_body, n_first=n1),
        grid=(n1 + n2,),
        in_specs=in_specs,
        out_specs=[orow(d), pl.BlockSpec((tm * ROW_TILES, LANES), lambda i: (i, 0)),
                   orow(ne), orow(ne)],
        out_shape=[jax.ShapeDtypeStruct((t1 + t2, d), F32),
                   jax.ShapeDtypeStruct(((t1 + t2) * ROW_TILES, LANES), PACKED),
                   jax.ShapeDtypeStruct((t1 + t2, ne), F32), jax.ShapeDtypeStruct((t1 + t2, ne), F32)],
        compiler_params=_cparams("arbitrary"),
        name="merge_outproj_router",
    )(*first, *second, *weights)


def _route_body(km_ref, dest_ref, cnt_ref, run_ref, start_ref, *, block):
    ph = pl.program_id(0)
    i = pl.program_id(1)
    km = km_ref[...]
    tm, ne = km.shape
    chosen = (km > 0.0).astype(F32)
    tile_cnt = jnp.sum(chosen, axis=0, keepdims=True)

    @pl.when((ph == 0) & (i == 0))
    def _():
        run_ref[...] = jnp.zeros_like(run_ref)

    @pl.when(ph == 0)
    def _():
        run_ref[...] += tile_cnt

    @pl.when((ph == 1) & (i == 0))
    def _():
        cnt = run_ref[...]
        cnt_ref[...] = cnt
        padded = jnp.ceil(cnt / block) * block
        r = lax.broadcasted_iota(jnp.int32, (ne, ne), 0)
        c = lax.broadcasted_iota(jnp.int32, (ne, ne), 1)
        before = (r < c).astype(F32)
        start_ref[...] = jnp.dot(padded, before, preferred_element_type=F32, precision=HIGHEST)
        run_ref[...] = jnp.zeros_like(run_ref)

    @pl.when(ph == 1)
    def _():
        r = lax.broadcasted_iota(jnp.int32, (tm, tm), 0)
        c = lax.broadcasted_iota(jnp.int32, (tm, tm), 1)
        earlier = (c < r).astype(BF16)
        rank = jnp.dot(earlier, chosen.astype(BF16), preferred_element_type=F32)
        slot = start_ref[...] + run_ref[...] + rank
        ones = jnp.ones((SUBLANES, ne), F32)
        for k in range(TOP_K):
            sk = jnp.where(km == float(k + 1), slot, 0.0)
            rowk = lax.dot_general(ones, sk, (((1,), (1,)), ((), ())),
                                   preferred_element_type=F32, precision=HIGHEST)
            dest_ref[k:k + 1, :] = rowk[0:1].astype(jnp.int32)
        run_ref[...] += tile_cnt


def _route(km, *, tm, block):
    t, ne = km.shape
    return pl.pallas_call(
        functools.partial(_route_body, block=block),
        grid=(2, t // tm),
        in_specs=[pl.BlockSpec((tm, ne), lambda p, i: (i, 0))],
        out_specs=[pl.BlockSpec((TOP_K, tm), lambda p, i: (0, i * p)),
                   pl.BlockSpec((1, ne), lambda p, i: (0, 0))],
        out_shape=[jax.ShapeDtypeStruct((TOP_K, t), jnp.int32),
                   jax.ShapeDtypeStruct((1, ne), F32)],
        scratch_shapes=[pltpu.VMEM((1, ne), F32), pltpu.VMEM((1, ne), F32)],
        compiler_params=_cparams("arbitrary", "arbitrary"),
        name="route_slots",
    )(km)


ROW_TILES = 4
ROW_WIDTH = ROW_TILES * 2 * LANES
PACKED = jnp.uint32
HIGH_HALF = 0xFFFF0000


def _row_copy(src, src_row, dst, dst_row, sem):
    src_at = pl.ds(pl.multiple_of(src_row * ROW_TILES, ROW_TILES), ROW_TILES)
    dst_at = pl.ds(pl.multiple_of(dst_row * ROW_TILES, ROW_TILES), ROW_TILES)
    return pltpu.make_async_copy(src.at[src_at, :], dst.at[dst_at, :], sem)


def _chunk(j, rows):
    return pl.ds(j, rows, stride=ROW_TILES)


def _pack_rows(x):
    def bf16_bits(v):
        return lax.bitcast_convert_type(v.astype(BF16).astype(F32), PACKED)

    words = []
    for j in range(ROW_TILES):
        lo = bf16_bits(x[:, 2 * j * LANES:(2 * j + 1) * LANES]) >> 16
        hi = bf16_bits(x[:, (2 * j + 1) * LANES:(2 * j + 2) * LANES]) & jnp.uint32(HIGH_HALF)
        words.append(lo | hi)
    return words


def _unpack_words(u):
    lo = lax.bitcast_convert_type(u << 16, F32)
    hi = lax.bitcast_convert_type(u & jnp.uint32(HIGH_HALF), F32)
    return lo, hi


def _dispatch_body(pend_ref, dest_ref, hm_ref, xs_ref, zero_ref, sem, zsem):
    tm = hm_ref.shape[0] // ROW_TILES
    block = zero_ref.shape[0] // ROW_TILES

    @pl.when(pl.program_id(0) == 0)
    def _():
        zero_ref[...] = jnp.zeros_like(zero_ref)
        n_exp = pend_ref.shape[0]
        n_slots = xs_ref.shape[0] // ROW_TILES
        fills = []
        for e in range(n_exp):
            end = pend_ref[e]
            begin = pend_ref[e - 1] if e > 0 else 0
            fills.append((end > begin, end - block))
        for j in range(n_exp):
            begin = pend_ref[n_exp - 1] + j * block
            fills.append((begin < n_slots, begin))
        for phase in ("start", "wait"):
            for cond, begin in fills:
                @pl.when(cond)
                def _():
                    rows = block * ROW_TILES
                    fill = pltpu.make_async_copy(
                        zero_ref,
                        xs_ref.at[pl.ds(pl.multiple_of(begin * ROW_TILES, rows), rows), :], zsem)
                    fill.start() if phase == "start" else fill.wait()

    def issue(c, carry):
        for u in range(ROW_DMA_UNROLL):
            t = c * ROW_DMA_UNROLL + u
            for k in range(TOP_K):
                _row_copy(hm_ref, t, xs_ref, dest_ref[k * tm + t], sem).start(priority=k % 2)
        return carry

    lax.fori_loop(0, tm // ROW_DMA_UNROLL, issue, 0)

    def drain(c, carry):
        for u in range(ROW_DMA_UNROLL):
            for k in range(TOP_K):
                _row_copy(hm_ref, c * ROW_DMA_UNROLL + u, xs_ref, 0, sem).wait()
        return carry

    lax.fori_loop(0, tm // ROW_DMA_UNROLL, drain, 0)


def _dispatch(pend, dest, hm, *, tm, n_slots, block):
    t = hm.shape[0] // ROW_TILES
    return pl.pallas_call(
        _dispatch_body,
        grid_spec=pltpu.PrefetchScalarGridSpec(
            num_scalar_prefetch=1,
            grid=(t // tm,),
            in_specs=[pl.BlockSpec((TOP_K * tm,), lambda i, pe: (i,), memory_space=pltpu.SMEM),
                      pl.BlockSpec((tm * ROW_TILES, LANES), lambda i, pe: (i, 0))],
            out_specs=pl.BlockSpec(memory_space=pl.ANY),
            scratch_shapes=[pltpu.VMEM((block * ROW_TILES, LANES), PACKED), pltpu.SemaphoreType.DMA,
                            pltpu.SemaphoreType.DMA],
        ),
        out_shape=jax.ShapeDtypeStruct((n_slots * ROW_TILES, LANES), PACKED),
        compiler_params=_cparams("arbitrary"),
        name="moe_dispatch_rows",
    )(pend, dest, hm)


PAIR_GROUP = 2 * LANES


def _expert_body(be_ref, nb_ref, xs_ref, wu_ref, bu_ref, wd_ref, bd_ref, ys_ref, wu_bf, wd_bf):
    i = pl.program_id(0)
    f2 = wu_ref.shape[1]
    new_expert = (i == 0) | (be_ref[i] != be_ref[jnp.maximum(i - 1, 0)])

    @pl.when((i < nb_ref[0]) & new_expert)
    def _():
        r = lax.broadcasted_iota(jnp.int32, (PAIR_GROUP, PAIR_GROUP), 0)
        c = lax.broadcasted_iota(jnp.int32, (PAIR_GROUP, PAIR_GROUP), 1)
        src = jnp.where(c < LANES, 2 * c, 2 * (c - LANES) + 1)
        perm = (r == src).astype(BF16)
        for g in range(f2 // PAIR_GROUP):
            cols = slice(g * PAIR_GROUP, (g + 1) * PAIR_GROUP)
            wu_bf[:, cols] = jnp.dot(wu_ref[:, cols].astype(BF16), perm,
                                     preferred_element_type=F32).astype(BF16)
        wd_bf[...] = wd_ref[...].astype(BF16)

    block = xs_ref.shape[0] // ROW_TILES

    @pl.when(i < nb_ref[0])
    def _():
        x = jnp.concatenate([half for j in range(ROW_TILES)
                             for half in _unpack_words(xs_ref[_chunk(j, block), :])],
                            axis=1).astype(BF16)
        u = jnp.dot(x, wu_bf[...], preferred_element_type=F32) + bu_ref[...]
        acts = []
        for g in range(f2 // PAIR_GROUP):
            glu = jnp.minimum(u[:, g * PAIR_GROUP:g * PAIR_GROUP + LANES], SWIGLU_LIMIT)
            lin = jnp.clip(u[:, g * PAIR_GROUP + LANES:(g + 1) * PAIR_GROUP],
                           -SWIGLU_LIMIT, SWIGLU_LIMIT)
            acts.append((glu * jax.nn.sigmoid(SWIGLU_ALPHA * glu) * (lin + 1.0)).astype(BF16))
        act = jnp.concatenate(acts, axis=1)
        y = jnp.dot(act, wd_bf[...], preferred_element_type=F32) + bd_ref[...]
        for j, words in enumerate(_pack_rows(y)):
            ys_ref[_chunk(j, block), :] = words

    @pl.when(i >= nb_ref[0])
    def _():
        ys_ref[...] = jnp.zeros_like(ys_ref)


def _experts(block_e, n_used, xs, wu, bu, wd, bd, *, block):
    n_slots = xs.shape[0] // ROW_TILES
    d, f2 = wu.shape[1:]
    n_blocks = n_slots // block
    rc_block = (block * ROW_TILES, LANES)
    xrow = pl.BlockSpec(rc_block, lambda i, be, nb: (jnp.maximum(jnp.minimum(i, nb[0] - 1), 0), 0))
    wspec = lambda a, b_: pl.BlockSpec((None, a, b_), lambda i, be, nb: (be[i], 0, 0))
    return pl.pallas_call(
        _expert_body,
        grid_spec=pltpu.PrefetchScalarGridSpec(
            num_scalar_prefetch=2,
            grid=(n_blocks,),
            in_specs=[xrow, wspec(d, f2), wspec(1, f2), wspec(f2 // 2, d), wspec(1, d)],
            out_specs=pl.BlockSpec(rc_block, lambda i, be, nb: (i, 0)),
            scratch_shapes=[pltpu.VMEM((d, f2), BF16), pltpu.VMEM((f2 // 2, d), BF16)],
        ),
        out_shape=jax.ShapeDtypeStruct(xs.shape, PACKED),
        compiler_params=_cparams("arbitrary"),
        name="moe_expert_mlp",
    )(block_e, n_used, xs, wu, bu, wd, bd)


def _final_body(dest_ref, next_dest_ref, h_ref, g2_ref, gm_ref, km_ref, nw_ref, ys_ref, o_ref,
                buf_ref, sems):
    i = pl.program_id(0)
    tm = h_ref.shape[0] // 2

    def gather(dref, half):
        base = half * TOP_K * tm

        def issue(c, carry):
            for u in range(ROW_DMA_UNROLL):
                t = c * ROW_DMA_UNROLL + u
                for k in range(TOP_K):
                    _row_copy(ys_ref, dref[base + k * tm + t], buf_ref.at[half, k], t,
                              sems.at[half]).start(priority=k % 2)
            return carry

        lax.fori_loop(0, tm // ROW_DMA_UNROLL, issue, 0)

    def drain(half):
        def wait(c, carry):
            for u in range(ROW_DMA_UNROLL):
                for k in range(TOP_K):
                    _row_copy(ys_ref, 0, buf_ref.at[half, k], c * ROW_DMA_UNROLL + u,
                              sems.at[half]).wait()
            return carry

        lax.fori_loop(0, tm // ROW_DMA_UNROLL, wait, 0)

    def combine(half):
        rows = slice(half * tm, (half + 1) * tm)
        km = km_ref[rows, :]
        gm = gm_ref[rows, :]
        g2 = g2_ref[...] if g2_ref.shape[0] == 1 else g2_ref[rows, :]
        gates = [jnp.sum(jnp.where(km == float(k + 1), gm, 0.0), axis=-1, keepdims=True)
                 for k in range(TOP_K)]
        ssq = jnp.zeros((tm, 1), F32)
        for j in range(ROW_TILES):
            acc = [None, None]
            for k in range(TOP_K):
                for part, v in enumerate(_unpack_words(buf_ref[half, k, _chunk(j, tm), :])):
                    acc[part] = gates[k] * v if acc[part] is None else acc[part] + gates[k] * v
            for part in range(2):
                cols = slice((2 * j + part) * LANES, (2 * j + part + 1) * LANES)
                y = h_ref[rows, cols] + g2[:, cols] * acc[part]
                ssq = ssq + jnp.sum(y * y, axis=-1, keepdims=True)
                o_ref[rows, cols] = y
        inv = lax.rsqrt(ssq * (1.0 / ROW_WIDTH) + RMS_EPS)
        o_ref[rows, :] = o_ref[rows, :] * inv * nw_ref[...]

    @pl.when(i == 0)
    def _():
        gather(dest_ref, 0)

    gather(dest_ref, 1)
    drain(0)
    combine(0)

    @pl.when(i + 1 < pl.num_programs(0))
    def _():
        gather(next_dest_ref, 0)

    drain(1)
    combine(1)


def _final(dest, h, g2, g2_spec, gm, km, norm_w, ys, *, tm, step0, n_steps):
    d = h.shape[1]
    ne = gm.shape[1]
    row = lambda width: pl.BlockSpec((2 * tm, width), lambda i: (i + step0, 0))
    per_step = 2 * TOP_K * tm
    last = step0 + n_steps - 1
    return pl.pallas_call(
        _final_body,
        grid=(n_steps,),
        in_specs=[pl.BlockSpec((per_step,), lambda i: (i + step0,), memory_space=pltpu.SMEM),
                  pl.BlockSpec((per_step,), lambda i: (jnp.minimum(i + step0 + 1, last),),
                               memory_space=pltpu.SMEM),
                  row(d), g2_spec, row(ne), row(ne),
                  pl.BlockSpec(norm_w.shape, lambda i: (0, 0)),
                  pl.BlockSpec(memory_space=pl.ANY)],
        out_specs=pl.BlockSpec((2 * tm, d), lambda i: (i, 0)),
        out_shape=jax.ShapeDtypeStruct((n_steps * 2 * tm, d), F32),
        scratch_shapes=[pltpu.VMEM((2, TOP_K, tm * ROW_TILES, LANES), PACKED),
                        pltpu.SemaphoreType.DMA((2,))],
        compiler_params=_cparams("arbitrary"),
        name="moe_combine_final_norm",
    )(dest, dest, h, g2, gm, km, norm_w, ys)


def kernel(x_prompt, x_sample, cache_k_win, cache_v_win, state_conv, c_prompt, c_sample, w_ada, b_ada, norm_mix, norm_ffn, w_in, b_in, conv_w, sinks, w_branch_a, w_branch_b, w_out, b_out, w_router, b_router, w_up, b_up, w_down, b_down, norm_final):
    batch, seq, d = x_prompt.shape
    nseq, tq, _ = x_sample.shape
    depth = w_ada.shape[0]
    assert depth == 1 and tq == SUBLANES and seq % TOKEN_TILE == 0 and d == ROW_WIDTH
    assert seq % (2 * ROW_DMA_TILE) == 0 and (nseq * tq) % (2 * ROW_DMA_TILE) == 0
    wb = cache_k_win.shape[2]
    tp, ts = batch * seq, nseq * tq
    t_all = tp + ts
    tm_s = min(ts, TOKEN_TILE)
    assert ts % tm_s == 0 and tp % tm_s == 0 and t_all % ROW_DMA_TILE == 0
    nmod = w_ada.shape[2] // d

    n_c = batch + nseq
    c_rows = -(-n_c // SUBLANES) * SUBLANES
    c_all = jnp.concatenate([c_prompt, c_sample, jnp.zeros((c_rows - n_c, d), F32)], axis=0)
    mod = _ada(c_all, w_ada[0], b_ada[0])
    mod_p = mod[:batch].reshape(batch * nmod, 1, d)
    mod_s = jnp.repeat(mod[batch:n_c], tq, axis=0)
    tiles_per_seq = seq // TOKEN_TILE
    p_spec = lambda chunk: pl.BlockSpec((None, 1, d), lambda i: ((i // tiles_per_seq) * nmod + chunk, 0, 0))
    s_spec = lambda chunk: pl.BlockSpec((tm_s, d), lambda i: (i, chunk))

    q_lo = 3 * d
    q_hi = q_lo + N_HEADS * HEAD_DIM

    def q_cols(a):
        lead = a.shape[0]
        qa = a[:, q_lo:q_hi].reshape(lead, N_KV_HEADS, GROUP, HEAD_DIM).transpose(0, 2, 1, 3)
        qa = qa.reshape(lead, q_hi - q_lo) * (HEAD_DIM ** -0.5 * LOG2_E)
        return jnp.concatenate([a[:, :q_lo], qa, a[:, q_hi:]], axis=1)

    w_in_b = q_cols(w_in[0]).astype(BF16)
    b_in2 = q_cols(b_in[0].reshape(1, -1))
    nmix = norm_mix[0].reshape(1, d)
    cw = conv_w[0]

    xp = x_prompt.reshape(tp, d)
    ya_p, q_p, k_p, v_p, ga_p, gb_p, ut_p = _inproj(
        xp, mod_p, mod_p, (p_spec(0), p_spec(1)), nmix, w_in_b, b_in2, cw,
        tm=TOKEN_TILE, tiles_per_seq=tiles_per_seq)
    xs_ = x_sample.reshape(ts, d)
    st = state_conv[0]
    zpad = lambda a: jnp.pad(a, ((0, 0), (0, tq - a.shape[1]), (0, 0))).reshape(ts, d)
    hist1 = zpad(st[:, 1:2])
    hist2 = zpad(st)
    ya_s, q_s, k_s, v_s, ga_s, gb_s, u_s = _inproj(
        xs_, mod_s, mod_s, (s_spec(0), s_spec(1)), nmix, w_in_b, b_in2, cw,
        tm=tm_s, tiles_per_seq=0, hist=(hist1, hist2))

    sk = sinks[0]
    yb_p = _attn_prompt(q_p, k_p, v_p, sk, batch=batch, seq=seq)
    kbuf = cache_k_win[0].reshape(nseq, wb, KV_DIM)
    vbuf = cache_v_win[0].reshape(nseq, wb, KV_DIM)
    yb_s, nk_s, nv_s = _attn_sample(
        q_s.reshape(nseq, tq, d), k_s.reshape(nseq, tq, KV_DIM), v_s.reshape(nseq, tq, KV_DIM),
        kbuf, vbuf, sk, bs=SUBLANES)
    yb_s = yb_s.reshape(ts, d)

    wa = w_branch_a[0].astype(BF16)
    wbb = w_branch_b[0].reshape(N_KV_HEADS, GROUP, HEAD_DIM, d).transpose(1, 0, 2, 3)
    wbb = wbb.reshape(N_HEADS * HEAD_DIM, d).astype(BF16)
    wo = w_out[0].astype(BF16)
    bo = b_out[0].reshape(1, d)
    nffn = norm_ffn[0].reshape(1, d)
    wr0 = w_router[0].astype(BF16)
    wr1 = (w_router[0] - wr0.astype(F32)).astype(BF16)
    wr2 = (w_router[0] - wr0.astype(F32) - wr1.astype(F32)).astype(BF16)
    wr = jnp.concatenate([wr0, wr1, wr2], axis=1)
    br = b_router[0].reshape(1, -1)
    n_tiles_p = tp // tm_s
    pp_spec = lambda chunk: pl.BlockSpec(
        (None, 1, d),
        lambda i: ((jnp.minimum(i, n_tiles_p - 1) // (seq // tm_s)) * nmod + chunk, 0, 0))
    sp_spec = lambda chunk: pl.BlockSpec((tm_s, d), lambda i: (jnp.maximum(i - n_tiles_p, 0), chunk))
    h_all, hm_all, gm_all, km_all = _post(
        (xp, ya_p, yb_p, ga_p, gb_p, mod_p, mod_p, mod_p), (pp_spec(2), pp_spec(3), pp_spec(4)),
        (xs_, ya_s, yb_s, ga_s, gb_s, mod_s, mod_s, mod_s), (sp_spec(2), sp_spec(3), sp_spec(4)),
        wa, wbb, wo, bo, nffn, wr, br, tm=tm_s)

    block = EXPERT_BLOCK
    dest, counts = _route(km_all, tm=tm_s, block=block)
    n_assign = t_all * TOP_K
    n_blocks = -(-(n_assign + N_EXPERTS * (block - 1)) // block)
    cnt = counts[0].astype(jnp.int32)
    pend = jnp.cumsum((cnt + block - 1) // block * block)
    n_used = (pend[-1] // block).astype(jnp.int32)
    blk_ids = jnp.minimum(jnp.arange(n_blocks, dtype=jnp.int32), n_used - 1)
    block_e = jnp.sum((pend[None, :] <= (blk_ids * block)[:, None]).astype(jnp.int32), axis=1)
    block_e = jnp.minimum(block_e, N_EXPERTS - 1)

    dest = dest.reshape(TOP_K, t_all // ROW_DMA_TILE, ROW_DMA_TILE).transpose(1, 0, 2).reshape(-1)
    xs_rows = _dispatch(pend.astype(jnp.int32), dest, hm_all, tm=ROW_DMA_TILE,
                        n_slots=n_blocks * block, block=block)
    f2 = w_up.shape[3]
    bu = b_up[0].reshape(N_EXPERTS, f2 // PAIR_GROUP, LANES, 2).transpose(0, 1, 3, 2)
    bu = bu.reshape(N_EXPERTS, 1, f2)
    bd = b_down[0].reshape(N_EXPERTS, 1, d)
    ys = _experts(block_e, n_used.reshape(1), xs_rows, w_up[0], bu, w_down[0], bd, block=block)

    nfin = norm_final.reshape(1, d)
    step_rows = 2 * ROW_DMA_TILE
    steps_p = tp // step_rows
    pf_spec = pl.BlockSpec((None, 1, d), lambda i: ((i // (seq // step_rows)) * nmod + 5, 0, 0))
    y_p = _final(dest, h_all, mod_p, pf_spec, gm_all, km_all, nfin, ys,
                 tm=ROW_DMA_TILE, step0=0, n_steps=steps_p)
    sf_spec = pl.BlockSpec((step_rows, d), lambda i: (i, 5))
    y_s = _final(dest, h_all, mod_s, sf_spec, gm_all, km_all, nfin, ys,
                 tm=ROW_DMA_TILE, step0=steps_p, n_steps=ts // step_rows)

    keep = min(WINDOW, seq)
    k4 = k_p.reshape(batch, seq, N_KV_HEADS, HEAD_DIM)[:, seq - keep:]
    v4 = v_p.reshape(batch, seq, N_KV_HEADS, HEAD_DIM)[:, seq - keep:]
    conv_p = ut_p.reshape(batch, tiles_per_seq, SUBLANES, d)[:, -1, SUBLANES - (CONV_WIDTH - 1):]
    conv_s = u_s.reshape(nseq, tq, d)[:, tq - (CONV_WIDTH - 1):]
    return (y_p.reshape(batch, seq, d), y_s.reshape(nseq, tq, d),
            k4[None], v4[None], conv_p[None],
            nk_s.reshape(1, nseq, wb, N_KV_HEADS, HEAD_DIM),
            nv_s.reshape(1, nseq, wb, N_KV_HEADS, HEAD_DIM), conv_s[None])
```

```python
import functools

import jax
import jax.numpy as jnp
from jax import lax
from jax.experimental import pallas as pl
from jax.experimental.pallas import tpu as pltpu

F32 = jnp.float32
BF16 = jnp.bfloat16
HIGHEST = lax.Precision.HIGHEST

N_HEADS = 16
N_KV_HEADS = 4
HEAD_DIM = 64
GROUP = N_HEADS // N_KV_HEADS
KV_DIM = N_KV_HEADS * HEAD_DIM
WINDOW = 128
N_EXPERTS = 32
TOP_K = 4
CONV_WIDTH = 3
SWIGLU_ALPHA = 1.702
SWIGLU_LIMIT = 7.0
RMS_EPS = 1e-5
ALIBI_SLOPES = tuple(2.0 ** (-8.0 * (h + 1) / N_HEADS) for h in range(N_HEADS))

SUBLANES = 8
LANES = 128
VMEM_LIMIT_BYTES = 56 * 1024 * 1024
TOKEN_TILE = 512
ROW_DMA_TILE = 256
ROW_DMA_UNROLL = 8
EXPERT_BLOCK = 512


def _cparams(*sem):
    return pltpu.CompilerParams(dimension_semantics=sem, vmem_limit_bytes=VMEM_LIMIT_BYTES)


def _rmsnorm(x, g):
    ms = jnp.mean(x * x, axis=-1, keepdims=True)
    return x * lax.rsqrt(ms + RMS_EPS) * g


def _ada_body(c_ref, w_ref, b_ref, o_ref):
    c = c_ref[...]
    a = c * jax.nn.sigmoid(c)
    o_ref[...] = jnp.dot(a, w_ref[...], preferred_element_type=F32, precision=HIGHEST) + b_ref[...]


def _ada(c_all, w_ada, b_ada):
    rows, d = c_all.shape
    n = w_ada.shape[1]
    return pl.pallas_call(
        _ada_body,
        grid=(n // d,),
        in_specs=[pl.BlockSpec((rows, d), lambda j: (0, 0)),
                  pl.BlockSpec((d, d), lambda j: (0, j)),
                  pl.BlockSpec((1, d), lambda j: (0, j))],
        out_specs=pl.BlockSpec((rows, d), lambda j: (0, j)),
        out_shape=jax.ShapeDtypeStruct((rows, n), F32),
        compiler_params=_cparams("arbitrary"),
        name="ada_modulation",
    )(c_all, w_ada, b_ada.reshape(1, n))


def _inproj_body(*refs, d, nc, tiles_per_seq):
    carry_mode = tiles_per_seq > 0
    if carry_mode:
        (x_ref, sh_ref, sc_ref, nw_ref, w_ref, b_ref, cw_ref,
         ya_ref, q_ref, k_ref, v_ref, ga_ref, gb_ref, ut_ref, carry_ref) = refs
    else:
        (x_ref, sh_ref, sc_ref, nw_ref, w_ref, b_ref, cw_ref, p1_ref, p2_ref,
         ya_ref, q_ref, k_ref, v_ref, ga_ref, gb_ref, ut_ref) = refs
    tm = x_ref.shape[0]
    xm = (_rmsnorm(x_ref[...], nw_ref[...]) * (1.0 + sc_ref[...]) + sh_ref[...]).astype(BF16)

    def proj(lo, width):
        return (jnp.dot(xm, w_ref[:, lo:lo + width], preferred_element_type=F32)
                + b_ref[:, lo:lo + width])

    if carry_mode:
        @pl.when(pl.program_id(0) % tiles_per_seq == 0)
        def _():
            carry_ref[...] = jnp.zeros_like(carry_ref)
    else:
        t_in_seq = lax.broadcasted_iota(jnp.int32, (tm, nc), 0) % SUBLANES

    for j in range(d // nc):
        c0 = j * nc
        u = proj(c0, nc) * proj(2 * d + c0, nc)
        if carry_mode:
            ue = jnp.concatenate([carry_ref[:, c0:c0 + nc], u], axis=0)
            s1 = pltpu.roll(ue, 1, axis=0)[SUBLANES:]
            s2 = pltpu.roll(ue, 2, axis=0)[SUBLANES:]
            carry_ref[:, c0:c0 + nc] = u[tm - SUBLANES:]
            ut_ref[:, c0:c0 + nc] = u[tm - SUBLANES:]
        else:
            s1 = jnp.where(t_in_seq >= 1, pltpu.roll(u, 1, axis=0), p1_ref[:, c0:c0 + nc])
            s2 = jnp.where(t_in_seq >= 2, pltpu.roll(u, 2, axis=0), p2_ref[:, c0:c0 + nc])
            ut_ref[:, c0:c0 + nc] = u
        cw = cw_ref[:, c0:c0 + nc]
        conv = s2 * cw[0:1] + s1 * cw[1:2] + u * cw[2:3]
        ya_ref[:, c0:c0 + nc] = (proj(d + c0, nc) * conv).astype(BF16)

    q_ref[...] = proj(3 * d, d).astype(BF16)
    k_ref[...] = proj(4 * d, KV_DIM)
    v_ref[...] = proj(4 * d + KV_DIM, KV_DIM)
    ga_ref[...] = jax.nn.sigmoid(proj(4 * d + 2 * KV_DIM, d)).astype(BF16)
    gb_ref[...] = jax.nn.sigmoid(proj(5 * d + 2 * KV_DIM, d)).astype(BF16)


def _inproj(x, shift, scale, shift_scale_specs, norm_w, w_in, b_in, conv_w, *, tm, tiles_per_seq,
            hist=None):
    t, d = x.shape
    n_in = w_in.shape[1]
    nc = 512
    carry_mode = tiles_per_seq > 0
    row = lambda width: pl.BlockSpec((tm, width), lambda i: (i, 0))
    full = lambda a: pl.BlockSpec(a.shape, lambda i: (0,) * a.ndim)
    in_specs = [row(d), shift_scale_specs[0], shift_scale_specs[1], full(norm_w), full(w_in),
                full(b_in), full(conv_w)]
    args = [x, shift, scale, norm_w, w_in, b_in, conv_w]
    if not carry_mode:
        in_specs += [row(d), row(d)]
        args += list(hist)
    ut_rows = SUBLANES if carry_mode else tm
    out_specs = [row(d), row(d), row(KV_DIM), row(KV_DIM), row(d), row(d),
                 pl.BlockSpec((ut_rows, d), lambda i: (i, 0))]
    out_shape = [jax.ShapeDtypeStruct((t, d), BF16), jax.ShapeDtypeStruct((t, d), BF16),
                 jax.ShapeDtypeStruct((t, KV_DIM), F32), jax.ShapeDtypeStruct((t, KV_DIM), F32),
                 jax.ShapeDtypeStruct((t, d), BF16), jax.ShapeDtypeStruct((t, d), BF16),
                 jax.ShapeDtypeStruct((t // tm * ut_rows, d), F32)]
    return pl.pallas_call(
        functools.partial(_inproj_body, d=d, nc=nc, tiles_per_seq=tiles_per_seq),
        grid=(t // tm,),
        in_specs=in_specs,
        out_specs=out_specs,
        out_shape=out_shape,
        scratch_shapes=[pltpu.VMEM((SUBLANES, d), F32)] if carry_mode else [],
        compiler_params=_cparams("arbitrary"),
        name="inproj_conv_long" if carry_mode else "inproj_conv_short",
    )(*args)


LOG2_E = 1.4426950408889634


def _softmax_rows(s):
    m = jnp.max(s, axis=-1, keepdims=True)
    p = jnp.exp2(s - m)
    den = jnp.sum(p, axis=-1, keepdims=True)
    return (p * (1.0 / den)).astype(BF16)


def _own_block(rows, rows_per_kv):
    r = lax.broadcasted_iota(jnp.int32, (rows, KV_DIM), 0)
    c = lax.broadcasted_iota(jnp.int32, (rows, KV_DIM), 1)
    return (c // HEAD_DIM) == (r // rows_per_kv) % N_KV_HEADS


def _attn_prompt_body(q_ref, kc_ref, kp_ref, vc_ref, vp_ref, bias_ref, o_ref):
    blk = q_ref.shape[0]
    kk = jnp.concatenate([kp_ref[...], kc_ref[...]], axis=0)
    vv = jnp.concatenate([vp_ref[...], vc_ref[...]], axis=0)
    col_kv = lax.broadcasted_iota(jnp.int32, kk.shape, 1) // HEAD_DIM
    not_sink = lax.broadcasted_iota(jnp.int32, kk.shape, 0) > 0
    kk_kv = [jnp.where((col_kv == kv) & not_sink, kk, 0.0).astype(BF16)
             for kv in range(N_KV_HEADS)]
    vv_kv = [jnp.where((col_kv == kv) & not_sink, vv, 0.0).astype(BF16)
             for kv in range(N_KV_HEADS)]

    def scores(g):
        a = q_ref[:, g * KV_DIM:(g + 1) * KV_DIM]
        return [lax.dot_general(a, kk_kv[kv], (((1,), (1,)), ((), ())),
                                preferred_element_type=F32)
                + bias_ref[g, kv * blk:(kv + 1) * blk, :] for kv in range(N_KV_HEADS)]

    def weighted_values(g, probs):
        out = jnp.dot(probs[0], vv_kv[0], preferred_element_type=F32)
        for kv in range(1, N_KV_HEADS):
            out = out + jnp.dot(probs[kv], vv_kv[kv], preferred_element_type=F32)
        o_ref[:, g * KV_DIM:(g + 1) * KV_DIM] = out.astype(BF16)

    s_next = scores(0)
    p_prev = None
    for g in range(GROUP):
        s_cur = s_next
        if g + 1 < GROUP:
            s_next = scores(g + 1)
        p_cur = [_softmax_rows(s) for s in s_cur]
        if p_prev is not None:
            weighted_values(g - 1, p_prev)
        p_prev = p_cur
    weighted_values(GROUP - 1, p_prev)


def _attn_table(sinks, tq_rows, n_keys, key_offset, first_block):
    assert key_offset >= WINDOW
    kv = jnp.arange(N_KV_HEADS)[None, :, None, None]
    g = jnp.arange(GROUP)[:, None, None, None]
    qi = jnp.arange(tq_rows)[None, None, :, None]
    kj = jnp.arange(n_keys)[None, None, None, :]
    slope = jnp.exp2(-8.0 * (kv * GROUP + g + 1).astype(F32) / N_HEADS)
    dist = qi + key_offset - kj
    valid = (dist >= 0) & (dist < WINDOW)
    if first_block:
        valid = valid & (kj >= key_offset)
    bias = jnp.where(valid, -slope * dist.astype(F32), -jnp.inf)
    sink = sinks.reshape(N_KV_HEADS, GROUP).T[:, :, None, None]
    bias = jnp.where(kj == 0, sink, bias) * LOG2_E
    return bias.reshape(GROUP, N_KV_HEADS * tq_rows, n_keys)


def _attn_prompt(q, k, v, sinks, *, batch, seq):
    t, d = q.shape
    blk = WINDOW
    nb = seq // blk
    rows = N_KV_HEADS * blk
    bias = jnp.stack([_attn_table(sinks, blk, 2 * blk, blk, True),
                      _attn_table(sinks, blk, 2 * blk, blk, False)])
    cur = lambda width: pl.BlockSpec((blk, width), lambda b, j: (b * nb + j, 0))
    prev = lambda width: pl.BlockSpec((blk, width), lambda b, j: (b * nb + jnp.maximum(j - 1, 0), 0))
    return pl.pallas_call(
        _attn_prompt_body,
        grid=(batch, nb),
        in_specs=[cur(d), cur(KV_DIM), prev(KV_DIM), cur(KV_DIM), prev(KV_DIM),
                  pl.BlockSpec((None, GROUP, rows, 2 * blk), lambda b, j: (jnp.minimum(j, 1), 0, 0, 0))],
        out_specs=cur(d),
        out_shape=jax.ShapeDtypeStruct((t, d), BF16),
        compiler_params=_cparams("arbitrary", "arbitrary"),
        name="attn_window_long",
    )(q, k, k, v, v, bias)


def _attn_sample_body(q_ref, kn_ref, vn_ref, kb_ref, vb_ref, bias_ref, o_ref, nk_ref, nv_ref):
    bs, tq, _ = q_ref.shape
    rows_g = N_KV_HEADS * tq
    rows = GROUP * rows_g
    own = _own_block(rows, tq)
    bias = bias_ref[...]
    not_sink = lax.broadcasted_iota(jnp.int32, (kb_ref.shape[1] + tq, KV_DIM), 0) > 0
    for b in range(bs):
        keys = jnp.concatenate([kb_ref[b], kn_ref[b]], axis=0)
        vals = jnp.concatenate([vb_ref[b], vn_ref[b]], axis=0)
        nk_ref[b] = keys[tq:]
        nv_ref[b] = vals[tq:]
        keys = jnp.where(not_sink, keys, 0.0).astype(BF16)
        vals = jnp.where(not_sink, vals, 0.0).astype(BF16)
        qb = q_ref[b].astype(F32)
        qbd = jnp.concatenate([qb[:, g * KV_DIM:(g + 1) * KV_DIM]
                               for g in range(GROUP) for _ in range(N_KV_HEADS)], axis=0)
        qbd = jnp.where(own, qbd, 0.0).astype(BF16)
        s = lax.dot_general(qbd, keys, (((1,), (1,)), ((), ())), preferred_element_type=F32)
        o = jnp.dot(_softmax_rows(s + bias), vals, preferred_element_type=F32)
        o = jnp.where(own, o, 0.0)
        outs = []
        for g in range(GROUP):
            out = o[g * rows_g:g * rows_g + tq]
            for kv in range(1, N_KV_HEADS):
                out = out + o[g * rows_g + kv * tq:g * rows_g + (kv + 1) * tq]
            outs.append(out)
        o_ref[b] = jnp.concatenate(outs, axis=1).astype(BF16)


def _attn_sample(q, k_new, v_new, k_buf, v_buf, sinks, *, bs):
    nseq, tq, d = q.shape
    wb = k_buf.shape[1]
    rows = N_HEADS * tq
    bias = _attn_table(sinks, tq, wb + tq, wb, False).reshape(rows, wb + tq)
    blk3 = lambda a, b_: pl.BlockSpec((bs, a, b_), lambda i: (i, 0, 0))
    return pl.pallas_call(
        _attn_sample_body,
        grid=(nseq // bs,),
        in_specs=[blk3(tq, d), blk3(tq, KV_DIM), blk3(tq, KV_DIM), blk3(wb, KV_DIM), blk3(wb, KV_DIM),
                  pl.BlockSpec((rows, wb + tq), lambda i: (0, 0))],
        out_specs=[blk3(tq, d), blk3(wb, KV_DIM), blk3(wb, KV_DIM)],
        out_shape=[jax.ShapeDtypeStruct((nseq, tq, d), BF16),
                   jax.ShapeDtypeStruct((nseq, wb, KV_DIM), F32),
                   jax.ShapeDtypeStruct((nseq, wb, KV_DIM), F32)],
        compiler_params=_cparams("arbitrary"),
        name="attn_window_short",
    )(q, k_new, v_new, k_buf, v_buf, bias)


N_POST_ACTS = 8


def _post_body(*refs, n_first):
    first, second = refs[:N_POST_ACTS], refs[N_POST_ACTS:2 * N_POST_ACTS]
    rest = refs[2 * N_POST_ACTS:]
    i = pl.program_id(0)

    @pl.when(i < n_first)
    def _():
        _post_tile(*first, *rest)

    @pl.when(i >= n_first)
    def _():
        _post_tile(*second, *rest)


def _post_tile(x_ref, ya_ref, yb_ref, ga_ref, gb_ref, g1_ref, sh2_ref, sc2_ref, wa_ref, wb_ref,
               wo_ref, bo_ref, nf_ref, wr_ref, br_ref, h_ref, hm_ref, gm_ref, km_ref):
    a = jnp.dot(ya_ref[...], wa_ref[...], preferred_element_type=F32)
    b = jnp.dot(yb_ref[...], wb_ref[...], preferred_element_type=F32)
    merged = ga_ref[...].astype(F32) * a + gb_ref[...].astype(F32) * b
    o = jnp.dot(merged.astype(BF16), wo_ref[...], preferred_element_type=F32) + bo_ref[...]
    h = x_ref[...] + g1_ref[...] * o
    h_ref[...] = h
    hm = _rmsnorm(h, nf_ref[...]) * (1.0 + sc2_ref[...]) + sh2_ref[...]
    for j, words in enumerate(_pack_rows(hm)):
        hm_ref[_chunk(j, hm.shape[0]), :] = words
    ne = br_ref.shape[1]
    hm_hi = hm.astype(BF16)
    hm_lo = (hm - hm_hi.astype(F32)).astype(BF16)
    wr = wr_ref[...]
    p_hi = jnp.dot(hm_hi, wr, preferred_element_type=F32)
    p_lo = jnp.dot(hm_lo, wr[:, :2 * ne], preferred_element_type=F32)
    logits = (p_hi[:, :ne] + (p_hi[:, ne:2 * ne] + p_lo[:, :ne])
              + (p_hi[:, 2 * ne:] + p_lo[:, ne:]) + br_ref[...])
    tm = logits.shape[0]
    lane = lax.broadcasted_iota(jnp.int32, (tm, ne), 1)
    vals = logits
    km = jnp.zeros((tm, ne), F32)
    top = []
    for k in range(TOP_K):
        m = jnp.max(vals, axis=-1, keepdims=True)
        first = jnp.min(jnp.where(vals == m, lane, ne), axis=-1, keepdims=True)
        sel = lane == first
        km = jnp.where(sel, float(k + 1), km)
        vals = jnp.where(sel, -jnp.inf, vals)
        top.append(m)
    e = [jnp.exp(t - top[0]) for t in top]
    den = e[0] + e[1] + e[2] + e[3]
    gm = jnp.zeros((tm, ne), F32)
    for k in range(TOP_K):
        gm = jnp.where(km == float(k + 1), e[k] / den, gm)
    gm_ref[...] = gm
    km_ref[...] = km


def _post(first, first_mod_specs, second, second_mod_specs, wa, wb, wo, bo, nf, wr, br, *, tm):
    t1, d = first[0].shape
    t2 = second[0].shape[0]
    n1, n2 = t1 // tm, t2 // tm
    ne = br.shape[1]
    row1 = pl.BlockSpec((tm, d), lambda i: (jnp.minimum(i, n1 - 1), 0))
    row2 = pl.BlockSpec((tm, d), lambda i: (jnp.maximum(i - n1, 0), 0))
    orow = lambda width: pl.BlockSpec((tm, width), lambda i: (i, 0))
    full = lambda a: pl.BlockSpec(a.shape, lambda i: (0,) * a.ndim)
    weights = [wa, wb, wo, bo, nf, wr, br]
    in_specs = ([row1] * 5 + list(first_mod_specs) + [row2] * 5 + list(second_mod_specs)
                + [full(w) for w in weights])
    return pl.pallas_call(
        functools.partial(_post_body, n_first=n1),
        grid=(n1 + n2,),
        in_specs=in_specs,
        out_specs=[orow(d), pl.BlockSpec((tm * ROW_TILES, LANES), lambda i: (i, 0)),
                   orow(ne), orow(ne)],
        out_shape=[jax.ShapeDtypeStruct((t1 + t2, d), F32),
                   jax.ShapeDtypeStruct(((t1 + t2) * ROW_TILES, LANES), PACKED),
                   jax.ShapeDtypeStruct((t1 + t2, ne), F32), jax.ShapeDtypeStruct((t1 + t2, ne), F32)],
        compiler_params=_cparams("arbitrary"),
        name="merge_outproj_router",
    )(*first, *second, *weights)


def _route_body(km_ref, dest_ref, cnt_ref, run_ref, start_ref, *, block):
    ph = pl.program_id(0)
    i = pl.program_id(1)
    km = km_ref[...]
    tm, ne = km.shape
    chosen = (km > 0.0).astype(F32)
    tile_cnt = jnp.sum(chosen, axis=0, keepdims=True)

    @pl.when((ph == 0) & (i == 0))
    def _():
        run_ref[...] = jnp.zeros_like(run_ref)

    @pl.when(ph == 0)
    def _():
        run_ref[...] += tile_cnt

    @pl.when((ph == 1) & (i == 0))
    def _():
        cnt = run_ref[...]
        cnt_ref[...] = cnt
        padded = jnp.ceil(cnt / block) * block
        r = lax.broadcasted_iota(jnp.int32, (ne, ne), 0)
        c = lax.broadcasted_iota(jnp.int32, (ne, ne), 1)
        before = (r < c).astype(F32)
        start_ref[...] = jnp.dot(padded, before, preferred_element_type=F32, precision=HIGHEST)
        run_ref[...] = jnp.zeros_like(run_ref)

    @pl.when(ph == 1)
    def _():
        r = lax.broadcasted_iota(jnp.int32, (tm, tm), 0)
        c = lax.broadcasted_iota(jnp.int32, (tm, tm), 1)
        earlier = (c < r).astype(BF16)
        rank = jnp.dot(earlier, chosen.astype(BF16), preferred_element_type=F32)
        slot = start_ref[...] + run_ref[...] + rank
        ones = jnp.ones((SUBLANES, ne), F32)
        for k in range(TOP_K):
            sk = jnp.where(km == float(k + 1), slot, 0.0)
            rowk = lax.dot_general(ones, sk, (((1,), (1,)), ((), ())),
                                   preferred_element_type=F32, precision=HIGHEST)
            dest_ref[k:k + 1, :] = rowk[0:1].astype(jnp.int32)
        run_ref[...] += tile_cnt


def _route(km, *, tm, block):
    t, ne = km.shape
    return pl.pallas_call(
        functools.partial(_route_body, block=block),
        grid=(2, t // tm),
        in_specs=[pl.BlockSpec((tm, ne), lambda p, i: (i, 0))],
        out_specs=[pl.BlockSpec((TOP_K, tm), lambda p, i: (0, i * p)),
                   pl.BlockSpec((1, ne), lambda p, i: (0, 0))],
        out_shape=[jax.ShapeDtypeStruct((TOP_K, t), jnp.int32),
                   jax.ShapeDtypeStruct((1, ne), F32)],
        scratch_shapes=[pltpu.VMEM((1, ne), F32), pltpu.VMEM((1, ne), F32)],
        compiler_params=_cparams("arbitrary", "arbitrary"),
        name="route_slots",
    )(km)


ROW_TILES = 4
ROW_WIDTH = ROW_TILES * 2 * LANES
PACKED = jnp.uint32
HIGH_HALF = 0xFFFF0000


def _row_copy(src, src_row, dst, dst_row, sem):
    src_at = pl.ds(pl.multiple_of(src_row * ROW_TILES, ROW_TILES), ROW_TILES)
    dst_at = pl.ds(pl.multiple_of(dst_row * ROW_TILES, ROW_TILES), ROW_TILES)
    return pltpu.make_async_copy(src.at[src_at, :], dst.at[dst_at, :], sem)


def _chunk(j, rows, row0=0):
    return pl.ds(row0 * ROW_TILES + j, rows, stride=ROW_TILES)


def _pack_rows(x):
    def bf16_bits(v):
        return lax.bitcast_convert_type(v.astype(BF16).astype(F32), PACKED)

    words = []
    for j in range(ROW_TILES):
        lo = bf16_bits(x[:, 2 * j * LANES:(2 * j + 1) * LANES]) >> 16
        hi = bf16_bits(x[:, (2 * j + 1) * LANES:(2 * j + 2) * LANES]) & jnp.uint32(HIGH_HALF)
        words.append(lo | hi)
    return words


def _unpack_words(u):
    lo = lax.bitcast_convert_type(u << 16, F32)
    hi = lax.bitcast_convert_type(u & jnp.uint32(HIGH_HALF), F32)
    return lo, hi


def _dispatch_body(pend_ref, dest_ref, hm_ref, xs_ref, zero_ref, sem, zsem):
    tm = hm_ref.shape[0] // ROW_TILES
    block = zero_ref.shape[0] // ROW_TILES

    @pl.when(pl.program_id(0) == 0)
    def _():
        zero_ref[...] = jnp.zeros_like(zero_ref)
        n_exp = pend_ref.shape[0]
        n_slots = xs_ref.shape[0] // ROW_TILES
        fills = []
        for e in range(n_exp):
            end = pend_ref[e]
            begin = pend_ref[e - 1] if e > 0 else 0
            fills.append((end > begin, end - block))
        for j in range(n_exp):
            begin = pend_ref[n_exp - 1] + j * block
            fills.append((begin < n_slots, begin))
        for phase in ("start", "wait"):
            for cond, begin in fills:
                @pl.when(cond)
                def _():
                    rows = block * ROW_TILES
                    fill = pltpu.make_async_copy(
                        zero_ref,
                        xs_ref.at[pl.ds(pl.multiple_of(begin * ROW_TILES, rows), rows), :], zsem)
                    fill.start() if phase == "start" else fill.wait()

    def issue(c, carry):
        for u in range(ROW_DMA_UNROLL):
            t = c * ROW_DMA_UNROLL + u
            for k in range(TOP_K):
                _row_copy(hm_ref, t, xs_ref, dest_ref[k * tm + t], sem).start(priority=k % 2)
        return carry

    lax.fori_loop(0, tm // ROW_DMA_UNROLL, issue, 0)

    def drain(c, carry):
        for u in range(ROW_DMA_UNROLL):
            for k in range(TOP_K):
                _row_copy(hm_ref, c * ROW_DMA_UNROLL + u, xs_ref, 0, sem).wait()
        return carry

    lax.fori_loop(0, tm // ROW_DMA_UNROLL, drain, 0)


def _dispatch(pend, dest, hm, *, tm, n_slots, block):
    t = hm.shape[0] // ROW_TILES
    return pl.pallas_call(
        _dispatch_body,
        grid_spec=pltpu.PrefetchScalarGridSpec(
            num_scalar_prefetch=1,
            grid=(t // tm,),
            in_specs=[pl.BlockSpec((TOP_K * tm,), lambda i, pe: (i,), memory_space=pltpu.SMEM),
                      pl.BlockSpec((tm * ROW_TILES, LANES), lambda i, pe: (i, 0))],
            out_specs=pl.BlockSpec(memory_space=pl.ANY),
            scratch_shapes=[pltpu.VMEM((block * ROW_TILES, LANES), PACKED), pltpu.SemaphoreType.DMA,
                            pltpu.SemaphoreType.DMA],
        ),
        out_shape=jax.ShapeDtypeStruct((n_slots * ROW_TILES, LANES), PACKED),
        compiler_params=_cparams("arbitrary"),
        name="moe_dispatch_rows",
    )(pend, dest, hm)


PAIR_GROUP = 2 * LANES
HALF_BLOCKS = 2


def _expert_body(be_ref, nb_ref, nvalid_ref, next_ref, slot_ref, xs_ref, bu_ref, bd_ref, wu_hbm,
                 wd_hbm, ys_ref, wu_f32, wd_f32, wu_bf, wd_bf, sems):
    i = pl.program_id(0)
    f2 = wu_f32.shape[2]
    used = i < nb_ref[0]
    e = be_ref[i]
    slot = slot_ref[i]
    new_expert = (i == 0) | (e != be_ref[jnp.maximum(i - 1, 0)])

    def fetch(expert, s):
        return (pltpu.make_async_copy(wu_hbm.at[expert], wu_f32.at[s], sems.at[s, 0]),
                pltpu.make_async_copy(wd_hbm.at[expert], wd_f32.at[s], sems.at[s, 1]))

    @pl.when(used & (i == 0))
    def _():
        for cp in fetch(e, slot):
            cp.start()

    @pl.when(used & new_expert)
    def _():
        for cp in fetch(e, slot):
            cp.wait()

        @pl.when(next_ref[i] >= 0)
        def _():
            for cp in fetch(next_ref[i], 1 - slot):
                cp.start()

        r = lax.broadcasted_iota(jnp.int32, (PAIR_GROUP, PAIR_GROUP), 0)
        c = lax.broadcasted_iota(jnp.int32, (PAIR_GROUP, PAIR_GROUP), 1)
        src = jnp.where(c < LANES, 2 * c, 2 * (c - LANES) + 1)
        perm = (r == src).astype(BF16)
        for g in range(f2 // PAIR_GROUP):
            cols = slice(g * PAIR_GROUP, (g + 1) * PAIR_GROUP)
            wu_bf[:, cols] = jnp.dot(wu_f32[slot, :, cols].astype(BF16), perm,
                                     preferred_element_type=F32).astype(BF16)
        wd_bf[...] = wd_f32[slot].astype(BF16)

    rows = xs_ref.shape[0] // ROW_TILES // HALF_BLOCKS

    def half(row0):
        x = jnp.concatenate([part for j in range(ROW_TILES)
                             for part in _unpack_words(xs_ref[_chunk(j, rows, row0), :])],
                            axis=1).astype(BF16)
        u = jnp.dot(x, wu_bf[...], preferred_element_type=F32) + bu_ref[...]
        acts = []
        for g in range(f2 // PAIR_GROUP):
            glu = jnp.minimum(u[:, g * PAIR_GROUP:g * PAIR_GROUP + LANES], SWIGLU_LIMIT)
            lin = jnp.clip(u[:, g * PAIR_GROUP + LANES:(g + 1) * PAIR_GROUP],
                           -SWIGLU_LIMIT, SWIGLU_LIMIT)
            acts.append((glu * jax.nn.sigmoid(SWIGLU_ALPHA * glu) * (lin + 1.0)).astype(BF16))
        act = jnp.concatenate(acts, axis=1)
        y = jnp.dot(act, wd_bf[...], preferred_element_type=F32) + bd_ref[...]
        for j, words in enumerate(_pack_rows(y)):
            ys_ref[_chunk(j, rows, row0), :] = words

    for hb in range(HALF_BLOCKS):
        live = used & (nvalid_ref[i] > hb * rows)

        @pl.when(live)
        def _():
            half(hb * rows)

        @pl.when(jnp.logical_not(live))
        def _():
            for j in range(ROW_TILES):
                ys_ref[_chunk(j, rows, hb * rows), :] = jnp.zeros((rows, LANES), PACKED)


def _experts(block_e, n_used, n_valid, next_e, slot, xs, wu, bu, wd, bd, *, block):
    n_slots = xs.shape[0] // ROW_TILES
    d, f2 = wu.shape[1:]
    n_blocks = n_slots // block
    rc_block = (block * ROW_TILES, LANES)
    xrow = pl.BlockSpec(rc_block,
                        lambda i, be, nb, *_: (jnp.maximum(jnp.minimum(i, nb[0] - 1), 0), 0))
    bias = lambda width: pl.BlockSpec((None, 1, width), lambda i, be, *_: (be[i], 0, 0))
    return pl.pallas_call(
        _expert_body,
        grid_spec=pltpu.PrefetchScalarGridSpec(
            num_scalar_prefetch=5,
            grid=(n_blocks,),
            in_specs=[xrow, bias(f2), bias(d), pl.BlockSpec(memory_space=pl.ANY),
                      pl.BlockSpec(memory_space=pl.ANY)],
            out_specs=pl.BlockSpec(rc_block, lambda i, *_: (i, 0)),
            scratch_shapes=[pltpu.VMEM((2, d, f2), F32), pltpu.VMEM((2, f2 // 2, d), F32),
                            pltpu.VMEM((d, f2), BF16), pltpu.VMEM((f2 // 2, d), BF16),
                            pltpu.SemaphoreType.DMA((2, 2))],
        ),
        out_shape=jax.ShapeDtypeStruct(xs.shape, PACKED),
        compiler_params=_cparams("arbitrary"),
        name="moe_expert_mlp",
    )(block_e, n_used, n_valid, next_e, slot, xs, bu, bd, wu, wd)


def _final_body(dest_ref, next_dest_ref, h_ref, g2_ref, gm_ref, km_ref, nw_ref, ys_ref, o_ref,
                buf_ref, sems):
    i = pl.program_id(0)
    tm = h_ref.shape[0] // 2

    def gather(dref, half):
        base = half * TOP_K * tm

        def issue(c, carry):
            for u in range(ROW_DMA_UNROLL):
                t = c * ROW_DMA_UNROLL + u
                for k in range(TOP_K):
                    _row_copy(ys_ref, dref[base + k * tm + t], buf_ref.at[half, k], t,
                              sems.at[half]).start(priority=k % 2)
            return carry

        lax.fori_loop(0, tm // ROW_DMA_UNROLL, issue, 0)

    def drain(half):
        def wait(c, carry):
            for u in range(ROW_DMA_UNROLL):
                for k in range(TOP_K):
                    _row_copy(ys_ref, 0, buf_ref.at[half, k], c * ROW_DMA_UNROLL + u,
                              sems.at[half]).wait()
            return carry

        lax.fori_loop(0, tm // ROW_DMA_UNROLL, wait, 0)

    def combine(half):
        rows = slice(half * tm, (half + 1) * tm)
        km = km_ref[rows, :]
        gm = gm_ref[rows, :]
        g2 = g2_ref[...] if g2_ref.shape[0] == 1 else g2_ref[rows, :]
        gates = [jnp.sum(jnp.where(km == float(k + 1), gm, 0.0), axis=-1, keepdims=True)
                 for k in range(TOP_K)]
        ssq = jnp.zeros((tm, 1), F32)
        for j in range(ROW_TILES):
            acc = [None, None]
            for k in range(TOP_K):
                for part, v in enumerate(_unpack_words(buf_ref[half, k, _chunk(j, tm), :])):
                    acc[part] = gates[k] * v if acc[part] is None else acc[part] + gates[k] * v
            for part in range(2):
                cols = slice((2 * j + part) * LANES, (2 * j + part + 1) * LANES)
                y = h_ref[rows, cols] + g2[:, cols] * acc[part]
                ssq = ssq + jnp.sum(y * y, axis=-1, keepdims=True)
                o_ref[rows, cols] = y
        inv = lax.rsqrt(ssq * (1.0 / ROW_WIDTH) + RMS_EPS)
        o_ref[rows, :] = o_ref[rows, :] * inv * nw_ref[...]

    @pl.when(i == 0)
    def _():
        gather(dest_ref, 0)

    gather(dest_ref, 1)
    drain(0)
    combine(0)

    @pl.when(i + 1 < pl.num_programs(0))
    def _():
        gather(next_dest_ref, 0)

    drain(1)
    combine(1)


def _final(dest, h, g2, g2_spec, gm, km, norm_w, ys, *, tm, step0, n_steps):
    d = h.shape[1]
    ne = gm.shape[1]
    row = lambda width: pl.BlockSpec((2 * tm, width), lambda i: (i + step0, 0))
    per_step = 2 * TOP_K * tm
    last = step0 + n_steps - 1
    return pl.pallas_call(
        _final_body,
        grid=(n_steps,),
        in_specs=[pl.BlockSpec((per_step,), lambda i: (i + step0,), memory_space=pltpu.SMEM),
                  pl.BlockSpec((per_step,), lambda i: (jnp.minimum(i + step0 + 1, last),),
                               memory_space=pltpu.SMEM),
                  row(d), g2_spec, row(ne), row(ne),
                  pl.BlockSpec(norm_w.shape, lambda i: (0, 0)),
                  pl.BlockSpec(memory_space=pl.ANY)],
        out_specs=pl.BlockSpec((2 * tm, d), lambda i: (i, 0)),
        out_shape=jax.ShapeDtypeStruct((n_steps * 2 * tm, d), F32),
        scratch_shapes=[pltpu.VMEM((2, TOP_K, tm * ROW_TILES, LANES), PACKED),
                        pltpu.SemaphoreType.DMA((2,))],
        compiler_params=_cparams("arbitrary"),
        name="moe_combine_final_norm",
    )(dest, dest, h, g2, gm, km, norm_w, ys)


def kernel(x_prompt, x_sample, cache_k_win, cache_v_win, state_conv, c_prompt, c_sample, w_ada, b_ada, norm_mix, norm_ffn, w_in, b_in, conv_w, sinks, w_branch_a, w_branch_b, w_out, b_out, w_router, b_router, w_up, b_up, w_down, b_down, norm_final):
    batch, seq, d = x_prompt.shape
    nseq, tq, _ = x_sample.shape
    depth = w_ada.shape[0]
    assert depth == 1 and tq == SUBLANES and seq % TOKEN_TILE == 0 and d == ROW_WIDTH
    assert seq % (2 * ROW_DMA_TILE) == 0 and (nseq * tq) % (2 * ROW_DMA_TILE) == 0
    wb = cache_k_win.shape[2]
    tp, ts = batch * seq, nseq * tq
    t_all = tp + ts
    tm_s = min(ts, TOKEN_TILE)
    assert ts % tm_s == 0 and tp % tm_s == 0 and t_all % ROW_DMA_TILE == 0
    nmod = w_ada.shape[2] // d

    n_c = batch + nseq
    c_rows = -(-n_c // SUBLANES) * SUBLANES
    c_all = jnp.concatenate([c_prompt, c_sample, jnp.zeros((c_rows - n_c, d), F32)], axis=0)
    mod = _ada(c_all, w_ada[0], b_ada[0])
    mod_p = mod[:batch].reshape(batch * nmod, 1, d)
    mod_s = jnp.repeat(mod[batch:n_c], tq, axis=0)
    tiles_per_seq = seq // TOKEN_TILE
    p_spec = lambda chunk: pl.BlockSpec((None, 1, d), lambda i: ((i // tiles_per_seq) * nmod + chunk, 0, 0))
    s_spec = lambda chunk: pl.BlockSpec((tm_s, d), lambda i: (i, chunk))

    q_lo = 3 * d
    q_hi = q_lo + N_HEADS * HEAD_DIM

    def q_cols(a):
        lead = a.shape[0]
        qa = a[:, q_lo:q_hi].reshape(lead, N_KV_HEADS, GROUP, HEAD_DIM).transpose(0, 2, 1, 3)
        qa = qa.reshape(lead, q_hi - q_lo) * (HEAD_DIM ** -0.5 * LOG2_E)
        return jnp.concatenate([a[:, :q_lo], qa, a[:, q_hi:]], axis=1)

    w_in_b = q_cols(w_in[0]).astype(BF16)
    b_in2 = q_cols(b_in[0].reshape(1, -1))
    nmix = norm_mix[0].reshape(1, d)
    cw = conv_w[0]

    xp = x_prompt.reshape(tp, d)
    ya_p, q_p, k_p, v_p, ga_p, gb_p, ut_p = _inproj(
        xp, mod_p, mod_p, (p_spec(0), p_spec(1)), nmix, w_in_b, b_in2, cw,
        tm=TOKEN_TILE, tiles_per_seq=tiles_per_seq)
    xs_ = x_sample.reshape(ts, d)
    st = state_conv[0]
    zpad = lambda a: jnp.pad(a, ((0, 0), (0, tq - a.shape[1]), (0, 0))).reshape(ts, d)
    hist1 = zpad(st[:, 1:2])
    hist2 = zpad(st)
    ya_s, q_s, k_s, v_s, ga_s, gb_s, u_s = _inproj(
        xs_, mod_s, mod_s, (s_spec(0), s_spec(1)), nmix, w_in_b, b_in2, cw,
        tm=tm_s, tiles_per_seq=0, hist=(hist1, hist2))

    sk = sinks[0]
    yb_p = _attn_prompt(q_p, k_p, v_p, sk, batch=batch, seq=seq)
    kbuf = cache_k_win[0].reshape(nseq, wb, KV_DIM)
    vbuf = cache_v_win[0].reshape(nseq, wb, KV_DIM)
    yb_s, nk_s, nv_s = _attn_sample(
        q_s.reshape(nseq, tq, d), k_s.reshape(nseq, tq, KV_DIM), v_s.reshape(nseq, tq, KV_DIM),
        kbuf, vbuf, sk, bs=SUBLANES)
    yb_s = yb_s.reshape(ts, d)

    wa = w_branch_a[0].astype(BF16)
    wbb = w_branch_b[0].reshape(N_KV_HEADS, GROUP, HEAD_DIM, d).transpose(1, 0, 2, 3)
    wbb = wbb.reshape(N_HEADS * HEAD_DIM, d).astype(BF16)
    wo = w_out[0].astype(BF16)
    bo = b_out[0].reshape(1, d)
    nffn = norm_ffn[0].reshape(1, d)
    wr0 = w_router[0].astype(BF16)
    wr1 = (w_router[0] - wr0.astype(F32)).astype(BF16)
    wr2 = (w_router[0] - wr0.astype(F32) - wr1.astype(F32)).astype(BF16)
    wr = jnp.concatenate([wr0, wr1, wr2], axis=1)
    br = b_router[0].reshape(1, -1)
    n_tiles_p = tp // tm_s
    pp_spec = lambda chunk: pl.BlockSpec(
        (None, 1, d),
        lambda i: ((jnp.minimum(i, n_tiles_p - 1) // (seq // tm_s)) * nmod + chunk, 0, 0))
    sp_spec = lambda chunk: pl.BlockSpec((tm_s, d), lambda i: (jnp.maximum(i - n_tiles_p, 0), chunk))
    h_all, hm_all, gm_all, km_all = _post(
        (xp, ya_p, yb_p, ga_p, gb_p, mod_p, mod_p, mod_p), (pp_spec(2), pp_spec(3), pp_spec(4)),
        (xs_, ya_s, yb_s, ga_s, gb_s, mod_s, mod_s, mod_s), (sp_spec(2), sp_spec(3), sp_spec(4)),
        wa, wbb, wo, bo, nffn, wr, br, tm=tm_s)

    block = EXPERT_BLOCK
    dest, counts = _route(km_all, tm=tm_s, block=block)
    n_assign = t_all * TOP_K
    n_blocks = -(-(n_assign + N_EXPERTS * (block - 1)) // block)
    cnt = counts[0].astype(jnp.int32)
    pend = jnp.cumsum((cnt + block - 1) // block * block)
    n_used = (pend[-1] // block).astype(jnp.int32)
    blk_ids = jnp.minimum(jnp.arange(n_blocks, dtype=jnp.int32), n_used - 1)
    block_e = jnp.sum((pend[None, :] <= (blk_ids * block)[:, None]).astype(jnp.int32), axis=1)
    block_e = jnp.minimum(block_e, N_EXPERTS - 1)
    row_end = (pend - (cnt + block - 1) // block * block + cnt)[block_e]
    n_valid = jnp.clip(row_end - blk_ids * block, 0, block).astype(jnp.int32)
    nonempty = cnt > 0
    ids = jnp.arange(N_EXPERTS, dtype=jnp.int32)
    later = jnp.where(nonempty[None, :] & (ids[None, :] > ids[:, None]), ids[None, :], N_EXPERTS)
    next_e = jnp.min(later, axis=1)
    next_e = jnp.where(next_e == N_EXPERTS, -1, next_e).astype(jnp.int32)[block_e]
    slot = ((jnp.cumsum(nonempty.astype(jnp.int32)) - 1) % 2).astype(jnp.int32)[block_e]

    dest = dest.reshape(TOP_K, t_all // ROW_DMA_TILE, ROW_DMA_TILE).transpose(1, 0, 2).reshape(-1)
    xs_rows = _dispatch(pend.astype(jnp.int32), dest, hm_all, tm=ROW_DMA_TILE,
                        n_slots=n_blocks * block, block=block)
    f2 = w_up.shape[3]
    bu = b_up[0].reshape(N_EXPERTS, f2 // PAIR_GROUP, LANES, 2).transpose(0, 1, 3, 2)
    bu = bu.reshape(N_EXPERTS, 1, f2)
    bd = b_down[0].reshape(N_EXPERTS, 1, d)
    ys = _experts(block_e, n_used.reshape(1), n_valid, next_e, slot, xs_rows, w_up[0], bu,
                  w_down[0], bd, block=block)

    nfin = norm_final.reshape(1, d)
    step_rows = 2 * ROW_DMA_TILE
    steps_p = tp // step_rows
    pf_spec = pl.BlockSpec((None, 1, d), lambda i: ((i // (seq // step_rows)) * nmod + 5, 0, 0))
    y_p = _final(dest, h_all, mod_p, pf_spec, gm_all, km_all, nfin, ys,
                 tm=ROW_DMA_TILE, step0=0, n_steps=steps_p)
    sf_spec = pl.BlockSpec((step_rows, d), lambda i: (i, 5))
    y_s = _final(dest, h_all, mod_s, sf_spec, gm_all, km_all, nfin, ys,
                 tm=ROW_DMA_TILE, step0=steps_p, n_steps=ts // step_rows)

    keep = min(WINDOW, seq)
    k4 = k_p.reshape(batch, seq, N_KV_HEADS, HEAD_DIM)[:, seq - keep:]
    v4 = v_p.reshape(batch, seq, N_KV_HEADS, HEAD_DIM)[:, seq - keep:]
    conv_p = ut_p.reshape(batch, tiles_per_seq, SUBLANES, d)[:, -1, SUBLANES - (CONV_WIDTH - 1):]
    conv_s = u_s.reshape(nseq, tq, d)[:, tq - (CONV_WIDTH - 1):]
    return (y_p.reshape(batch, seq, d), y_s.reshape(nseq, tq, d),
            k4[None], v4[None], conv_p[None],
            nk_s.reshape(1, nseq, wb, N_KV_HEADS, HEAD_DIM),
            nv_s.reshape(1, nseq, wb, N_KV_HEADS, HEAD_DIM), conv_s[None])
```

```python
import functools

import jax
import jax.numpy as jnp
from jax import lax
from jax.experimental import pallas as pl
from jax.experimental.pallas import tpu as pltpu

F32 = jnp.float32
BF16 = jnp.bfloat16
HIGHEST = lax.Precision.HIGHEST

N_HEADS = 16
N_KV_HEADS = 4
HEAD_DIM = 64
GROUP = N_HEADS // N_KV_HEADS
KV_DIM = N_KV_HEADS * HEAD_DIM
WINDOW = 128
N_EXPERTS = 32
TOP_K = 4
CONV_WIDTH = 3
SWIGLU_ALPHA = 1.702
SWIGLU_LIMIT = 7.0
RMS_EPS = 1e-5
ALIBI_SLOPES = tuple(2.0 ** (-8.0 * (h + 1) / N_HEADS) for h in range(N_HEADS))

SUBLANES = 8
LANES = 128
VMEM_LIMIT_BYTES = 56 * 1024 * 1024
TOKEN_TILE = 512
ROW_DMA_TILE = 256
ROW_DMA_UNROLL = 8
EXPERT_BLOCK = 512


def _cparams(*sem):
    return pltpu.CompilerParams(dimension_semantics=sem, vmem_limit_bytes=VMEM_LIMIT_BYTES)


def _rmsnorm(x, g):
    ms = jnp.mean(x * x, axis=-1, keepdims=True)
    return x * lax.rsqrt(ms + RMS_EPS) * g


def _ada_body(c_ref, w_ref, b_ref, o_ref):
    c = c_ref[...]
    a = c * jax.nn.sigmoid(c)
    o_ref[...] = jnp.dot(a, w_ref[...], preferred_element_type=F32, precision=HIGHEST) + b_ref[...]


def _ada(c_all, w_ada, b_ada):
    rows, d = c_all.shape
    n = w_ada.shape[1]
    return pl.pallas_call(
        _ada_body,
        grid=(n // d,),
        in_specs=[pl.BlockSpec((rows, d), lambda j: (0, 0)),
                  pl.BlockSpec((d, d), lambda j: (0, j)),
                  pl.BlockSpec((1, d), lambda j: (0, j))],
        out_specs=pl.BlockSpec((rows, d), lambda j: (0, j)),
        out_shape=jax.ShapeDtypeStruct((rows, n), F32),
        compiler_params=_cparams("arbitrary"),
        name="ada_modulation",
    )(c_all, w_ada, b_ada.reshape(1, n))


def _inproj_body(*refs, d, nc, tiles_per_seq):
    carry_mode = tiles_per_seq > 0
    if carry_mode:
        (x_ref, sh_ref, sc_ref, nw_ref, w_ref, b_ref, cw_ref,
         ya_ref, q_ref, k_ref, v_ref, ga_ref, gb_ref, ut_ref, carry_ref) = refs
    else:
        (x_ref, sh_ref, sc_ref, nw_ref, w_ref, b_ref, cw_ref, p1_ref, p2_ref,
         ya_ref, q_ref, k_ref, v_ref, ga_ref, gb_ref, ut_ref) = refs
    tm = x_ref.shape[0]
    xm = (_rmsnorm(x_ref[...], nw_ref[...]) * (1.0 + sc_ref[...]) + sh_ref[...]).astype(BF16)

    def proj(lo, width):
        return (jnp.dot(xm, w_ref[:, lo:lo + width], preferred_element_type=F32)
                + b_ref[:, lo:lo + width])

    if carry_mode:
        @pl.when(pl.program_id(0) % tiles_per_seq == 0)
        def _():
            carry_ref[...] = jnp.zeros_like(carry_ref)
    else:
        t_in_seq = lax.broadcasted_iota(jnp.int32, (tm, nc), 0) % SUBLANES

    for j in range(d // nc):
        c0 = j * nc
        u = proj(c0, nc) * proj(2 * d + c0, nc)
        if carry_mode:
            ue = jnp.concatenate([carry_ref[:, c0:c0 + nc], u], axis=0)
            s1 = pltpu.roll(ue, 1, axis=0)[SUBLANES:]
            s2 = pltpu.roll(ue, 2, axis=0)[SUBLANES:]
            carry_ref[:, c0:c0 + nc] = u[tm - SUBLANES:]
            ut_ref[:, c0:c0 + nc] = u[tm - SUBLANES:]
        else:
            s1 = jnp.where(t_in_seq >= 1, pltpu.roll(u, 1, axis=0), p1_ref[:, c0:c0 + nc])
            s2 = jnp.where(t_in_seq >= 2, pltpu.roll(u, 2, axis=0), p2_ref[:, c0:c0 + nc])
            ut_ref[:, c0:c0 + nc] = u
        cw = cw_ref[:, c0:c0 + nc]
        conv = s2 * cw[0:1] + s1 * cw[1:2] + u * cw[2:3]
        ya_ref[:, c0:c0 + nc] = (proj(d + c0, nc) * conv).astype(BF16)

    q_ref[...] = proj(3 * d, d).astype(BF16)
    k_ref[...] = proj(4 * d, KV_DIM)
    v_ref[...] = proj(4 * d + KV_DIM, KV_DIM)
    ga_ref[...] = jax.nn.sigmoid(proj(4 * d + 2 * KV_DIM, d)).astype(BF16)
    gb_ref[...] = jax.nn.sigmoid(proj(5 * d + 2 * KV_DIM, d)).astype(BF16)


def _inproj(x, shift, scale, shift_scale_specs, norm_w, w_in, b_in, conv_w, *, tm, tiles_per_seq,
            hist=None):
    t, d = x.shape
    n_in = w_in.shape[1]
    nc = 512
    carry_mode = tiles_per_seq > 0
    row = lambda width: pl.BlockSpec((tm, width), lambda i: (i, 0))
    full = lambda a: pl.BlockSpec(a.shape, lambda i: (0,) * a.ndim)
    in_specs = [row(d), shift_scale_specs[0], shift_scale_specs[1], full(norm_w), full(w_in),
                full(b_in), full(conv_w)]
    args = [x, shift, scale, norm_w, w_in, b_in, conv_w]
    if not carry_mode:
        in_specs += [row(d), row(d)]
        args += list(hist)
    ut_rows = SUBLANES if carry_mode else tm
    out_specs = [row(d), row(d), row(KV_DIM), row(KV_DIM), row(d), row(d),
                 pl.BlockSpec((ut_rows, d), lambda i: (i, 0))]
    out_shape = [jax.ShapeDtypeStruct((t, d), BF16), jax.ShapeDtypeStruct((t, d), BF16),
                 jax.ShapeDtypeStruct((t, KV_DIM), F32), jax.ShapeDtypeStruct((t, KV_DIM), F32),
                 jax.ShapeDtypeStruct((t, d), BF16), jax.ShapeDtypeStruct((t, d), BF16),
                 jax.ShapeDtypeStruct((t // tm * ut_rows, d), F32)]
    return pl.pallas_call(
        functools.partial(_inproj_body, d=d, nc=nc, tiles_per_seq=tiles_per_seq),
        grid=(t // tm,),
        in_specs=in_specs,
        out_specs=out_specs,
        out_shape=out_shape,
        scratch_shapes=[pltpu.VMEM((SUBLANES, d), F32)] if carry_mode else [],
        compiler_params=_cparams("arbitrary"),
        name="inproj_conv_long" if carry_mode else "inproj_conv_short",
    )(*args)


LOG2_E = 1.4426950408889634


def _softmax_rows(s):
    m = jnp.max(s, axis=-1, keepdims=True)
    p = jnp.exp2(s - m)
    den = jnp.sum(p, axis=-1, keepdims=True)
    return (p * (1.0 / den)).astype(BF16)


def _own_block(rows, rows_per_kv):
    r = lax.broadcasted_iota(jnp.int32, (rows, KV_DIM), 0)
    c = lax.broadcasted_iota(jnp.int32, (rows, KV_DIM), 1)
    return (c // HEAD_DIM) == (r // rows_per_kv) % N_KV_HEADS


def _attn_prompt_body(q_ref, kc_ref, kp_ref, vc_ref, vp_ref, bias_ref, o_ref):
    blk = q_ref.shape[0]
    kk = jnp.concatenate([kp_ref[...], kc_ref[...]], axis=0)
    vv = jnp.concatenate([vp_ref[...], vc_ref[...]], axis=0)
    col_kv = lax.broadcasted_iota(jnp.int32, kk.shape, 1) // HEAD_DIM
    not_sink = lax.broadcasted_iota(jnp.int32, kk.shape, 0) > 0
    kk_kv = [jnp.where((col_kv == kv) & not_sink, kk, 0.0).astype(BF16)
             for kv in range(N_KV_HEADS)]
    vv_kv = [jnp.where((col_kv == kv) & not_sink, vv, 0.0).astype(BF16)
             for kv in range(N_KV_HEADS)]

    def scores(g):
        a = q_ref[:, g * KV_DIM:(g + 1) * KV_DIM]
        return [lax.dot_general(a, kk_kv[kv], (((1,), (1,)), ((), ())),
                                preferred_element_type=F32)
                + bias_ref[g, kv * blk:(kv + 1) * blk, :] for kv in range(N_KV_HEADS)]

    def weighted_values(g, probs):
        out = jnp.dot(probs[0], vv_kv[0], preferred_element_type=F32)
        for kv in range(1, N_KV_HEADS):
            out = out + jnp.dot(probs[kv], vv_kv[kv], preferred_element_type=F32)
        o_ref[:, g * KV_DIM:(g + 1) * KV_DIM] = out.astype(BF16)

    s_next = scores(0)
    p_prev = None
    for g in range(GROUP):
        s_cur = s_next
        if g + 1 < GROUP:
            s_next = scores(g + 1)
        p_cur = [_softmax_rows(s) for s in s_cur]
        if p_prev is not None:
            weighted_values(g - 1, p_prev)
        p_prev = p_cur
    weighted_values(GROUP - 1, p_prev)


def _attn_table(sinks, tq_rows, n_keys, key_offset, first_block):
    assert key_offset >= WINDOW
    kv = jnp.arange(N_KV_HEADS)[None, :, None, None]
    g = jnp.arange(GROUP)[:, None, None, None]
    qi = jnp.arange(tq_rows)[None, None, :, None]
    kj = jnp.arange(n_keys)[None, None, None, :]
    slope = jnp.exp2(-8.0 * (kv * GROUP + g + 1).astype(F32) / N_HEADS)
    dist = qi + key_offset - kj
    valid = (dist >= 0) & (dist < WINDOW)
    if first_block:
        valid = valid & (kj >= key_offset)
    bias = jnp.where(valid, -slope * dist.astype(F32), -jnp.inf)
    sink = sinks.reshape(N_KV_HEADS, GROUP).T[:, :, None, None]
    bias = jnp.where(kj == 0, sink, bias) * LOG2_E
    return bias.reshape(GROUP, N_KV_HEADS * tq_rows, n_keys)


def _attn_prompt(q, k, v, sinks, *, batch, seq):
    t, d = q.shape
    blk = WINDOW
    nb = seq // blk
    rows = N_KV_HEADS * blk
    bias = jnp.stack([_attn_table(sinks, blk, 2 * blk, blk, True),
                      _attn_table(sinks, blk, 2 * blk, blk, False)])
    cur = lambda width: pl.BlockSpec((blk, width), lambda b, j: (b * nb + j, 0))
    prev = lambda width: pl.BlockSpec((blk, width), lambda b, j: (b * nb + jnp.maximum(j - 1, 0), 0))
    return pl.pallas_call(
        _attn_prompt_body,
        grid=(batch, nb),
        in_specs=[cur(d), cur(KV_DIM), prev(KV_DIM), cur(KV_DIM), prev(KV_DIM),
                  pl.BlockSpec((None, GROUP, rows, 2 * blk), lambda b, j: (jnp.minimum(j, 1), 0, 0, 0))],
        out_specs=cur(d),
        out_shape=jax.ShapeDtypeStruct((t, d), BF16),
        compiler_params=_cparams("arbitrary", "arbitrary"),
        name="attn_window_long",
    )(q, k, k, v, v, bias)


def _attn_sample_body(q_ref, kn_ref, vn_ref, kb_ref, vb_ref, bias_ref, o_ref, nk_ref, nv_ref):
    bs, tq, _ = q_ref.shape
    rows_g = N_KV_HEADS * tq
    rows = GROUP * rows_g
    own = _own_block(rows, tq)
    bias = bias_ref[...]
    not_sink = lax.broadcasted_iota(jnp.int32, (kb_ref.shape[1] + tq, KV_DIM), 0) > 0
    for b in range(bs):
        keys = jnp.concatenate([kb_ref[b], kn_ref[b]], axis=0)
        vals = jnp.concatenate([vb_ref[b], vn_ref[b]], axis=0)
        nk_ref[b] = keys[tq:]
        nv_ref[b] = vals[tq:]
        keys = jnp.where(not_sink, keys, 0.0).astype(BF16)
        vals = jnp.where(not_sink, vals, 0.0).astype(BF16)
        qb = q_ref[b].astype(F32)
        qbd = jnp.concatenate([qb[:, g * KV_DIM:(g + 1) * KV_DIM]
                               for g in range(GROUP) for _ in range(N_KV_HEADS)], axis=0)
        qbd = jnp.where(own, qbd, 0.0).astype(BF16)
        s = lax.dot_general(qbd, keys, (((1,), (1,)), ((), ())), preferred_element_type=F32)
        o = jnp.dot(_softmax_rows(s + bias), vals, preferred_element_type=F32)
        o = jnp.where(own, o, 0.0)
        outs = []
        for g in range(GROUP):
            out = o[g * rows_g:g * rows_g + tq]
            for kv in range(1, N_KV_HEADS):
                out = out + o[g * rows_g + kv * tq:g * rows_g + (kv + 1) * tq]
            outs.append(out)
        o_ref[b] = jnp.concatenate(outs, axis=1).astype(BF16)


def _attn_sample(q, k_new, v_new, k_buf, v_buf, sinks, *, bs):
    nseq, tq, d = q.shape
    wb = k_buf.shape[1]
    rows = N_HEADS * tq
    bias = _attn_table(sinks, tq, wb + tq, wb, False).reshape(rows, wb + tq)
    blk3 = lambda a, b_: pl.BlockSpec((bs, a, b_), lambda i: (i, 0, 0))
    return pl.pallas_call(
        _attn_sample_body,
        grid=(nseq // bs,),
        in_specs=[blk3(tq, d), blk3(tq, KV_DIM), blk3(tq, KV_DIM), blk3(wb, KV_DIM), blk3(wb, KV_DIM),
                  pl.BlockSpec((rows, wb + tq), lambda i: (0, 0))],
        out_specs=[blk3(tq, d), blk3(wb, KV_DIM), blk3(wb, KV_DIM)],
        out_shape=[jax.ShapeDtypeStruct((nseq, tq, d), BF16),
                   jax.ShapeDtypeStruct((nseq, wb, KV_DIM), F32),
                   jax.ShapeDtypeStruct((nseq, wb, KV_DIM), F32)],
        compiler_params=_cparams("arbitrary"),
        name="attn_window_short",
    )(q, k_new, v_new, k_buf, v_buf, bias)


N_POST_ACTS = 8


def _post_body(*refs, n_first):
    first, second = refs[:N_POST_ACTS], refs[N_POST_ACTS:2 * N_POST_ACTS]
    rest = refs[2 * N_POST_ACTS:]
    i = pl.program_id(0)

    @pl.when(i == 0)
    def _():
        rest[-1][...] = jnp.zeros_like(rest[-1])

    @pl.when(i < n_first)
    def _():
        _post_tile(*first, *rest)

    @pl.when(i >= n_first)
    def _():
        _post_tile(*second, *rest)


def _post_tile(x_ref, ya_ref, yb_ref, ga_ref, gb_ref, g1_ref, sh2_ref, sc2_ref, wa_ref, wb_ref,
               wo_ref, bo_ref, nf_ref, wr_ref, br_ref, h_ref, hm_ref, gm_ref, km_ref, cnt_ref):
    a = jnp.dot(ya_ref[...], wa_ref[...], preferred_element_type=F32)
    b = jnp.dot(yb_ref[...], wb_ref[...], preferred_element_type=F32)
    merged = ga_ref[...].astype(F32) * a + gb_ref[...].astype(F32) * b
    o = jnp.dot(merged.astype(BF16), wo_ref[...], preferred_element_type=F32) + bo_ref[...]
    h = x_ref[...] + g1_ref[...] * o
    h_ref[...] = h
    hm = _rmsnorm(h, nf_ref[...]) * (1.0 + sc2_ref[...]) + sh2_ref[...]
    for j, words in enumerate(_pack_rows(hm)):
        hm_ref[_chunk(j, hm.shape[0]), :] = words
    ne = br_ref.shape[1]
    hm_hi = hm.astype(BF16)
    hm_lo = (hm - hm_hi.astype(F32)).astype(BF16)
    wr = wr_ref[...]
    p_hi = jnp.dot(hm_hi, wr, preferred_element_type=F32)
    p_lo = jnp.dot(hm_lo, wr[:, :2 * ne], preferred_element_type=F32)
    logits = (p_hi[:, :ne] + (p_hi[:, ne:2 * ne] + p_lo[:, :ne])
              + (p_hi[:, 2 * ne:] + p_lo[:, ne:]) + br_ref[...])
    tm = logits.shape[0]
    lane = lax.broadcasted_iota(jnp.int32, (tm, ne), 1)
    vals = logits
    km = jnp.zeros((tm, ne), F32)
    top = []
    for k in range(TOP_K):
        m = jnp.max(vals, axis=-1, keepdims=True)
        first = jnp.min(jnp.where(vals == m, lane, ne), axis=-1, keepdims=True)
        sel = lane == first
        km = jnp.where(sel, float(k + 1), km)
        vals = jnp.where(sel, -jnp.inf, vals)
        top.append(m)
    e = [jnp.exp(t - top[0]) for t in top]
    den = e[0] + e[1] + e[2] + e[3]
    gm = jnp.zeros((tm, ne), F32)
    for k in range(TOP_K):
        gm = jnp.where(km == float(k + 1), e[k] / den, gm)
    gm_ref[...] = gm
    km_ref[...] = km
    cnt_ref[...] += jnp.sum((km > 0.0).astype(F32), axis=0, keepdims=True)


def _post(first, first_mod_specs, second, second_mod_specs, wa, wb, wo, bo, nf, wr, br, *, tm):
    t1, d = first[0].shape
    t2 = second[0].shape[0]
    n1, n2 = t1 // tm, t2 // tm
    ne = br.shape[1]
    row1 = pl.BlockSpec((tm, d), lambda i: (jnp.minimum(i, n1 - 1), 0))
    row2 = pl.BlockSpec((tm, d), lambda i: (jnp.maximum(i - n1, 0), 0))
    orow = lambda width: pl.BlockSpec((tm, width), lambda i: (i, 0))
    full = lambda a: pl.BlockSpec(a.shape, lambda i: (0,) * a.ndim)
    weights = [wa, wb, wo, bo, nf, wr, br]
    in_specs = ([row1] * 5 + list(first_mod_specs) + [row2] * 5 + list(second_mod_specs)
                + [full(w) for w in weights])
    return pl.pallas_call(
        functools.partial(_post_body, n_first=n1),
        grid=(n1 + n2,),
        in_specs=in_specs,
        out_specs=[orow(d), pl.BlockSpec((tm * ROW_TILES, LANES), lambda i: (i, 0)),
                   orow(ne), orow(ne), pl.BlockSpec((1, ne), lambda i: (0, 0))],
        out_shape=[jax.ShapeDtypeStruct((t1 + t2, d), F32),
                   jax.ShapeDtypeStruct(((t1 + t2) * ROW_TILES, LANES), PACKED),
                   jax.ShapeDtypeStruct((t1 + t2, ne), F32), jax.ShapeDtypeStruct((t1 + t2, ne), F32),
                   jax.ShapeDtypeStruct((1, ne), F32)],
        compiler_params=_cparams("arbitrary"),
        name="merge_outproj_router",
    )(*first, *second, *weights)


def _route_body(km_ref, cnt_ref, dest_ref, run_ref, start_ref, *, block):
    i = pl.program_id(0)
    km = km_ref[...]
    tm, ne = km.shape
    chosen = (km > 0.0).astype(F32)

    @pl.when(i == 0)
    def _():
        padded = jnp.ceil(cnt_ref[...] / block) * block
        r = lax.broadcasted_iota(jnp.int32, (ne, ne), 0)
        c = lax.broadcasted_iota(jnp.int32, (ne, ne), 1)
        before = (r < c).astype(F32)
        start_ref[...] = jnp.dot(padded, before, preferred_element_type=F32, precision=HIGHEST)
        run_ref[...] = jnp.zeros_like(run_ref)

    r = lax.broadcasted_iota(jnp.int32, (tm, tm), 0)
    c = lax.broadcasted_iota(jnp.int32, (tm, tm), 1)
    earlier = (c < r).astype(BF16)
    rank = jnp.dot(earlier, chosen.astype(BF16), preferred_element_type=F32)
    slot = start_ref[...] + run_ref[...] + rank
    ones = jnp.ones((SUBLANES, ne), F32)
    for k in range(TOP_K):
        sk = jnp.where(km == float(k + 1), slot, 0.0)
        rowk = lax.dot_general(ones, sk, (((1,), (1,)), ((), ())),
                               preferred_element_type=F32, precision=HIGHEST)
        dest_ref[k:k + 1, :] = rowk[0:1].astype(jnp.int32)
    run_ref[...] += jnp.sum(chosen, axis=0, keepdims=True)


def _route(km, counts, *, tm, block):
    t, ne = km.shape
    return pl.pallas_call(
        functools.partial(_route_body, block=block),
        grid=(t // tm,),
        in_specs=[pl.BlockSpec((tm, ne), lambda i: (i, 0)),
                  pl.BlockSpec((1, ne), lambda i: (0, 0))],
        out_specs=pl.BlockSpec((TOP_K, tm), lambda i: (0, i)),
        out_shape=jax.ShapeDtypeStruct((TOP_K, t), jnp.int32),
        scratch_shapes=[pltpu.VMEM((1, ne), F32), pltpu.VMEM((1, ne), F32)],
        compiler_params=_cparams("arbitrary"),
        name="route_slots",
    )(km, counts)


ROW_TILES = 4
ROW_WIDTH = ROW_TILES * 2 * LANES
PACKED = jnp.uint32
HIGH_HALF = 0xFFFF0000


def _row_copy(src, src_row, dst, dst_row, sem):
    src_at = pl.ds(pl.multiple_of(src_row * ROW_TILES, ROW_TILES), ROW_TILES)
    dst_at = pl.ds(pl.multiple_of(dst_row * ROW_TILES, ROW_TILES), ROW_TILES)
    return pltpu.make_async_copy(src.at[src_at, :], dst.at[dst_at, :], sem)


def _chunk(j, rows, row0=0):
    return pl.ds(row0 * ROW_TILES + j, rows, stride=ROW_TILES)


def _pack_rows(x):
    def bf16_bits(v):
        return lax.bitcast_convert_type(v.astype(BF16).astype(F32), PACKED)

    words = []
    for j in range(ROW_TILES):
        lo = bf16_bits(x[:, 2 * j * LANES:(2 * j + 1) * LANES]) >> 16
        hi = bf16_bits(x[:, (2 * j + 1) * LANES:(2 * j + 2) * LANES]) & jnp.uint32(HIGH_HALF)
        words.append(lo | hi)
    return words


def _unpack_words(u):
    lo = lax.bitcast_convert_type(u << 16, F32)
    hi = lax.bitcast_convert_type(u & jnp.uint32(HIGH_HALF), F32)
    return lo, hi


def _dispatch_body(pend_ref, dest_ref, hm_ref, xs_ref, zero_ref, sem, zsem):
    tm = hm_ref.shape[0] // ROW_TILES
    block = zero_ref.shape[0] // ROW_TILES

    @pl.when(pl.program_id(0) == 0)
    def _():
        zero_ref[...] = jnp.zeros_like(zero_ref)
        n_exp = pend_ref.shape[0]
        n_slots = xs_ref.shape[0] // ROW_TILES
        fills = []
        for e in range(n_exp):
            end = pend_ref[e]
            begin = pend_ref[e - 1] if e > 0 else 0
            fills.append((end > begin, end - block))
        for j in range(n_exp):
            begin = pend_ref[n_exp - 1] + j * block
            fills.append((begin < n_slots, begin))
        for phase in ("start", "wait"):
            for cond, begin in fills:
                @pl.when(cond)
                def _():
                    rows = block * ROW_TILES
                    fill = pltpu.make_async_copy(
                        zero_ref,
                        xs_ref.at[pl.ds(pl.multiple_of(begin * ROW_TILES, rows), rows), :], zsem)
                    fill.start() if phase == "start" else fill.wait()

    def issue(c, carry):
        for u in range(ROW_DMA_UNROLL):
            t = c * ROW_DMA_UNROLL + u
            for k in range(TOP_K):
                _row_copy(hm_ref, t, xs_ref, dest_ref[k * tm + t], sem).start(priority=k % 2)
        return carry

    lax.fori_loop(0, tm // ROW_DMA_UNROLL, issue, 0)

    def drain(c, carry):
        for u in range(ROW_DMA_UNROLL):
            for k in range(TOP_K):
                _row_copy(hm_ref, c * ROW_DMA_UNROLL + u, xs_ref, 0, sem).wait()
        return carry

    lax.fori_loop(0, tm // ROW_DMA_UNROLL, drain, 0)


def _dispatch(pend, dest, hm, *, tm, n_slots, block):
    t = hm.shape[0] // ROW_TILES
    return pl.pallas_call(
        _dispatch_body,
        grid_spec=pltpu.PrefetchScalarGridSpec(
            num_scalar_prefetch=1,
            grid=(t // tm,),
            in_specs=[pl.BlockSpec((TOP_K * tm,), lambda i, pe: (i,), memory_space=pltpu.SMEM),
                      pl.BlockSpec((tm * ROW_TILES, LANES), lambda i, pe: (i, 0))],
            out_specs=pl.BlockSpec(memory_space=pl.ANY),
            scratch_shapes=[pltpu.VMEM((block * ROW_TILES, LANES), PACKED), pltpu.SemaphoreType.DMA,
                            pltpu.SemaphoreType.DMA],
        ),
        out_shape=jax.ShapeDtypeStruct((n_slots * ROW_TILES, LANES), PACKED),
        compiler_params=_cparams("arbitrary"),
        name="moe_dispatch_rows",
    )(pend, dest, hm)


PAIR_GROUP = 2 * LANES
HALF_BLOCKS = 2


def _expert_body(be_ref, nb_ref, nvalid_ref, next_ref, slot_ref, xs_ref, bu_ref, bd_ref, wu_hbm,
                 wd_hbm, ys_ref, wu_f32, wd_f32, wu_bf, wd_bf, sems):
    i = pl.program_id(0)
    f2 = wu_f32.shape[2]
    used = i < nb_ref[0]
    e = be_ref[i]
    slot = slot_ref[i]
    new_expert = (i == 0) | (e != be_ref[jnp.maximum(i - 1, 0)])

    def fetch(expert, s):
        return (pltpu.make_async_copy(wu_hbm.at[expert], wu_f32.at[s], sems.at[s, 0]),
                pltpu.make_async_copy(wd_hbm.at[expert], wd_f32.at[s], sems.at[s, 1]))

    @pl.when(used & (i == 0))
    def _():
        for cp in fetch(e, slot):
            cp.start()

    @pl.when(used & new_expert)
    def _():
        for cp in fetch(e, slot):
            cp.wait()

        @pl.when(next_ref[i] >= 0)
        def _():
            for cp in fetch(next_ref[i], 1 - slot):
                cp.start()

        r = lax.broadcasted_iota(jnp.int32, (PAIR_GROUP, PAIR_GROUP), 0)
        c = lax.broadcasted_iota(jnp.int32, (PAIR_GROUP, PAIR_GROUP), 1)
        src = jnp.where(c < LANES, 2 * c, 2 * (c - LANES) + 1)
        perm = (r == src).astype(BF16)
        for g in range(f2 // PAIR_GROUP):
            cols = slice(g * PAIR_GROUP, (g + 1) * PAIR_GROUP)
            wu_bf[:, cols] = jnp.dot(wu_f32[slot, :, cols].astype(BF16), perm,
                                     preferred_element_type=F32).astype(BF16)
        wd_bf[...] = wd_f32[slot].astype(BF16)

    rows = xs_ref.shape[0] // ROW_TILES // HALF_BLOCKS

    def half(row0):
        x = jnp.concatenate([part for j in range(ROW_TILES)
                             for part in _unpack_words(xs_ref[_chunk(j, rows, row0), :])],
                            axis=1).astype(BF16)
        u = jnp.dot(x, wu_bf[...], preferred_element_type=F32) + bu_ref[...]
        acts = []
        for g in range(f2 // PAIR_GROUP):
            glu = jnp.minimum(u[:, g * PAIR_GROUP:g * PAIR_GROUP + LANES], SWIGLU_LIMIT)
            lin = jnp.clip(u[:, g * PAIR_GROUP + LANES:(g + 1) * PAIR_GROUP],
                           -SWIGLU_LIMIT, SWIGLU_LIMIT)
            acts.append((glu * jax.nn.sigmoid(SWIGLU_ALPHA * glu) * (lin + 1.0)).astype(BF16))
        act = jnp.concatenate(acts, axis=1)
        y = jnp.dot(act, wd_bf[...], preferred_element_type=F32) + bd_ref[...]
        for j, words in enumerate(_pack_rows(y)):
            ys_ref[_chunk(j, rows, row0), :] = words

    for hb in range(HALF_BLOCKS):
        live = used & (nvalid_ref[i] > hb * rows)

        @pl.when(live)
        def _():
            half(hb * rows)

        @pl.when(jnp.logical_not(live))
        def _():
            for j in range(ROW_TILES):
                ys_ref[_chunk(j, rows, hb * rows), :] = jnp.zeros((rows, LANES), PACKED)


def _experts(block_e, n_used, n_valid, next_e, slot, xs, wu, bu, wd, bd, *, block):
    n_slots = xs.shape[0] // ROW_TILES
    d, f2 = wu.shape[1:]
    n_blocks = n_slots // block
    rc_block = (block * ROW_TILES, LANES)
    xrow = pl.BlockSpec(rc_block,
                        lambda i, be, nb, *_: (jnp.maximum(jnp.minimum(i, nb[0] - 1), 0), 0))
    bias = lambda width: pl.BlockSpec((None, 1, width), lambda i, be, *_: (be[i], 0, 0))
    return pl.pallas_call(
        _expert_body,
        grid_spec=pltpu.PrefetchScalarGridSpec(
            num_scalar_prefetch=5,
            grid=(n_blocks,),
            in_specs=[xrow, bias(f2), bias(d), pl.BlockSpec(memory_space=pl.ANY),
                      pl.BlockSpec(memory_space=pl.ANY)],
            out_specs=pl.BlockSpec(rc_block, lambda i, *_: (i, 0)),
            scratch_shapes=[pltpu.VMEM((2, d, f2), F32), pltpu.VMEM((2, f2 // 2, d), F32),
                            pltpu.VMEM((d, f2), BF16), pltpu.VMEM((f2 // 2, d), BF16),
                            pltpu.SemaphoreType.DMA((2, 2))],
        ),
        out_shape=jax.ShapeDtypeStruct(xs.shape, PACKED),
        compiler_params=_cparams("arbitrary"),
        name="moe_expert_mlp",
    )(block_e, n_used, n_valid, next_e, slot, xs, bu, bd, wu, wd)


def _final_body(dest_ref, next_dest_ref, h_ref, g2_ref, gm_ref, km_ref, nw_ref, ys_ref, o_ref,
                buf_ref, sems):
    i = pl.program_id(0)
    tm = h_ref.shape[0] // 2

    def gather(dref, half):
        base = half * TOP_K * tm

        def issue(c, carry):
            for u in range(ROW_DMA_UNROLL):
                t = c * ROW_DMA_UNROLL + u
                for k in range(TOP_K):
                    _row_copy(ys_ref, dref[base + k * tm + t], buf_ref.at[half, k], t,
                              sems.at[half]).start(priority=k % 2)
            return carry

        lax.fori_loop(0, tm // ROW_DMA_UNROLL, issue, 0)

    def drain(half):
        def wait(c, carry):
            for u in range(ROW_DMA_UNROLL):
                for k in range(TOP_K):
                    _row_copy(ys_ref, 0, buf_ref.at[half, k], c * ROW_DMA_UNROLL + u,
                              sems.at[half]).wait()
            return carry

        lax.fori_loop(0, tm // ROW_DMA_UNROLL, wait, 0)

    def combine(half):
        rows = slice(half * tm, (half + 1) * tm)
        km = km_ref[rows, :]
        gm = gm_ref[rows, :]
        g2 = g2_ref[...] if g2_ref.shape[0] == 1 else g2_ref[rows, :]
        gates = [jnp.sum(jnp.where(km == float(k + 1), gm, 0.0), axis=-1, keepdims=True)
                 for k in range(TOP_K)]
        ssq = jnp.zeros((tm, 1), F32)
        for j in range(ROW_TILES):
            acc = [None, None]
            for k in range(TOP_K):
                for part, v in enumerate(_unpack_words(buf_ref[half, k, _chunk(j, tm), :])):
                    acc[part] = gates[k] * v if acc[part] is None else acc[part] + gates[k] * v
            for part in range(2):
                cols = slice((2 * j + part) * LANES, (2 * j + part + 1) * LANES)
                y = h_ref[rows, cols] + g2[:, cols] * acc[part]
                ssq = ssq + jnp.sum(y * y, axis=-1, keepdims=True)
                o_ref[rows, cols] = y
        inv = lax.rsqrt(ssq * (1.0 / ROW_WIDTH) + RMS_EPS)
        o_ref[rows, :] = o_ref[rows, :] * inv * nw_ref[...]

    @pl.when(i == 0)
    def _():
        gather(dest_ref, 0)

    gather(dest_ref, 1)
    drain(0)
    combine(0)

    @pl.when(i + 1 < pl.num_programs(0))
    def _():
        gather(next_dest_ref, 0)

    drain(1)
    combine(1)


def _final(dest, h, g2, g2_spec, gm, km, norm_w, ys, *, tm, step0, n_steps):
    d = h.shape[1]
    ne = gm.shape[1]
    row = lambda width: pl.BlockSpec((2 * tm, width), lambda i: (i + step0, 0))
    per_step = 2 * TOP_K * tm
    last = step0 + n_steps - 1
    return pl.pallas_call(
        _final_body,
        grid=(n_steps,),
        in_specs=[pl.BlockSpec((per_step,), lambda i: (i + step0,), memory_space=pltpu.SMEM),
                  pl.BlockSpec((per_step,), lambda i: (jnp.minimum(i + step0 + 1, last),),
                               memory_space=pltpu.SMEM),
                  row(d), g2_spec, row(ne), row(ne),
                  pl.BlockSpec(norm_w.shape, lambda i: (0, 0)),
                  pl.BlockSpec(memory_space=pl.ANY)],
        out_specs=pl.BlockSpec((2 * tm, d), lambda i: (i, 0)),
        out_shape=jax.ShapeDtypeStruct((n_steps * 2 * tm, d), F32),
        scratch_shapes=[pltpu.VMEM((2, TOP_K, tm * ROW_TILES, LANES), PACKED),
                        pltpu.SemaphoreType.DMA((2,))],
        compiler_params=_cparams("arbitrary"),
        name="moe_combine_final_norm",
    )(dest, dest, h, g2, gm, km, norm_w, ys)


def kernel(x_prompt, x_sample, cache_k_win, cache_v_win, state_conv, c_prompt, c_sample, w_ada, b_ada, norm_mix, norm_ffn, w_in, b_in, conv_w, sinks, w_branch_a, w_branch_b, w_out, b_out, w_router, b_router, w_up, b_up, w_down, b_down, norm_final):
    batch, seq, d = x_prompt.shape
    nseq, tq, _ = x_sample.shape
    depth = w_ada.shape[0]
    assert depth == 1 and tq == SUBLANES and seq % TOKEN_TILE == 0 and d == ROW_WIDTH
    assert seq % (2 * ROW_DMA_TILE) == 0 and (nseq * tq) % (2 * ROW_DMA_TILE) == 0
    wb = cache_k_win.shape[2]
    tp, ts = batch * seq, nseq * tq
    t_all = tp + ts
    tm_s = min(ts, TOKEN_TILE)
    assert ts % tm_s == 0 and tp % tm_s == 0 and t_all % ROW_DMA_TILE == 0
    nmod = w_ada.shape[2] // d

    n_c = batch + nseq
    c_rows = -(-n_c // SUBLANES) * SUBLANES
    c_all = jnp.concatenate([c_prompt, c_sample, jnp.zeros((c_rows - n_c, d), F32)], axis=0)
    mod = _ada(c_all, w_ada[0], b_ada[0])
    mod_p = mod[:batch].reshape(batch * nmod, 1, d)
    mod_s = jnp.repeat(mod[batch:n_c], tq, axis=0)
    tiles_per_seq = seq // TOKEN_TILE
    p_spec = lambda chunk: pl.BlockSpec((None, 1, d), lambda i: ((i // tiles_per_seq) * nmod + chunk, 0, 0))
    s_spec = lambda chunk: pl.BlockSpec((tm_s, d), lambda i: (i, chunk))

    q_lo = 3 * d
    q_hi = q_lo + N_HEADS * HEAD_DIM

    def q_cols(a):
        lead = a.shape[0]
        qa = a[:, q_lo:q_hi].reshape(lead, N_KV_HEADS, GROUP, HEAD_DIM).transpose(0, 2, 1, 3)
        qa = qa.reshape(lead, q_hi - q_lo) * (HEAD_DIM ** -0.5 * LOG2_E)
        return jnp.concatenate([a[:, :q_lo], qa, a[:, q_hi:]], axis=1)

    w_in_b = q_cols(w_in[0]).astype(BF16)
    b_in2 = q_cols(b_in[0].reshape(1, -1))
    nmix = norm_mix[0].reshape(1, d)
    cw = conv_w[0]

    xp = x_prompt.reshape(tp, d)
    ya_p, q_p, k_p, v_p, ga_p, gb_p, ut_p = _inproj(
        xp, mod_p, mod_p, (p_spec(0), p_spec(1)), nmix, w_in_b, b_in2, cw,
        tm=TOKEN_TILE, tiles_per_seq=tiles_per_seq)
    xs_ = x_sample.reshape(ts, d)
    st = state_conv[0]
    zpad = lambda a: jnp.pad(a, ((0, 0), (0, tq - a.shape[1]), (0, 0))).reshape(ts, d)
    hist1 = zpad(st[:, 1:2])
    hist2 = zpad(st)
    ya_s, q_s, k_s, v_s, ga_s, gb_s, u_s = _inproj(
        xs_, mod_s, mod_s, (s_spec(0), s_spec(1)), nmix, w_in_b, b_in2, cw,
        tm=tm_s, tiles_per_seq=0, hist=(hist1, hist2))

    sk = sinks[0]
    yb_p = _attn_prompt(q_p, k_p, v_p, sk, batch=batch, seq=seq)
    kbuf = cache_k_win[0].reshape(nseq, wb, KV_DIM)
    vbuf = cache_v_win[0].reshape(nseq, wb, KV_DIM)
    yb_s, nk_s, nv_s = _attn_sample(
        q_s.reshape(nseq, tq, d), k_s.reshape(nseq, tq, KV_DIM), v_s.reshape(nseq, tq, KV_DIM),
        kbuf, vbuf, sk, bs=SUBLANES)
    yb_s = yb_s.reshape(ts, d)

    wa = w_branch_a[0].astype(BF16)
    wbb = w_branch_b[0].reshape(N_KV_HEADS, GROUP, HEAD_DIM, d).transpose(1, 0, 2, 3)
    wbb = wbb.reshape(N_HEADS * HEAD_DIM, d).astype(BF16)
    wo = w_out[0].astype(BF16)
    bo = b_out[0].reshape(1, d)
    nffn = norm_ffn[0].reshape(1, d)
    wr0 = w_router[0].astype(BF16)
    wr1 = (w_router[0] - wr0.astype(F32)).astype(BF16)
    wr2 = (w_router[0] - wr0.astype(F32) - wr1.astype(F32)).astype(BF16)
    wr = jnp.concatenate([wr0, wr1, wr2], axis=1)
    br = b_router[0].reshape(1, -1)
    n_tiles_p = tp // tm_s
    pp_spec = lambda chunk: pl.BlockSpec(
        (None, 1, d),
        lambda i: ((jnp.minimum(i, n_tiles_p - 1) // (seq // tm_s)) * nmod + chunk, 0, 0))
    sp_spec = lambda chunk: pl.BlockSpec((tm_s, d), lambda i: (jnp.maximum(i - n_tiles_p, 0), chunk))
    h_all, hm_all, gm_all, km_all, counts = _post(
        (xp, ya_p, yb_p, ga_p, gb_p, mod_p, mod_p, mod_p), (pp_spec(2), pp_spec(3), pp_spec(4)),
        (xs_, ya_s, yb_s, ga_s, gb_s, mod_s, mod_s, mod_s), (sp_spec(2), sp_spec(3), sp_spec(4)),
        wa, wbb, wo, bo, nffn, wr, br, tm=tm_s)

    block = EXPERT_BLOCK
    dest = _route(km_all, counts, tm=tm_s, block=block)
    n_assign = t_all * TOP_K
    n_blocks = -(-(n_assign + N_EXPERTS * (block - 1)) // block)
    cnt = counts[0].astype(jnp.int32)
    pend = jnp.cumsum((cnt + block - 1) // block * block)
    n_used = (pend[-1] // block).astype(jnp.int32)
    blk_ids = jnp.minimum(jnp.arange(n_blocks, dtype=jnp.int32), n_used - 1)
    block_e = jnp.sum((pend[None, :] <= (blk_ids * block)[:, None]).astype(jnp.int32), axis=1)
    block_e = jnp.minimum(block_e, N_EXPERTS - 1)
    nonempty = cnt > 0
    ids = jnp.arange(N_EXPERTS, dtype=jnp.int32)
    later = jnp.where(nonempty[None, :] & (ids[None, :] > ids[:, None]), ids[None, :], N_EXPERTS)
    next_of = jnp.min(later, axis=1)
    next_of = jnp.where(next_of == N_EXPERTS, -1, next_of)
    slot_of = (jnp.cumsum(nonempty.astype(jnp.int32)) - 1) % 2
    row_end_of = pend - (cnt + block - 1) // block * block + cnt
    per_expert = jnp.stack([row_end_of, next_of, slot_of]).astype(jnp.int32)
    onehot = (block_e[None, :, None] == ids[None, None, :]).astype(jnp.int32)
    row_end, next_e, slot = jnp.sum(onehot * per_expert[:, None, :], axis=2)
    n_valid = jnp.clip(row_end - blk_ids * block, 0, block).astype(jnp.int32)

    dest = dest.reshape(TOP_K, t_all // ROW_DMA_TILE, ROW_DMA_TILE).transpose(1, 0, 2).reshape(-1)
    xs_rows = _dispatch(pend.astype(jnp.int32), dest, hm_all, tm=ROW_DMA_TILE,
                        n_slots=n_blocks * block, block=block)
    f2 = w_up.shape[3]
    bu = b_up[0].reshape(N_EXPERTS, f2 // PAIR_GROUP, LANES, 2).transpose(0, 1, 3, 2)
    bu = bu.reshape(N_EXPERTS, 1, f2)
    bd = b_down[0].reshape(N_EXPERTS, 1, d)
    ys = _experts(block_e, n_used.reshape(1), n_valid, next_e, slot, xs_rows, w_up[0], bu,
                  w_down[0], bd, block=block)

    nfin = norm_final.reshape(1, d)
    step_rows = 2 * ROW_DMA_TILE
    steps_p = tp // step_rows
    pf_spec = pl.BlockSpec((None, 1, d), lambda i: ((i // (seq // step_rows)) * nmod + 5, 0, 0))
    y_p = _final(dest, h_all, mod_p, pf_spec, gm_all, km_all, nfin, ys,
                 tm=ROW_DMA_TILE, step0=0, n_steps=steps_p)
    sf_spec = pl.BlockSpec((step_rows, d), lambda i: (i, 5))
    y_s = _final(dest, h_all, mod_s, sf_spec, gm_all, km_all, nfin, ys,
                 tm=ROW_DMA_TILE, step0=steps_p, n_steps=ts // step_rows)

    keep = min(WINDOW, seq)
    k4 = k_p.reshape(batch, seq, N_KV_HEADS, HEAD_DIM)[:, seq - keep:]
    v4 = v_p.reshape(batch, seq, N_KV_HEADS, HEAD_DIM)[:, seq - keep:]
    conv_p = ut_p.reshape(batch, tiles_per_seq, SUBLANES, d)[:, -1, SUBLANES - (CONV_WIDTH - 1):]
    conv_s = u_s.reshape(nseq, tq, d)[:, tq - (CONV_WIDTH - 1):]
    return (y_p.reshape(batch, seq, d), y_s.reshape(nseq, tq, d),
            k4[None], v4[None], conv_p[None],
            nk_s.reshape(1, nseq, wb, N_KV_HEADS, HEAD_DIM),
            nv_s.reshape(1, nseq, wb, N_KV_HEADS, HEAD_DIM), conv_s[None])
```

```python
import functools

import jax
import jax.numpy as jnp
from jax import lax
from jax.experimental import pallas as pl
from jax.experimental.pallas import tpu as pltpu

F32 = jnp.float32
BF16 = jnp.bfloat16
HIGHEST = lax.Precision.HIGHEST

N_HEADS = 16
N_KV_HEADS = 4
HEAD_DIM = 64
GROUP = N_HEADS // N_KV_HEADS
KV_DIM = N_KV_HEADS * HEAD_DIM
WINDOW = 128
N_EXPERTS = 32
TOP_K = 4
CONV_WIDTH = 3
SWIGLU_ALPHA = 1.702
SWIGLU_LIMIT = 7.0
RMS_EPS = 1e-5
ALIBI_SLOPES = tuple(2.0 ** (-8.0 * (h + 1) / N_HEADS) for h in range(N_HEADS))

SUBLANES = 8
LANES = 128
VMEM_LIMIT_BYTES = 56 * 1024 * 1024
TOKEN_TILE = 512
ROW_DMA_TILE = 256
ROW_DMA_UNROLL = 8
EXPERT_BLOCK = 1024


def _cparams(*sem):
    return pltpu.CompilerParams(dimension_semantics=sem, vmem_limit_bytes=VMEM_LIMIT_BYTES)


def _rmsnorm(x, g):
    ms = jnp.mean(x * x, axis=-1, keepdims=True)
    return x * lax.rsqrt(ms + RMS_EPS) * g


def _ada_body(c_ref, w_ref, b_ref, o_ref):
    c = c_ref[...]
    a = c * jax.nn.sigmoid(c)
    o_ref[...] = jnp.dot(a, w_ref[...], preferred_element_type=F32, precision=HIGHEST) + b_ref[...]


def _ada(c_all, w_ada, b_ada):
    rows, d = c_all.shape
    n = w_ada.shape[1]
    return pl.pallas_call(
        _ada_body,
        grid=(n // d,),
        in_specs=[pl.BlockSpec((rows, d), lambda j: (0, 0)),
                  pl.BlockSpec((d, d), lambda j: (0, j)),
                  pl.BlockSpec((1, d), lambda j: (0, j))],
        out_specs=pl.BlockSpec((rows, d), lambda j: (0, j)),
        out_shape=jax.ShapeDtypeStruct((rows, n), F32),
        compiler_params=_cparams("arbitrary"),
        name="ada_modulation",
    )(c_all, w_ada, b_ada.reshape(1, n))


def _inproj_body(*refs, d, nc, tiles_per_seq):
    carry_mode = tiles_per_seq > 0
    if carry_mode:
        (x_ref, sh_ref, sc_ref, nw_ref, w_ref, b_ref, cw_ref,
         ya_ref, q_ref, k_ref, v_ref, ga_ref, gb_ref, ut_ref, carry_ref) = refs
    else:
        (x_ref, sh_ref, sc_ref, nw_ref, w_ref, b_ref, cw_ref, p1_ref, p2_ref,
         ya_ref, q_ref, k_ref, v_ref, ga_ref, gb_ref, ut_ref) = refs
    tm = x_ref.shape[0]
    xm = (_rmsnorm(x_ref[...], nw_ref[...]) * (1.0 + sc_ref[...]) + sh_ref[...]).astype(BF16)

    def proj(lo, width):
        return (jnp.dot(xm, w_ref[:, lo:lo + width], preferred_element_type=F32)
                + b_ref[:, lo:lo + width])

    if carry_mode:
        @pl.when(pl.program_id(0) % tiles_per_seq == 0)
        def _():
            carry_ref[...] = jnp.zeros_like(carry_ref)
    else:
        t_in_seq = lax.broadcasted_iota(jnp.int32, (tm, nc), 0) % SUBLANES

    for j in range(d // nc):
        c0 = j * nc
        u = proj(c0, nc) * proj(2 * d + c0, nc)
        if carry_mode:
            ue = jnp.concatenate([carry_ref[:, c0:c0 + nc], u], axis=0)
            s1 = pltpu.roll(ue, 1, axis=0)[SUBLANES:]
            s2 = pltpu.roll(ue, 2, axis=0)[SUBLANES:]
            carry_ref[:, c0:c0 + nc] = u[tm - SUBLANES:]
            ut_ref[:, c0:c0 + nc] = u[tm - SUBLANES:]
        else:
            s1 = jnp.where(t_in_seq >= 1, pltpu.roll(u, 1, axis=0), p1_ref[:, c0:c0 + nc])
            s2 = jnp.where(t_in_seq >= 2, pltpu.roll(u, 2, axis=0), p2_ref[:, c0:c0 + nc])
            ut_ref[:, c0:c0 + nc] = u
        cw = cw_ref[:, c0:c0 + nc]
        conv = s2 * cw[0:1] + s1 * cw[1:2] + u * cw[2:3]
        ya_ref[:, c0:c0 + nc] = (proj(d + c0, nc) * conv).astype(BF16)

    q_ref[...] = proj(3 * d, d).astype(BF16)
    k_ref[...] = proj(4 * d, KV_DIM)
    v_ref[...] = proj(4 * d + KV_DIM, KV_DIM)
    ga_ref[...] = jax.nn.sigmoid(proj(4 * d + 2 * KV_DIM, d)).astype(BF16)
    gb_ref[...] = jax.nn.sigmoid(proj(5 * d + 2 * KV_DIM, d)).astype(BF16)


def _inproj(x, shift, scale, shift_scale_specs, norm_w, w_in, b_in, conv_w, *, tm, tiles_per_seq,
            hist=None):
    t, d = x.shape
    n_in = w_in.shape[1]
    nc = 512
    carry_mode = tiles_per_seq > 0
    row = lambda width: pl.BlockSpec((tm, width), lambda i: (i, 0))
    full = lambda a: pl.BlockSpec(a.shape, lambda i: (0,) * a.ndim)
    in_specs = [row(d), shift_scale_specs[0], shift_scale_specs[1], full(norm_w), full(w_in),
                full(b_in), full(conv_w)]
    args = [x, shift, scale, norm_w, w_in, b_in, conv_w]
    if not carry_mode:
        in_specs += [row(d), row(d)]
        args += list(hist)
    ut_rows = SUBLANES if carry_mode else tm
    out_specs = [row(d), row(d), row(KV_DIM), row(KV_DIM), row(d), row(d),
                 pl.BlockSpec((ut_rows, d), lambda i: (i, 0))]
    out_shape = [jax.ShapeDtypeStruct((t, d), BF16), jax.ShapeDtypeStruct((t, d), BF16),
                 jax.ShapeDtypeStruct((t, KV_DIM), F32), jax.ShapeDtypeStruct((t, KV_DIM), F32),
                 jax.ShapeDtypeStruct((t, d), BF16), jax.ShapeDtypeStruct((t, d), BF16),
                 jax.ShapeDtypeStruct((t // tm * ut_rows, d), F32)]
    return pl.pallas_call(
        functools.partial(_inproj_body, d=d, nc=nc, tiles_per_seq=tiles_per_seq),
        grid=(t // tm,),
        in_specs=in_specs,
        out_specs=out_specs,
        out_shape=out_shape,
        scratch_shapes=[pltpu.VMEM((SUBLANES, d), F32)] if carry_mode else [],
        compiler_params=_cparams("arbitrary"),
        name="inproj_conv_long" if carry_mode else "inproj_conv_short",
    )(*args)


LOG2_E = 1.4426950408889634


def _softmax_rows(s):
    m = jnp.max(s, axis=-1, keepdims=True)
    p = jnp.exp2(s - m)
    den = jnp.sum(p, axis=-1, keepdims=True)
    return (p * (1.0 / den)).astype(BF16)


def _own_block(rows, rows_per_kv):
    r = lax.broadcasted_iota(jnp.int32, (rows, KV_DIM), 0)
    c = lax.broadcasted_iota(jnp.int32, (rows, KV_DIM), 1)
    return (c // HEAD_DIM) == (r // rows_per_kv) % N_KV_HEADS


def _attn_prompt_body(q_ref, kc_ref, kp_ref, vc_ref, vp_ref, bias_ref, o_ref):
    blk = kp_ref.shape[0]
    first_variant = jnp.minimum(pl.program_id(1), 1)
    _attn_block(q_ref, o_ref, slice(0, blk), bias_ref.at[first_variant],
                jnp.concatenate([kp_ref[...], kc_ref[0:blk, :]], axis=0),
                jnp.concatenate([vp_ref[...], vc_ref[0:blk, :]], axis=0))
    _attn_block(q_ref, o_ref, slice(blk, 2 * blk), bias_ref.at[1], kc_ref[...], vc_ref[...])


def _attn_block(q_ref, o_ref, qrows, bias_ref, kk, vv):
    blk = qrows.stop - qrows.start
    col_kv = lax.broadcasted_iota(jnp.int32, kk.shape, 1) // HEAD_DIM
    not_sink = lax.broadcasted_iota(jnp.int32, kk.shape, 0) > 0
    kk_kv = [jnp.where((col_kv == kv) & not_sink, kk, 0.0).astype(BF16)
             for kv in range(N_KV_HEADS)]
    vv_kv = [jnp.where((col_kv == kv) & not_sink, vv, 0.0).astype(BF16)
             for kv in range(N_KV_HEADS)]

    def scores(g):
        a = q_ref[qrows, g * KV_DIM:(g + 1) * KV_DIM]
        return [lax.dot_general(a, kk_kv[kv], (((1,), (1,)), ((), ())),
                                preferred_element_type=F32)
                + bias_ref[g, kv * blk:(kv + 1) * blk, :] for kv in range(N_KV_HEADS)]

    def weighted_values(g, probs):
        out = jnp.dot(probs[0], vv_kv[0], preferred_element_type=F32)
        for kv in range(1, N_KV_HEADS):
            out = out + jnp.dot(probs[kv], vv_kv[kv], preferred_element_type=F32)
        o_ref[qrows, g * KV_DIM:(g + 1) * KV_DIM] = out.astype(BF16)

    s_next = scores(0)
    p_prev = None
    for g in range(GROUP):
        s_cur = s_next
        if g + 1 < GROUP:
            s_next = scores(g + 1)
        p_cur = [_softmax_rows(s) for s in s_cur]
        if p_prev is not None:
            weighted_values(g - 1, p_prev)
        p_prev = p_cur
    weighted_values(GROUP - 1, p_prev)


def _attn_table(sinks, tq_rows, n_keys, key_offset, first_block):
    assert key_offset >= WINDOW
    kv = jnp.arange(N_KV_HEADS)[None, :, None, None]
    g = jnp.arange(GROUP)[:, None, None, None]
    qi = jnp.arange(tq_rows)[None, None, :, None]
    kj = jnp.arange(n_keys)[None, None, None, :]
    slope = jnp.exp2(-8.0 * (kv * GROUP + g + 1).astype(F32) / N_HEADS)
    dist = qi + key_offset - kj
    valid = (dist >= 0) & (dist < WINDOW)
    if first_block:
        valid = valid & (kj >= key_offset)
    bias = jnp.where(valid, -slope * dist.astype(F32), -jnp.inf)
    sink = sinks.reshape(N_KV_HEADS, GROUP).T[:, :, None, None]
    bias = jnp.where(kj == 0, sink, bias) * LOG2_E
    return bias.reshape(GROUP, N_KV_HEADS * tq_rows, n_keys)


def _attn_prompt(q, k, v, sinks, *, batch, seq):
    t, d = q.shape
    blk = WINDOW
    nb = seq // blk
    rows = N_KV_HEADS * blk
    bias = jnp.stack([_attn_table(sinks, blk, 2 * blk, blk, True),
                      _attn_table(sinks, blk, 2 * blk, blk, False)])
    assert nb % 2 == 0
    steps = nb // 2
    cur = lambda width: pl.BlockSpec((2 * blk, width), lambda b, j: (b * steps + j, 0))
    prev = lambda width: pl.BlockSpec((blk, width),
                                      lambda b, j: (b * nb + jnp.maximum(2 * j - 1, 0), 0))
    return pl.pallas_call(
        _attn_prompt_body,
        grid=(batch, steps),
        in_specs=[cur(d), cur(KV_DIM), prev(KV_DIM), cur(KV_DIM), prev(KV_DIM),
                  pl.BlockSpec((2, GROUP, rows, 2 * blk), lambda b, j: (0, 0, 0, 0))],
        out_specs=cur(d),
        out_shape=jax.ShapeDtypeStruct((t, d), BF16),
        compiler_params=_cparams("arbitrary", "arbitrary"),
        name="attn_window_long",
    )(q, k, k, v, v, bias)


def _attn_sample_body(q_ref, kn_ref, vn_ref, kb_ref, vb_ref, bias_ref, o_ref, nk_ref, nv_ref):
    bs, tq, _ = q_ref.shape
    rows_g = N_KV_HEADS * tq
    rows = GROUP * rows_g
    own = _own_block(rows, tq)
    bias = bias_ref[...]
    not_sink = lax.broadcasted_iota(jnp.int32, (kb_ref.shape[1] + tq, KV_DIM), 0) > 0
    for b in range(bs):
        keys = jnp.concatenate([kb_ref[b], kn_ref[b]], axis=0)
        vals = jnp.concatenate([vb_ref[b], vn_ref[b]], axis=0)
        nk_ref[b] = keys[tq:]
        nv_ref[b] = vals[tq:]
        keys = jnp.where(not_sink, keys, 0.0).astype(BF16)
        vals = jnp.where(not_sink, vals, 0.0).astype(BF16)
        qb = q_ref[b].astype(F32)
        qbd = jnp.concatenate([qb[:, g * KV_DIM:(g + 1) * KV_DIM]
                               for g in range(GROUP) for _ in range(N_KV_HEADS)], axis=0)
        qbd = jnp.where(own, qbd, 0.0).astype(BF16)
        s = lax.dot_general(qbd, keys, (((1,), (1,)), ((), ())), preferred_element_type=F32)
        o = jnp.dot(_softmax_rows(s + bias), vals, preferred_element_type=F32)
        o = jnp.where(own, o, 0.0)
        outs = []
        for g in range(GROUP):
            out = o[g * rows_g:g * rows_g + tq]
            for kv in range(1, N_KV_HEADS):
                out = out + o[g * rows_g + kv * tq:g * rows_g + (kv + 1) * tq]
            outs.append(out)
        o_ref[b] = jnp.concatenate(outs, axis=1).astype(BF16)


def _attn_sample(q, k_new, v_new, k_buf, v_buf, sinks, *, bs):
    nseq, tq, d = q.shape
    wb = k_buf.shape[1]
    rows = N_HEADS * tq
    bias = _attn_table(sinks, tq, wb + tq, wb, False).reshape(rows, wb + tq)
    blk3 = lambda a, b_: pl.BlockSpec((bs, a, b_), lambda i: (i, 0, 0))
    return pl.pallas_call(
        _attn_sample_body,
        grid=(nseq // bs,),
        in_specs=[blk3(tq, d), blk3(tq, KV_DIM), blk3(tq, KV_DIM), blk3(wb, KV_DIM), blk3(wb, KV_DIM),
                  pl.BlockSpec((rows, wb + tq), lambda i: (0, 0))],
        out_specs=[blk3(tq, d), blk3(wb, KV_DIM), blk3(wb, KV_DIM)],
        out_shape=[jax.ShapeDtypeStruct((nseq, tq, d), BF16),
                   jax.ShapeDtypeStruct((nseq, wb, KV_DIM), F32),
                   jax.ShapeDtypeStruct((nseq, wb, KV_DIM), F32)],
        compiler_params=_cparams("arbitrary"),
        name="attn_window_short",
    )(q, k_new, v_new, k_buf, v_buf, bias)


N_POST_ACTS = 8


def _post_body(*refs, n_first):
    first, second = refs[:N_POST_ACTS], refs[N_POST_ACTS:2 * N_POST_ACTS]
    rest = refs[2 * N_POST_ACTS:]
    i = pl.program_id(0)

    @pl.when(i == 0)
    def _():
        rest[-1][...] = jnp.zeros_like(rest[-1])

    @pl.when(i < n_first)
    def _():
        _post_tile(*first, *rest)

    @pl.when(i >= n_first)
    def _():
        _post_tile(*second, *rest)


def _post_tile(x_ref, ya_ref, yb_ref, ga_ref, gb_ref, g1_ref, sh2_ref, sc2_ref, wa_ref, wb_ref,
               wo_ref, bo_ref, nf_ref, wr_ref, br_ref, h_ref, hm_ref, gm_ref, km_ref, cnt_ref):
    a = jnp.dot(ya_ref[...], wa_ref[...], preferred_element_type=F32)
    b = jnp.dot(yb_ref[...], wb_ref[...], preferred_element_type=F32)
    merged = ga_ref[...].astype(F32) * a + gb_ref[...].astype(F32) * b
    o = jnp.dot(merged.astype(BF16), wo_ref[...], preferred_element_type=F32) + bo_ref[...]
    h = x_ref[...] + g1_ref[...] * o
    h_ref[...] = h
    hm = _rmsnorm(h, nf_ref[...]) * (1.0 + sc2_ref[...]) + sh2_ref[...]
    for j, words in enumerate(_pack_rows(hm)):
        hm_ref[_chunk(j, hm.shape[0]), :] = words
    ne = br_ref.shape[1]
    hm_hi = hm.astype(BF16)
    hm_lo = (hm - hm_hi.astype(F32)).astype(BF16)
    wr = wr_ref[...]
    p_hi = jnp.dot(hm_hi, wr, preferred_element_type=F32)
    p_lo = jnp.dot(hm_lo, wr[:, :2 * ne], preferred_element_type=F32)
    logits = (p_hi[:, :ne] + (p_hi[:, ne:2 * ne] + p_lo[:, :ne])
              + (p_hi[:, 2 * ne:] + p_lo[:, ne:]) + br_ref[...])
    tm = logits.shape[0]
    lane = lax.broadcasted_iota(jnp.int32, (tm, ne), 1)
    vals = logits
    km = jnp.zeros((tm, ne), F32)
    top = []
    for k in range(TOP_K):
        m = jnp.max(vals, axis=-1, keepdims=True)
        first = jnp.min(jnp.where(vals == m, lane, ne), axis=-1, keepdims=True)
        sel = lane == first
        km = jnp.where(sel, float(k + 1), km)
        vals = jnp.where(sel, -jnp.inf, vals)
        top.append(m)
    e = [jnp.exp(t - top[0]) for t in top]
    den = e[0] + e[1] + e[2] + e[3]
    gm = jnp.zeros((tm, ne), F32)
    for k in range(TOP_K):
        gm = jnp.where(km == float(k + 1), e[k] / den, gm)
    gm_ref[...] = gm
    km_ref[...] = km
    cnt_ref[...] += jnp.sum((km > 0.0).astype(F32), axis=0, keepdims=True)


def _post(first, first_mod_specs, second, second_mod_specs, wa, wb, wo, bo, nf, wr, br, *, tm):
    t1, d = first[0].shape
    t2 = second[0].shape[0]
    n1, n2 = t1 // tm, t2 // tm
    ne = br.shape[1]
    row1 = pl.BlockSpec((tm, d), lambda i: (jnp.minimum(i, n1 - 1), 0))
    row2 = pl.BlockSpec((tm, d), lambda i: (jnp.maximum(i - n1, 0), 0))
    orow = lambda width: pl.BlockSpec((tm, width), lambda i: (i, 0))
    full = lambda a: pl.BlockSpec(a.shape, lambda i: (0,) * a.ndim)
    weights = [wa, wb, wo, bo, nf, wr, br]
    in_specs = ([row1] * 5 + list(first_mod_specs) + [row2] * 5 + list(second_mod_specs)
                + [full(w) for w in weights])
    return pl.pallas_call(
        functools.partial(_post_body, n_first=n1),
        grid=(n1 + n2,),
        in_specs=in_specs,
        out_specs=[orow(d), pl.BlockSpec((tm * ROW_TILES, LANES), lambda i: (i, 0)),
                   orow(ne), orow(ne), pl.BlockSpec((1, ne), lambda i: (0, 0))],
        out_shape=[jax.ShapeDtypeStruct((t1 + t2, d), F32),
                   jax.ShapeDtypeStruct(((t1 + t2) * ROW_TILES, LANES), PACKED),
                   jax.ShapeDtypeStruct((t1 + t2, ne), F32), jax.ShapeDtypeStruct((t1 + t2, ne), F32),
                   jax.ShapeDtypeStruct((1, ne), F32)],
        compiler_params=_cparams("arbitrary"),
        name="merge_outproj_router",
    )(*first, *second, *weights)


def _route_body(km_ref, cnt_ref, dest_ref, run_ref, start_ref, *, block):
    i = pl.program_id(0)
    km = km_ref[...]
    tm, ne = km.shape
    chosen = (km > 0.0).astype(F32)

    @pl.when(i == 0)
    def _():
        padded = jnp.ceil(cnt_ref[...] / block) * block
        r = lax.broadcasted_iota(jnp.int32, (ne, ne), 0)
        c = lax.broadcasted_iota(jnp.int32, (ne, ne), 1)
        before = (r < c).astype(F32)
        start_ref[...] = jnp.dot(padded, before, preferred_element_type=F32, precision=HIGHEST)
        run_ref[...] = jnp.zeros_like(run_ref)

    r = lax.broadcasted_iota(jnp.int32, (tm, tm), 0)
    c = lax.broadcasted_iota(jnp.int32, (tm, tm), 1)
    earlier = (c < r).astype(BF16)
    rank = jnp.dot(earlier, chosen.astype(BF16), preferred_element_type=F32)
    slot = start_ref[...] + run_ref[...] + rank
    ones = jnp.ones((SUBLANES, ne), F32)
    for k in range(TOP_K):
        sk = jnp.where(km == float(k + 1), slot, 0.0)
        rowk = lax.dot_general(ones, sk, (((1,), (1,)), ((), ())),
                               preferred_element_type=F32, precision=HIGHEST)
        dest_ref[k:k + 1, :] = rowk[0:1].astype(jnp.int32)
    run_ref[...] += jnp.sum(chosen, axis=0, keepdims=True)


def _route(km, counts, *, tm, block):
    t, ne = km.shape
    return pl.pallas_call(
        functools.partial(_route_body, block=block),
        grid=(t // tm,),
        in_specs=[pl.BlockSpec((tm, ne), lambda i: (i, 0)),
                  pl.BlockSpec((1, ne), lambda i: (0, 0))],
        out_specs=pl.BlockSpec((TOP_K, tm), lambda i: (0, i)),
        out_shape=jax.ShapeDtypeStruct((TOP_K, t), jnp.int32),
        scratch_shapes=[pltpu.VMEM((1, ne), F32), pltpu.VMEM((1, ne), F32)],
        compiler_params=_cparams("arbitrary"),
        name="route_slots",
    )(km, counts)


ROW_TILES = 4
ROW_WIDTH = ROW_TILES * 2 * LANES
PACKED = jnp.uint32
HIGH_HALF = 0xFFFF0000


def _row_copy(src, src_row, dst, dst_row, sem):
    src_at = pl.ds(pl.multiple_of(src_row * ROW_TILES, ROW_TILES), ROW_TILES)
    dst_at = pl.ds(pl.multiple_of(dst_row * ROW_TILES, ROW_TILES), ROW_TILES)
    return pltpu.make_async_copy(src.at[src_at, :], dst.at[dst_at, :], sem)


def _chunk(j, rows, row0=0):
    return pl.ds(row0 * ROW_TILES + j, rows, stride=ROW_TILES)


def _pack_rows(x):
    def bf16_bits(v):
        return lax.bitcast_convert_type(v.astype(BF16).astype(F32), PACKED)

    words = []
    for j in range(ROW_TILES):
        lo = bf16_bits(x[:, 2 * j * LANES:(2 * j + 1) * LANES]) >> 16
        hi = bf16_bits(x[:, (2 * j + 1) * LANES:(2 * j + 2) * LANES]) & jnp.uint32(HIGH_HALF)
        words.append(lo | hi)
    return words


def _unpack_words(u):
    lo = lax.bitcast_convert_type(u << 16, F32)
    hi = lax.bitcast_convert_type(u & jnp.uint32(HIGH_HALF), F32)
    return lo, hi


def _dispatch_body(pend_ref, dest_ref, hm_ref, xs_ref, zero_ref, sem, zsem):
    tm = hm_ref.shape[0] // ROW_TILES
    block = zero_ref.shape[0] // ROW_TILES

    @pl.when(pl.program_id(0) == 0)
    def _():
        zero_ref[...] = jnp.zeros_like(zero_ref)
        n_exp = pend_ref.shape[0]
        n_slots = xs_ref.shape[0] // ROW_TILES
        fills = []
        for e in range(n_exp):
            end = pend_ref[e]
            begin = pend_ref[e - 1] if e > 0 else 0
            fills.append((end > begin, end - block))
        for j in range(n_exp):
            begin = pend_ref[n_exp - 1] + j * block
            fills.append((begin < n_slots, begin))
        for phase in ("start", "wait"):
            for cond, begin in fills:
                @pl.when(cond)
                def _():
                    rows = block * ROW_TILES
                    fill = pltpu.make_async_copy(
                        zero_ref,
                        xs_ref.at[pl.ds(pl.multiple_of(begin * ROW_TILES, rows), rows), :], zsem)
                    fill.start() if phase == "start" else fill.wait()

    def issue(c, carry):
        for u in range(ROW_DMA_UNROLL):
            t = c * ROW_DMA_UNROLL + u
            for k in range(TOP_K):
                _row_copy(hm_ref, t, xs_ref, dest_ref[k * tm + t], sem).start(priority=k % 2)
        return carry

    lax.fori_loop(0, tm // ROW_DMA_UNROLL, issue, 0)

    def drain(c, carry):
        for u in range(ROW_DMA_UNROLL):
            for k in range(TOP_K):
                _row_copy(hm_ref, c * ROW_DMA_UNROLL + u, xs_ref, 0, sem).wait()
        return carry

    lax.fori_loop(0, tm // ROW_DMA_UNROLL, drain, 0)


def _dispatch(pend, dest, hm, *, tm, n_slots, block):
    t = hm.shape[0] // ROW_TILES
    return pl.pallas_call(
        _dispatch_body,
        grid_spec=pltpu.PrefetchScalarGridSpec(
            num_scalar_prefetch=1,
            grid=(t // tm,),
            in_specs=[pl.BlockSpec((TOP_K * tm,), lambda i, pe: (i,), memory_space=pltpu.SMEM),
                      pl.BlockSpec((tm * ROW_TILES, LANES), lambda i, pe: (i, 0))],
            out_specs=pl.BlockSpec(memory_space=pl.ANY),
            scratch_shapes=[pltpu.VMEM((block * ROW_TILES, LANES), PACKED), pltpu.SemaphoreType.DMA,
                            pltpu.SemaphoreType.DMA],
        ),
        out_shape=jax.ShapeDtypeStruct((n_slots * ROW_TILES, LANES), PACKED),
        compiler_params=_cparams("arbitrary"),
        name="moe_dispatch_rows",
    )(pend, dest, hm)


PAIR_GROUP = 2 * LANES
HALF_BLOCKS = 4


def _expert_body(be_ref, nb_ref, nvalid_ref, next_ref, slot_ref, xs_ref, bu_ref, bd_ref, wu_hbm,
                 wd_hbm, ys_ref, wu_f32, wd_f32, wu_bf, wd_bf, sems):
    i = pl.program_id(0)
    f2 = wu_f32.shape[2]
    used = i < nb_ref[0]
    e = be_ref[i]
    slot = slot_ref[i]
    new_expert = (i == 0) | (e != be_ref[jnp.maximum(i - 1, 0)])

    def fetch(expert, s):
        return (pltpu.make_async_copy(wu_hbm.at[expert], wu_f32.at[s], sems.at[s, 0]),
                pltpu.make_async_copy(wd_hbm.at[expert], wd_f32.at[s], sems.at[s, 1]))

    @pl.when(used & (i == 0))
    def _():
        for cp in fetch(e, slot):
            cp.start()

    @pl.when(used & new_expert)
    def _():
        for cp in fetch(e, slot):
            cp.wait()

        @pl.when(next_ref[i] >= 0)
        def _():
            for cp in fetch(next_ref[i], 1 - slot):
                cp.start()

        r = lax.broadcasted_iota(jnp.int32, (PAIR_GROUP, PAIR_GROUP), 0)
        c = lax.broadcasted_iota(jnp.int32, (PAIR_GROUP, PAIR_GROUP), 1)
        src = jnp.where(c < LANES, 2 * c, 2 * (c - LANES) + 1)
        perm = (r == src).astype(BF16)
        for g in range(f2 // PAIR_GROUP):
            cols = slice(g * PAIR_GROUP, (g + 1) * PAIR_GROUP)
            wu_bf[:, cols] = jnp.dot(wu_f32[slot, :, cols].astype(BF16), perm,
                                     preferred_element_type=F32).astype(BF16)
        wd_bf[...] = wd_f32[slot].astype(BF16)

    rows = xs_ref.shape[0] // ROW_TILES // HALF_BLOCKS

    def half(row0):
        x = jnp.concatenate([part for j in range(ROW_TILES)
                             for part in _unpack_words(xs_ref[_chunk(j, rows, row0), :])],
                            axis=1).astype(BF16)
        u = jnp.dot(x, wu_bf[...], preferred_element_type=F32) + bu_ref[...]
        acts = []
        for g in range(f2 // PAIR_GROUP):
            glu = jnp.minimum(u[:, g * PAIR_GROUP:g * PAIR_GROUP + LANES], SWIGLU_LIMIT)
            lin = jnp.clip(u[:, g * PAIR_GROUP + LANES:(g + 1) * PAIR_GROUP],
                           -SWIGLU_LIMIT, SWIGLU_LIMIT)
            acts.append((glu * jax.nn.sigmoid(SWIGLU_ALPHA * glu) * (lin + 1.0)).astype(BF16))
        act = jnp.concatenate(acts, axis=1)
        y = jnp.dot(act, wd_bf[...], preferred_element_type=F32) + bd_ref[...]
        for j, words in enumerate(_pack_rows(y)):
            ys_ref[_chunk(j, rows, row0), :] = words

    for hb in range(HALF_BLOCKS):
        live = used & (nvalid_ref[i] > hb * rows)

        @pl.when(live)
        def _():
            half(hb * rows)

        @pl.when(jnp.logical_not(live))
        def _():
            for j in range(ROW_TILES):
                ys_ref[_chunk(j, rows, hb * rows), :] = jnp.zeros((rows, LANES), PACKED)


def _experts(block_e, n_used, n_valid, next_e, slot, xs, wu, bu, wd, bd, *, block):
    n_slots = xs.shape[0] // ROW_TILES
    d, f2 = wu.shape[1:]
    n_blocks = n_slots // block
    rc_block = (block * ROW_TILES, LANES)
    xrow = pl.BlockSpec(rc_block,
                        lambda i, be, nb, *_: (jnp.maximum(jnp.minimum(i, nb[0] - 1), 0), 0))
    bias = lambda width: pl.BlockSpec((None, 1, width), lambda i, be, *_: (be[i], 0, 0))
    return pl.pallas_call(
        _expert_body,
        grid_spec=pltpu.PrefetchScalarGridSpec(
            num_scalar_prefetch=5,
            grid=(n_blocks,),
            in_specs=[xrow, bias(f2), bias(d), pl.BlockSpec(memory_space=pl.ANY),
                      pl.BlockSpec(memory_space=pl.ANY)],
            out_specs=pl.BlockSpec(rc_block, lambda i, *_: (i, 0)),
            scratch_shapes=[pltpu.VMEM((2, d, f2), F32), pltpu.VMEM((2, f2 // 2, d), F32),
                            pltpu.VMEM((d, f2), BF16), pltpu.VMEM((f2 // 2, d), BF16),
                            pltpu.SemaphoreType.DMA((2, 2))],
        ),
        out_shape=jax.ShapeDtypeStruct(xs.shape, PACKED),
        compiler_params=_cparams("arbitrary"),
        name="moe_expert_mlp",
    )(block_e, n_used, n_valid, next_e, slot, xs, bu, bd, wu, wd)


def _final_body(dest_ref, next_dest_ref, h_ref, g2_ref, gm_ref, km_ref, nw_ref, ys_ref, o_ref,
                buf_ref, sems):
    i = pl.program_id(0)
    tm = h_ref.shape[0] // 2

    def gather(dref, half):
        base = half * TOP_K * tm

        def issue(c, carry):
            for u in range(ROW_DMA_UNROLL):
                t = c * ROW_DMA_UNROLL + u
                for k in range(TOP_K):
                    _row_copy(ys_ref, dref[base + k * tm + t], buf_ref.at[half, k], t,
                              sems.at[half]).start(priority=k % 2)
            return carry

        lax.fori_loop(0, tm // ROW_DMA_UNROLL, issue, 0)

    def drain(half):
        def wait(c, carry):
            for u in range(ROW_DMA_UNROLL):
                for k in range(TOP_K):
                    _row_copy(ys_ref, 0, buf_ref.at[half, k], c * ROW_DMA_UNROLL + u,
                              sems.at[half]).wait()
            return carry

        lax.fori_loop(0, tm // ROW_DMA_UNROLL, wait, 0)

    def combine(half):
        rows = slice(half * tm, (half + 1) * tm)
        km = km_ref[rows, :]
        gm = gm_ref[rows, :]
        g2 = g2_ref[...] if g2_ref.shape[0] == 1 else g2_ref[rows, :]
        gates = [jnp.sum(jnp.where(km == float(k + 1), gm, 0.0), axis=-1, keepdims=True)
                 for k in range(TOP_K)]
        ssq = jnp.zeros((tm, 1), F32)
        for j in range(ROW_TILES):
            acc = [None, None]
            for k in range(TOP_K):
                for part, v in enumerate(_unpack_words(buf_ref[half, k, _chunk(j, tm), :])):
                    acc[part] = gates[k] * v if acc[part] is None else acc[part] + gates[k] * v
            for part in range(2):
                cols = slice((2 * j + part) * LANES, (2 * j + part + 1) * LANES)
                y = h_ref[rows, cols] + g2[:, cols] * acc[part]
                ssq = ssq + jnp.sum(y * y, axis=-1, keepdims=True)
                o_ref[rows, cols] = y
        inv = lax.rsqrt(ssq * (1.0 / ROW_WIDTH) + RMS_EPS)
        o_ref[rows, :] = o_ref[rows, :] * inv * nw_ref[...]

    @pl.when(i == 0)
    def _():
        gather(dest_ref, 0)

    gather(dest_ref, 1)
    drain(0)
    combine(0)

    @pl.when(i + 1 < pl.num_programs(0))
    def _():
        gather(next_dest_ref, 0)

    drain(1)
    combine(1)


def _final(dest, h, g2, g2_spec, gm, km, norm_w, ys, *, tm, step0, n_steps):
    d = h.shape[1]
    ne = gm.shape[1]
    row = lambda width: pl.BlockSpec((2 * tm, width), lambda i: (i + step0, 0))
    per_step = 2 * TOP_K * tm
    last = step0 + n_steps - 1
    return pl.pallas_call(
        _final_body,
        grid=(n_steps,),
        in_specs=[pl.BlockSpec((per_step,), lambda i: (i + step0,), memory_space=pltpu.SMEM),
                  pl.BlockSpec((per_step,), lambda i: (jnp.minimum(i + step0 + 1, last),),
                               memory_space=pltpu.SMEM),
                  row(d), g2_spec, row(ne), row(ne),
                  pl.BlockSpec(norm_w.shape, lambda i: (0, 0)),
                  pl.BlockSpec(memory_space=pl.ANY)],
        out_specs=pl.BlockSpec((2 * tm, d), lambda i: (i, 0)),
        out_shape=jax.ShapeDtypeStruct((n_steps * 2 * tm, d), F32),
        scratch_shapes=[pltpu.VMEM((2, TOP_K, tm * ROW_TILES, LANES), PACKED),
                        pltpu.SemaphoreType.DMA((2,))],
        compiler_params=_cparams("arbitrary"),
        name="moe_combine_final_norm",
    )(dest, dest, h, g2, gm, km, norm_w, ys)


def kernel(x_prompt, x_sample, cache_k_win, cache_v_win, state_conv, c_prompt, c_sample, w_ada, b_ada, norm_mix, norm_ffn, w_in, b_in, conv_w, sinks, w_branch_a, w_branch_b, w_out, b_out, w_router, b_router, w_up, b_up, w_down, b_down, norm_final):
    batch, seq, d = x_prompt.shape
    nseq, tq, _ = x_sample.shape
    depth = w_ada.shape[0]
    assert depth == 1 and tq == SUBLANES and seq % TOKEN_TILE == 0 and d == ROW_WIDTH
    assert seq % (2 * ROW_DMA_TILE) == 0 and (nseq * tq) % (2 * ROW_DMA_TILE) == 0
    wb = cache_k_win.shape[2]
    tp, ts = batch * seq, nseq * tq
    t_all = tp + ts
    tm_s = min(ts, TOKEN_TILE)
    assert ts % tm_s == 0 and tp % tm_s == 0 and t_all % ROW_DMA_TILE == 0
    nmod = w_ada.shape[2] // d

    n_c = batch + nseq
    c_rows = -(-n_c // SUBLANES) * SUBLANES
    c_all = jnp.concatenate([c_prompt, c_sample, jnp.zeros((c_rows - n_c, d), F32)], axis=0)
    mod = _ada(c_all, w_ada[0], b_ada[0])
    mod_p = mod[:batch].reshape(batch * nmod, 1, d)
    mod_s = jnp.repeat(mod[batch:n_c], tq, axis=0)
    tiles_per_seq = seq // TOKEN_TILE
    p_spec = lambda chunk: pl.BlockSpec((None, 1, d), lambda i: ((i // tiles_per_seq) * nmod + chunk, 0, 0))
    s_spec = lambda chunk: pl.BlockSpec((tm_s, d), lambda i: (i, chunk))

    q_lo = 3 * d
    q_hi = q_lo + N_HEADS * HEAD_DIM

    def q_cols(a):
        lead = a.shape[0]
        qa = a[:, q_lo:q_hi].reshape(lead, N_KV_HEADS, GROUP, HEAD_DIM).transpose(0, 2, 1, 3)
        qa = qa.reshape(lead, q_hi - q_lo) * (HEAD_DIM ** -0.5 * LOG2_E)
        return jnp.concatenate([a[:, :q_lo], qa, a[:, q_hi:]], axis=1)

    w_in_b = q_cols(w_in[0]).astype(BF16)
    b_in2 = q_cols(b_in[0].reshape(1, -1))
    nmix = norm_mix[0].reshape(1, d)
    cw = conv_w[0]

    xp = x_prompt.reshape(tp, d)
    ya_p, q_p, k_p, v_p, ga_p, gb_p, ut_p = _inproj(
        xp, mod_p, mod_p, (p_spec(0), p_spec(1)), nmix, w_in_b, b_in2, cw,
        tm=TOKEN_TILE, tiles_per_seq=tiles_per_seq)
    xs_ = x_sample.reshape(ts, d)
    st = state_conv[0]
    zpad = lambda a: jnp.pad(a, ((0, 0), (0, tq - a.shape[1]), (0, 0))).reshape(ts, d)
    hist1 = zpad(st[:, 1:2])
    hist2 = zpad(st)
    ya_s, q_s, k_s, v_s, ga_s, gb_s, u_s = _inproj(
        xs_, mod_s, mod_s, (s_spec(0), s_spec(1)), nmix, w_in_b, b_in2, cw,
        tm=tm_s, tiles_per_seq=0, hist=(hist1, hist2))

    sk = sinks[0]
    yb_p = _attn_prompt(q_p, k_p, v_p, sk, batch=batch, seq=seq)
    kbuf = cache_k_win[0].reshape(nseq, wb, KV_DIM)
    vbuf = cache_v_win[0].reshape(nseq, wb, KV_DIM)
    yb_s, nk_s, nv_s = _attn_sample(
        q_s.reshape(nseq, tq, d), k_s.reshape(nseq, tq, KV_DIM), v_s.reshape(nseq, tq, KV_DIM),
        kbuf, vbuf, sk, bs=SUBLANES)
    yb_s = yb_s.reshape(ts, d)

    wa = w_branch_a[0].astype(BF16)
    wbb = w_branch_b[0].reshape(N_KV_HEADS, GROUP, HEAD_DIM, d).transpose(1, 0, 2, 3)
    wbb = wbb.reshape(N_HEADS * HEAD_DIM, d).astype(BF16)
    wo = w_out[0].astype(BF16)
    bo = b_out[0].reshape(1, d)
    nffn = norm_ffn[0].reshape(1, d)
    wr0 = w_router[0].astype(BF16)
    wr1 = (w_router[0] - wr0.astype(F32)).astype(BF16)
    wr2 = (w_router[0] - wr0.astype(F32) - wr1.astype(F32)).astype(BF16)
    wr = jnp.concatenate([wr0, wr1, wr2], axis=1)
    br = b_router[0].reshape(1, -1)
    n_tiles_p = tp // tm_s
    pp_spec = lambda chunk: pl.BlockSpec(
        (None, 1, d),
        lambda i: ((jnp.minimum(i, n_tiles_p - 1) // (seq // tm_s)) * nmod + chunk, 0, 0))
    sp_spec = lambda chunk: pl.BlockSpec((tm_s, d), lambda i: (jnp.maximum(i - n_tiles_p, 0), chunk))
    h_all, hm_all, gm_all, km_all, counts = _post(
        (xp, ya_p, yb_p, ga_p, gb_p, mod_p, mod_p, mod_p), (pp_spec(2), pp_spec(3), pp_spec(4)),
        (xs_, ya_s, yb_s, ga_s, gb_s, mod_s, mod_s, mod_s), (sp_spec(2), sp_spec(3), sp_spec(4)),
        wa, wbb, wo, bo, nffn, wr, br, tm=tm_s)

    block = EXPERT_BLOCK
    dest = _route(km_all, counts, tm=tm_s, block=block)
    n_assign = t_all * TOP_K
    n_blocks = -(-(n_assign + N_EXPERTS * (block - 1)) // block)
    cnt = counts[0].astype(jnp.int32)
    pend = jnp.cumsum((cnt + block - 1) // block * block)
    n_used = (pend[-1] // block).astype(jnp.int32)
    blk_ids = jnp.minimum(jnp.arange(n_blocks, dtype=jnp.int32), n_used - 1)
    block_e = jnp.sum((pend[None, :] <= (blk_ids * block)[:, None]).astype(jnp.int32), axis=1)
    block_e = jnp.minimum(block_e, N_EXPERTS - 1)
    nonempty = cnt > 0
    ids = jnp.arange(N_EXPERTS, dtype=jnp.int32)
    later = jnp.where(nonempty[None, :] & (ids[None, :] > ids[:, None]), ids[None, :], N_EXPERTS)
    next_of = jnp.min(later, axis=1)
    next_of = jnp.where(next_of == N_EXPERTS, -1, next_of)
    slot_of = (jnp.cumsum(nonempty.astype(jnp.int32)) - 1) % 2
    row_end_of = pend - (cnt + block - 1) // block * block + cnt
    per_expert = jnp.stack([row_end_of, next_of, slot_of]).astype(jnp.int32)
    onehot = (block_e[None, :, None] == ids[None, None, :]).astype(jnp.int32)
    row_end, next_e, slot = jnp.sum(onehot * per_expert[:, None, :], axis=2)
    n_valid = jnp.clip(row_end - blk_ids * block, 0, block).astype(jnp.int32)

    dest = dest.reshape(TOP_K, t_all // ROW_DMA_TILE, ROW_DMA_TILE).transpose(1, 0, 2).reshape(-1)
    xs_rows = _dispatch(pend.astype(jnp.int32), dest, hm_all, tm=ROW_DMA_TILE,
                        n_slots=n_blocks * block, block=block)
    f2 = w_up.shape[3]
    bu = b_up[0].reshape(N_EXPERTS, f2 // PAIR_GROUP, LANES, 2).transpose(0, 1, 3, 2)
    bu = bu.reshape(N_EXPERTS, 1, f2)
    bd = b_down[0].reshape(N_EXPERTS, 1, d)
    ys = _experts(block_e, n_used.reshape(1), n_valid, next_e, slot, xs_rows, w_up[0], bu,
                  w_down[0], bd, block=block)

    nfin = norm_final.reshape(1, d)
    step_rows = 2 * ROW_DMA_TILE
    steps_p = tp // step_rows
    pf_spec = pl.BlockSpec((None, 1, d), lambda i: ((i // (seq // step_rows)) * nmod + 5, 0, 0))
    y_p = _final(dest, h_all, mod_p, pf_spec, gm_all, km_all, nfin, ys,
                 tm=ROW_DMA_TILE, step0=0, n_steps=steps_p)
    sf_spec = pl.BlockSpec((step_rows, d), lambda i: (i, 5))
    y_s = _final(dest, h_all, mod_s, sf_spec, gm_all, km_all, nfin, ys,
                 tm=ROW_DMA_TILE, step0=steps_p, n_steps=ts // step_rows)

    keep = min(WINDOW, seq)
    k4 = k_p.reshape(batch, seq, N_KV_HEADS, HEAD_DIM)[:, seq - keep:]
    v4 = v_p.reshape(batch, seq, N_KV_HEADS, HEAD_DIM)[:, seq - keep:]
    conv_p = ut_p.reshape(batch, tiles_per_seq, SUBLANES, d)[:, -1, SUBLANES - (CONV_WIDTH - 1):]
    conv_s = u_s.reshape(nseq, tq, d)[:, tq - (CONV_WIDTH - 1):]
    return (y_p.reshape(batch, seq, d), y_s.reshape(nseq, tq, d),
            k4[None], v4[None], conv_p[None],
            nk_s.reshape(1, nseq, wb, N_KV_HEADS, HEAD_DIM),
            nv_s.reshape(1, nseq, wb, N_KV_HEADS, HEAD_DIM), conv_s[None])
```

```python
import functools

import jax
import jax.numpy as jnp
from jax import lax
from jax.experimental import pallas as pl
from jax.experimental.pallas import tpu as pltpu

F32 = jnp.float32
BF16 = jnp.bfloat16
HIGHEST = lax.Precision.HIGHEST

N_HEADS = 16
N_KV_HEADS = 4
HEAD_DIM = 64
GROUP = N_HEADS // N_KV_HEADS
KV_DIM = N_KV_HEADS * HEAD_DIM
WINDOW = 128
N_EXPERTS = 32
TOP_K = 4
CONV_WIDTH = 3
SWIGLU_ALPHA = 1.702
SWIGLU_LIMIT = 7.0
RMS_EPS = 1e-5
ALIBI_SLOPES = tuple(2.0 ** (-8.0 * (h + 1) / N_HEADS) for h in range(N_HEADS))

SUBLANES = 8
LANES = 128
VMEM_LIMIT_BYTES = 56 * 1024 * 1024
TOKEN_TILE = 512
ROW_DMA_TILE = 256
ROW_DMA_UNROLL = 8
EXPERT_BLOCK = 512


def _cparams(*sem):
    return pltpu.CompilerParams(dimension_semantics=sem, vmem_limit_bytes=VMEM_LIMIT_BYTES)


def _rmsnorm(x, g):
    ms = jnp.mean(x * x, axis=-1, keepdims=True)
    return x * lax.rsqrt(ms + RMS_EPS) * g


def _ada_body(c_ref, w_ref, b_ref, o_ref):
    c = c_ref[...]
    a = c * jax.nn.sigmoid(c)
    o_ref[...] = jnp.dot(a, w_ref[...], preferred_element_type=F32, precision=HIGHEST) + b_ref[...]


def _ada(c_all, w_ada, b_ada):
    rows, d = c_all.shape
    n = w_ada.shape[1]
    return pl.pallas_call(
        _ada_body,
        grid=(n // d,),
        in_specs=[pl.BlockSpec((rows, d), lambda j: (0, 0)),
                  pl.BlockSpec((d, d), lambda j: (0, j)),
                  pl.BlockSpec((1, d), lambda j: (0, j))],
        out_specs=pl.BlockSpec((rows, d), lambda j: (0, j)),
        out_shape=jax.ShapeDtypeStruct((rows, n), F32),
        compiler_params=_cparams("arbitrary"),
        name="ada_modulation",
    )(c_all, w_ada, b_ada.reshape(1, n))


def _inproj_body(*refs, d, nc, tiles_per_seq):
    carry_mode = tiles_per_seq > 0
    if carry_mode:
        (x_ref, sh_ref, sc_ref, nw_ref, w_ref, b_ref, cw_ref,
         ya_ref, q_ref, k_ref, v_ref, ga_ref, gb_ref, ut_ref, carry_ref) = refs
    else:
        (x_ref, sh_ref, sc_ref, nw_ref, w_ref, b_ref, cw_ref, p1_ref, p2_ref,
         ya_ref, q_ref, k_ref, v_ref, ga_ref, gb_ref, ut_ref) = refs
    tm = x_ref.shape[0]
    xm = (_rmsnorm(x_ref[...], nw_ref[...]) * (1.0 + sc_ref[...]) + sh_ref[...]).astype(BF16)

    def proj(lo, width):
        return (jnp.dot(xm, w_ref[:, lo:lo + width], preferred_element_type=F32)
                + b_ref[:, lo:lo + width])

    if carry_mode:
        @pl.when(pl.program_id(0) % tiles_per_seq == 0)
        def _():
            carry_ref[...] = jnp.zeros_like(carry_ref)
    else:
        t_in_seq = lax.broadcasted_iota(jnp.int32, (tm, nc), 0) % SUBLANES

    for j in range(d // nc):
        c0 = j * nc
        u = proj(c0, nc) * proj(2 * d + c0, nc)
        if carry_mode:
            ue = jnp.concatenate([carry_ref[:, c0:c0 + nc], u], axis=0)
            s1 = pltpu.roll(ue, 1, axis=0)[SUBLANES:]
            s2 = pltpu.roll(ue, 2, axis=0)[SUBLANES:]
            carry_ref[:, c0:c0 + nc] = u[tm - SUBLANES:]
            ut_ref[:, c0:c0 + nc] = u[tm - SUBLANES:]
        else:
            s1 = jnp.where(t_in_seq >= 1, pltpu.roll(u, 1, axis=0), p1_ref[:, c0:c0 + nc])
            s2 = jnp.where(t_in_seq >= 2, pltpu.roll(u, 2, axis=0), p2_ref[:, c0:c0 + nc])
            ut_ref[:, c0:c0 + nc] = u
        cw = cw_ref[:, c0:c0 + nc]
        conv = s2 * cw[0:1] + s1 * cw[1:2] + u * cw[2:3]
        ya_ref[:, c0:c0 + nc] = (proj(d + c0, nc) * conv).astype(BF16)

    q_ref[...] = proj(3 * d, d).astype(BF16)
    k_ref[...] = proj(4 * d, KV_DIM)
    v_ref[...] = proj(4 * d + KV_DIM, KV_DIM)
    ga_ref[...] = jax.nn.sigmoid(proj(4 * d + 2 * KV_DIM, d)).astype(BF16)
    gb_ref[...] = jax.nn.sigmoid(proj(5 * d + 2 * KV_DIM, d)).astype(BF16)


def _inproj(x, shift, scale, shift_scale_specs, norm_w, w_in, b_in, conv_w, *, tm, tiles_per_seq,
            hist=None):
    t, d = x.shape
    n_in = w_in.shape[1]
    nc = 512
    carry_mode = tiles_per_seq > 0
    row = lambda width: pl.BlockSpec((tm, width), lambda i: (i, 0))
    full = lambda a: pl.BlockSpec(a.shape, lambda i: (0,) * a.ndim)
    in_specs = [row(d), shift_scale_specs[0], shift_scale_specs[1], full(norm_w), full(w_in),
                full(b_in), full(conv_w)]
    args = [x, shift, scale, norm_w, w_in, b_in, conv_w]
    if not carry_mode:
        in_specs += [row(d), row(d)]
        args += list(hist)
    ut_rows = SUBLANES if carry_mode else tm
    out_specs = [row(d), row(d), row(KV_DIM), row(KV_DIM), row(d), row(d),
                 pl.BlockSpec((ut_rows, d), lambda i: (i, 0))]
    out_shape = [jax.ShapeDtypeStruct((t, d), BF16), jax.ShapeDtypeStruct((t, d), BF16),
                 jax.ShapeDtypeStruct((t, KV_DIM), F32), jax.ShapeDtypeStruct((t, KV_DIM), F32),
                 jax.ShapeDtypeStruct((t, d), BF16), jax.ShapeDtypeStruct((t, d), BF16),
                 jax.ShapeDtypeStruct((t // tm * ut_rows, d), F32)]
    return pl.pallas_call(
        functools.partial(_inproj_body, d=d, nc=nc, tiles_per_seq=tiles_per_seq),
        grid=(t // tm,),
        in_specs=in_specs,
        out_specs=out_specs,
        out_shape=out_shape,
        scratch_shapes=[pltpu.VMEM((SUBLANES, d), F32)] if carry_mode else [],
        compiler_params=_cparams("arbitrary"),
        name="inproj_conv_long" if carry_mode else "inproj_conv_short",
    )(*args)


LOG2_E = 1.4426950408889634


def _softmax_rows(s):
    m = jnp.max(s, axis=-1, keepdims=True)
    p = jnp.exp2(s - m)
    den = jnp.sum(p, axis=-1, keepdims=True)
    return (p * (1.0 / den)).astype(BF16)


def _own_block(rows, rows_per_kv):
    r = lax.broadcasted_iota(jnp.int32, (rows, KV_DIM), 0)
    c = lax.broadcasted_iota(jnp.int32, (rows, KV_DIM), 1)
    return (c // HEAD_DIM) == (r // rows_per_kv) % N_KV_HEADS


def _attn_prompt_body(q_ref, kc_ref, kp_ref, vc_ref, vp_ref, bias_ref, o_ref):
    blk = kp_ref.shape[0]
    first_variant = jnp.minimum(pl.program_id(1), 1)
    _attn_block(q_ref, o_ref, slice(0, blk), bias_ref.at[first_variant],
                jnp.concatenate([kp_ref[...], kc_ref[0:blk, :]], axis=0),
                jnp.concatenate([vp_ref[...], vc_ref[0:blk, :]], axis=0))
    _attn_block(q_ref, o_ref, slice(blk, 2 * blk), bias_ref.at[1], kc_ref[...], vc_ref[...])


def _attn_block(q_ref, o_ref, qrows, bias_ref, kk, vv):
    blk = qrows.stop - qrows.start
    col_kv = lax.broadcasted_iota(jnp.int32, kk.shape, 1) // HEAD_DIM
    not_sink = lax.broadcasted_iota(jnp.int32, kk.shape, 0) > 0
    kk_kv = [jnp.where((col_kv == kv) & not_sink, kk, 0.0).astype(BF16)
             for kv in range(N_KV_HEADS)]
    vv_kv = [jnp.where((col_kv == kv) & not_sink, vv, 0.0).astype(BF16)
             for kv in range(N_KV_HEADS)]

    def scores(g):
        a = q_ref[qrows, g * KV_DIM:(g + 1) * KV_DIM]
        return [lax.dot_general(a, kk_kv[kv], (((1,), (1,)), ((), ())),
                                preferred_element_type=F32)
                + bias_ref[g, kv * blk:(kv + 1) * blk, :] for kv in range(N_KV_HEADS)]

    def weighted_values(g, probs):
        out = jnp.dot(probs[0], vv_kv[0], preferred_element_type=F32)
        for kv in range(1, N_KV_HEADS):
            out = out + jnp.dot(probs[kv], vv_kv[kv], preferred_element_type=F32)
        o_ref[qrows, g * KV_DIM:(g + 1) * KV_DIM] = out.astype(BF16)

    s_next = scores(0)
    p_prev = None
    for g in range(GROUP):
        s_cur = s_next
        if g + 1 < GROUP:
            s_next = scores(g + 1)
        p_cur = [_softmax_rows(s) for s in s_cur]
        if p_prev is not None:
            weighted_values(g - 1, p_prev)
        p_prev = p_cur
    weighted_values(GROUP - 1, p_prev)


def _attn_table(sinks, tq_rows, n_keys, key_offset, first_block):
    assert key_offset >= WINDOW
    kv = jnp.arange(N_KV_HEADS)[None, :, None, None]
    g = jnp.arange(GROUP)[:, None, None, None]
    qi = jnp.arange(tq_rows)[None, None, :, None]
    kj = jnp.arange(n_keys)[None, None, None, :]
    slope = jnp.exp2(-8.0 * (kv * GROUP + g + 1).astype(F32) / N_HEADS)
    dist = qi + key_offset - kj
    valid = (dist >= 0) & (dist < WINDOW)
    if first_block:
        valid = valid & (kj >= key_offset)
    bias = jnp.where(valid, -slope * dist.astype(F32), -jnp.inf)
    sink = sinks.reshape(N_KV_HEADS, GROUP).T[:, :, None, None]
    bias = jnp.where(kj == 0, sink, bias) * LOG2_E
    return bias.reshape(GROUP, N_KV_HEADS * tq_rows, n_keys)


def _attn_prompt(q, k, v, sinks, *, batch, seq):
    t, d = q.shape
    blk = WINDOW
    nb = seq // blk
    rows = N_KV_HEADS * blk
    bias = jnp.stack([_attn_table(sinks, blk, 2 * blk, blk, True),
                      _attn_table(sinks, blk, 2 * blk, blk, False)])
    assert nb % 2 == 0
    steps = nb // 2
    cur = lambda width: pl.BlockSpec((2 * blk, width), lambda b, j: (b * steps + j, 0))
    prev = lambda width: pl.BlockSpec((blk, width),
                                      lambda b, j: (b * nb + jnp.maximum(2 * j - 1, 0), 0))
    return pl.pallas_call(
        _attn_prompt_body,
        grid=(batch, steps),
        in_specs=[cur(d), cur(KV_DIM), prev(KV_DIM), cur(KV_DIM), prev(KV_DIM),
                  pl.BlockSpec((2, GROUP, rows, 2 * blk), lambda b, j: (0, 0, 0, 0))],
        out_specs=cur(d),
        out_shape=jax.ShapeDtypeStruct((t, d), BF16),
        compiler_params=_cparams("arbitrary", "arbitrary"),
        name="attn_window_long",
    )(q, k, k, v, v, bias)


def _attn_sample_body(q_ref, kn_ref, vn_ref, kb_ref, vb_ref, bias_ref, o_ref, nk_ref, nv_ref):
    bs, tq, _ = q_ref.shape
    rows_g = N_KV_HEADS * tq
    rows = GROUP * rows_g
    own = _own_block(rows, tq)
    bias = bias_ref[...]
    not_sink = lax.broadcasted_iota(jnp.int32, (kb_ref.shape[1] + tq, KV_DIM), 0) > 0
    for b in range(bs):
        keys = jnp.concatenate([kb_ref[b], kn_ref[b]], axis=0)
        vals = jnp.concatenate([vb_ref[b], vn_ref[b]], axis=0)
        nk_ref[b] = keys[tq:]
        nv_ref[b] = vals[tq:]
        keys = jnp.where(not_sink, keys, 0.0).astype(BF16)
        vals = jnp.where(not_sink, vals, 0.0).astype(BF16)
        qb = q_ref[b].astype(F32)
        qbd = jnp.concatenate([qb[:, g * KV_DIM:(g + 1) * KV_DIM]
                               for g in range(GROUP) for _ in range(N_KV_HEADS)], axis=0)
        qbd = jnp.where(own, qbd, 0.0).astype(BF16)
        s = lax.dot_general(qbd, keys, (((1,), (1,)), ((), ())), preferred_element_type=F32)
        o = jnp.dot(_softmax_rows(s + bias), vals, preferred_element_type=F32)
        o = jnp.where(own, o, 0.0)
        outs = []
        for g in range(GROUP):
            out = o[g * rows_g:g * rows_g + tq]
            for kv in range(1, N_KV_HEADS):
                out = out + o[g * rows_g + kv * tq:g * rows_g + (kv + 1) * tq]
            outs.append(out)
        o_ref[b] = jnp.concatenate(outs, axis=1).astype(BF16)


def _attn_sample(q, k_new, v_new, k_buf, v_buf, sinks, *, bs):
    nseq, tq, d = q.shape
    wb = k_buf.shape[1]
    rows = N_HEADS * tq
    bias = _attn_table(sinks, tq, wb + tq, wb, False).reshape(rows, wb + tq)
    blk3 = lambda a, b_: pl.BlockSpec((bs, a, b_), lambda i: (i, 0, 0))
    return pl.pallas_call(
        _attn_sample_body,
        grid=(nseq // bs,),
        in_specs=[blk3(tq, d), blk3(tq, KV_DIM), blk3(tq, KV_DIM), blk3(wb, KV_DIM), blk3(wb, KV_DIM),
                  pl.BlockSpec((rows, wb + tq), lambda i: (0, 0))],
        out_specs=[blk3(tq, d), blk3(wb, KV_DIM), blk3(wb, KV_DIM)],
        out_shape=[jax.ShapeDtypeStruct((nseq, tq, d), BF16),
                   jax.ShapeDtypeStruct((nseq, wb, KV_DIM), F32),
                   jax.ShapeDtypeStruct((nseq, wb, KV_DIM), F32)],
        compiler_params=_cparams("arbitrary"),
        name="attn_window_short",
    )(q, k_new, v_new, k_buf, v_buf, bias)


N_POST_ACTS = 8


def _post_body(*refs, n_first):
    first, second = refs[:N_POST_ACTS], refs[N_POST_ACTS:2 * N_POST_ACTS]
    rest = refs[2 * N_POST_ACTS:]
    i = pl.program_id(0)

    @pl.when(i == 0)
    def _():
        rest[-1][...] = jnp.zeros_like(rest[-1])

    @pl.when(i < n_first)
    def _():
        _post_tile(*first, *rest)

    @pl.when(i >= n_first)
    def _():
        _post_tile(*second, *rest)


def _post_tile(x_ref, ya_ref, yb_ref, ga_ref, gb_ref, g1_ref, sh2_ref, sc2_ref, wa_ref, wb_ref,
               wo_ref, bo_ref, nf_ref, wr_ref, br_ref, h_ref, hm_ref, gm_ref, km_ref, cnt_ref):
    a = jnp.dot(ya_ref[...], wa_ref[...], preferred_element_type=F32)
    b = jnp.dot(yb_ref[...], wb_ref[...], preferred_element_type=F32)
    merged = ga_ref[...].astype(F32) * a + gb_ref[...].astype(F32) * b
    o = jnp.dot(merged.astype(BF16), wo_ref[...], preferred_element_type=F32) + bo_ref[...]
    h = x_ref[...] + g1_ref[...] * o
    h_ref[...] = h
    hm = _rmsnorm(h, nf_ref[...]) * (1.0 + sc2_ref[...]) + sh2_ref[...]
    for j, words in enumerate(_pack_rows(hm)):
        hm_ref[_chunk(j, hm.shape[0]), :] = words
    ne = br_ref.shape[1]
    hm_hi = hm.astype(BF16)
    hm_lo = (hm - hm_hi.astype(F32)).astype(BF16)
    wr = wr_ref[...]
    p_hi = jnp.dot(hm_hi, wr, preferred_element_type=F32)
    p_lo = jnp.dot(hm_lo, wr[:, :2 * ne], preferred_element_type=F32)
    logits = (p_hi[:, :ne] + (p_hi[:, ne:2 * ne] + p_lo[:, :ne])
              + (p_hi[:, 2 * ne:] + p_lo[:, ne:]) + br_ref[...])
    tm = logits.shape[0]
    lane = lax.broadcasted_iota(jnp.int32, (tm, ne), 1)
    vals = logits
    km = jnp.zeros((tm, ne), F32)
    top = []
    for k in range(TOP_K):
        m = jnp.max(vals, axis=-1, keepdims=True)
        first = jnp.min(jnp.where(vals == m, lane, ne), axis=-1, keepdims=True)
        sel = lane == first
        km = jnp.where(sel, float(k + 1), km)
        vals = jnp.where(sel, -jnp.inf, vals)
        top.append(m)
    e = [jnp.exp(t - top[0]) for t in top]
    den = e[0] + e[1] + e[2] + e[3]
    gm = jnp.zeros((tm, ne), F32)
    for k in range(TOP_K):
        gm = jnp.where(km == float(k + 1), e[k] / den, gm)
    gm_ref[...] = gm
    km_ref[...] = km
    cnt_ref[...] += jnp.sum((km > 0.0).astype(F32), axis=0, keepdims=True)


def _post(first, first_mod_specs, second, second_mod_specs, wa, wb, wo, bo, nf, wr, br, *, tm):
    t1, d = first[0].shape
    t2 = second[0].shape[0]
    n1, n2 = t1 // tm, t2 // tm
    ne = br.shape[1]
    row1 = pl.BlockSpec((tm, d), lambda i: (jnp.minimum(i, n1 - 1), 0))
    row2 = pl.BlockSpec((tm, d), lambda i: (jnp.maximum(i - n1, 0), 0))
    orow = lambda width: pl.BlockSpec((tm, width), lambda i: (i, 0))
    full = lambda a: pl.BlockSpec(a.shape, lambda i: (0,) * a.ndim)
    weights = [wa, wb, wo, bo, nf, wr, br]
    in_specs = ([row1] * 5 + list(first_mod_specs) + [row2] * 5 + list(second_mod_specs)
                + [full(w) for w in weights])
    return pl.pallas_call(
        functools.partial(_post_body, n_first=n1),
        grid=(n1 + n2,),
        in_specs=in_specs,
        out_specs=[orow(d), pl.BlockSpec((tm * ROW_TILES, LANES), lambda i: (i, 0)),
                   orow(ne), orow(ne), pl.BlockSpec((1, ne), lambda i: (0, 0))],
        out_shape=[jax.ShapeDtypeStruct((t1 + t2, d), F32),
                   jax.ShapeDtypeStruct(((t1 + t2) * ROW_TILES, LANES), PACKED),
                   jax.ShapeDtypeStruct((t1 + t2, ne), F32), jax.ShapeDtypeStruct((t1 + t2, ne), F32),
                   jax.ShapeDtypeStruct((1, ne), F32)],
        compiler_params=_cparams("arbitrary"),
        name="merge_outproj_router",
    )(*first, *second, *weights)


def _route_body(km_ref, cnt_ref, dest_ref, run_ref, start_ref, *, block):
    i = pl.program_id(0)
    km = km_ref[...]
    tm, ne = km.shape
    chosen = (km > 0.0).astype(F32)

    @pl.when(i == 0)
    def _():
        padded = jnp.ceil(cnt_ref[...] / block) * block
        r = lax.broadcasted_iota(jnp.int32, (ne, ne), 0)
        c = lax.broadcasted_iota(jnp.int32, (ne, ne), 1)
        before = (r < c).astype(F32)
        start_ref[...] = jnp.dot(padded, before, preferred_element_type=F32, precision=HIGHEST)
        run_ref[...] = jnp.zeros_like(run_ref)

    r = lax.broadcasted_iota(jnp.int32, (tm, tm), 0)
    c = lax.broadcasted_iota(jnp.int32, (tm, tm), 1)
    earlier = (c < r).astype(BF16)
    rank = jnp.dot(earlier, chosen.astype(BF16), preferred_element_type=F32)
    slot = start_ref[...] + run_ref[...] + rank
    ones = jnp.ones((SUBLANES, ne), F32)
    for k in range(TOP_K):
        sk = jnp.where(km == float(k + 1), slot, 0.0)
        rowk = lax.dot_general(ones, sk, (((1,), (1,)), ((), ())),
                               preferred_element_type=F32, precision=HIGHEST)
        dest_ref[k:k + 1, :] = rowk[0:1].astype(jnp.int32)
    run_ref[...] += jnp.sum(chosen, axis=0, keepdims=True)


def _route(km, counts, *, tm, block):
    t, ne = km.shape
    return pl.pallas_call(
        functools.partial(_route_body, block=block),
        grid=(t // tm,),
        in_specs=[pl.BlockSpec((tm, ne), lambda i: (i, 0)),
                  pl.BlockSpec((1, ne), lambda i: (0, 0))],
        out_specs=pl.BlockSpec((TOP_K, tm), lambda i: (0, i)),
        out_shape=jax.ShapeDtypeStruct((TOP_K, t), jnp.int32),
        scratch_shapes=[pltpu.VMEM((1, ne), F32), pltpu.VMEM((1, ne), F32)],
        compiler_params=_cparams("arbitrary"),
        name="route_slots",
    )(km, counts)


ROW_TILES = 4
ROW_WIDTH = ROW_TILES * 2 * LANES
PACKED = jnp.uint32
HIGH_HALF = 0xFFFF0000


def _row_copy(src, src_row, dst, dst_row, sem):
    src_at = pl.ds(pl.multiple_of(src_row * ROW_TILES, ROW_TILES), ROW_TILES)
    dst_at = pl.ds(pl.multiple_of(dst_row * ROW_TILES, ROW_TILES), ROW_TILES)
    return pltpu.make_async_copy(src.at[src_at, :], dst.at[dst_at, :], sem)


def _chunk(j, rows, row0=0):
    return pl.ds(row0 * ROW_TILES + j, rows, stride=ROW_TILES)


def _pack_rows(x):
    def bf16_bits(v):
        return lax.bitcast_convert_type(v.astype(BF16).astype(F32), PACKED)

    words = []
    for j in range(ROW_TILES):
        lo = bf16_bits(x[:, 2 * j * LANES:(2 * j + 1) * LANES]) >> 16
        hi = bf16_bits(x[:, (2 * j + 1) * LANES:(2 * j + 2) * LANES]) & jnp.uint32(HIGH_HALF)
        words.append(lo | hi)
    return words


def _unpack_words(u):
    lo = lax.bitcast_convert_type(u << 16, F32)
    hi = lax.bitcast_convert_type(u & jnp.uint32(HIGH_HALF), F32)
    return lo, hi


def _dispatch_body(pend_ref, dest_ref, hm_ref, xs_ref, zero_ref, sem, zsem):
    tm = hm_ref.shape[0] // ROW_TILES
    block = zero_ref.shape[0] // ROW_TILES

    @pl.when(pl.program_id(0) == 0)
    def _():
        zero_ref[...] = jnp.zeros_like(zero_ref)
        n_exp = pend_ref.shape[0]
        n_slots = xs_ref.shape[0] // ROW_TILES
        fills = []
        for e in range(n_exp):
            end = pend_ref[e]
            begin = pend_ref[e - 1] if e > 0 else 0
            fills.append((end > begin, end - block))
        for j in range(n_exp):
            begin = pend_ref[n_exp - 1] + j * block
            fills.append((begin < n_slots, begin))
        for phase in ("start", "wait"):
            for cond, begin in fills:
                @pl.when(cond)
                def _():
                    rows = block * ROW_TILES
                    fill = pltpu.make_async_copy(
                        zero_ref,
                        xs_ref.at[pl.ds(pl.multiple_of(begin * ROW_TILES, rows), rows), :], zsem)
                    fill.start() if phase == "start" else fill.wait()

    def issue(c, carry):
        for u in range(ROW_DMA_UNROLL):
            t = c * ROW_DMA_UNROLL + u
            for k in range(TOP_K):
                _row_copy(hm_ref, t, xs_ref, dest_ref[k * tm + t], sem).start(priority=k % 2)
        return carry

    lax.fori_loop(0, tm // ROW_DMA_UNROLL, issue, 0)

    def drain(c, carry):
        for u in range(ROW_DMA_UNROLL):
            for k in range(TOP_K):
                _row_copy(hm_ref, c * ROW_DMA_UNROLL + u, xs_ref, 0, sem).wait()
        return carry

    lax.fori_loop(0, tm // ROW_DMA_UNROLL, drain, 0)


def _dispatch(pend, dest, hm, *, tm, n_slots, block):
    t = hm.shape[0] // ROW_TILES
    return pl.pallas_call(
        _dispatch_body,
        grid_spec=pltpu.PrefetchScalarGridSpec(
            num_scalar_prefetch=1,
            grid=(t // tm,),
            in_specs=[pl.BlockSpec((TOP_K * tm,), lambda i, pe: (i,), memory_space=pltpu.SMEM),
                      pl.BlockSpec((tm * ROW_TILES, LANES), lambda i, pe: (i, 0))],
            out_specs=pl.BlockSpec(memory_space=pl.ANY),
            scratch_shapes=[pltpu.VMEM((block * ROW_TILES, LANES), PACKED), pltpu.SemaphoreType.DMA,
                            pltpu.SemaphoreType.DMA],
        ),
        out_shape=jax.ShapeDtypeStruct((n_slots * ROW_TILES, LANES), PACKED),
        compiler_params=_cparams("arbitrary"),
        name="moe_dispatch_rows",
    )(pend, dest, hm)


PAIR_GROUP = 2 * LANES
HALF_BLOCKS = 2


def _expert_body(be_ref, nb_ref, nvalid_ref, next_ref, slot_ref, xs_ref, bu_ref, bd_ref, wu_hbm,
                 wd_hbm, ys_ref, wu_f32, wd_f32, wu_bf, wd_bf, sems):
    i = pl.program_id(0)
    f2 = wu_f32.shape[2]
    used = i < nb_ref[0]
    e = be_ref[i]
    slot = slot_ref[i]
    new_expert = (i == 0) | (e != be_ref[jnp.maximum(i - 1, 0)])

    def fetch(expert, s):
        return (pltpu.make_async_copy(wu_hbm.at[expert], wu_f32.at[s], sems.at[s, 0]),
                pltpu.make_async_copy(wd_hbm.at[expert], wd_f32.at[s], sems.at[s, 1]))

    @pl.when(used & (i == 0))
    def _():
        for cp in fetch(e, slot):
            cp.start()

    @pl.when(used & new_expert)
    def _():
        for cp in fetch(e, slot):
            cp.wait()

        @pl.when(next_ref[i] >= 0)
        def _():
            for cp in fetch(next_ref[i], 1 - slot):
                cp.start()

        r = lax.broadcasted_iota(jnp.int32, (PAIR_GROUP, PAIR_GROUP), 0)
        c = lax.broadcasted_iota(jnp.int32, (PAIR_GROUP, PAIR_GROUP), 1)
        src = jnp.where(c < LANES, 2 * c, 2 * (c - LANES) + 1)
        perm = (r == src).astype(BF16)
        for g in range(f2 // PAIR_GROUP):
            cols = slice(g * PAIR_GROUP, (g + 1) * PAIR_GROUP)
            wu_bf[:, cols] = jnp.dot(wu_f32[slot, :, cols].astype(BF16), perm,
                                     preferred_element_type=F32).astype(BF16)
        wd_bf[...] = wd_f32[slot].astype(BF16)

    rows = xs_ref.shape[0] // ROW_TILES // HALF_BLOCKS

    def half(row0):
        x = jnp.concatenate([part for j in range(ROW_TILES)
                             for part in _unpack_words(xs_ref[_chunk(j, rows, row0), :])],
                            axis=1).astype(BF16)
        u = jnp.dot(x, wu_bf[...], preferred_element_type=F32) + bu_ref[...]
        acts = []
        for g in range(f2 // PAIR_GROUP):
            glu = jnp.minimum(u[:, g * PAIR_GROUP:g * PAIR_GROUP + LANES], SWIGLU_LIMIT)
            lin = jnp.clip(u[:, g * PAIR_GROUP + LANES:(g + 1) * PAIR_GROUP],
                           -SWIGLU_LIMIT, SWIGLU_LIMIT)
            acts.append((glu * jax.nn.sigmoid(SWIGLU_ALPHA * glu) * (lin + 1.0)).astype(BF16))
        act = jnp.concatenate(acts, axis=1)
        y = jnp.dot(act, wd_bf[...], preferred_element_type=F32) + bd_ref[...]
        for j, words in enumerate(_pack_rows(y)):
            ys_ref[_chunk(j, rows, row0), :] = words

    for hb in range(HALF_BLOCKS):
        live = used & (nvalid_ref[i] > hb * rows)

        @pl.when(live)
        def _():
            half(hb * rows)

        @pl.when(jnp.logical_not(live))
        def _():
            for j in range(ROW_TILES):
                ys_ref[_chunk(j, rows, hb * rows), :] = jnp.zeros((rows, LANES), PACKED)


def _experts(block_e, n_used, n_valid, next_e, slot, xs, wu, bu, wd, bd, *, block):
    n_slots = xs.shape[0] // ROW_TILES
    d, f2 = wu.shape[1:]
    n_blocks = n_slots // block
    rc_block = (block * ROW_TILES, LANES)
    xrow = pl.BlockSpec(rc_block,
                        lambda i, be, nb, *_: (jnp.maximum(jnp.minimum(i, nb[0] - 1), 0), 0))
    bias = lambda width: pl.BlockSpec((None, 1, width), lambda i, be, *_: (be[i], 0, 0))
    return pl.pallas_call(
        _expert_body,
        grid_spec=pltpu.PrefetchScalarGridSpec(
            num_scalar_prefetch=5,
            grid=(n_blocks,),
            in_specs=[xrow, bias(f2), bias(d), pl.BlockSpec(memory_space=pl.ANY),
                      pl.BlockSpec(memory_space=pl.ANY)],
            out_specs=pl.BlockSpec(rc_block, lambda i, *_: (i, 0)),
            scratch_shapes=[pltpu.VMEM((2, d, f2), F32), pltpu.VMEM((2, f2 // 2, d), F32),
                            pltpu.VMEM((d, f2), BF16), pltpu.VMEM((f2 // 2, d), BF16),
                            pltpu.SemaphoreType.DMA((2, 2))],
        ),
        out_shape=jax.ShapeDtypeStruct(xs.shape, PACKED),
        compiler_params=_cparams("arbitrary"),
        name="moe_expert_mlp",
    )(block_e, n_used, n_valid, next_e, slot, xs, bu, bd, wu, wd)


def _final_body(dest_ref, next_dest_ref, h_ref, g2_ref, gm_ref, km_ref, nw_ref, ys_ref, o_ref,
                buf_ref, sems):
    i = pl.program_id(0)
    tm = h_ref.shape[0] // 2

    def gather(dref, half):
        base = half * TOP_K * tm

        def issue(c, carry):
            for u in range(ROW_DMA_UNROLL):
                t = c * ROW_DMA_UNROLL + u
                for k in range(TOP_K):
                    _row_copy(ys_ref, dref[base + k * tm + t], buf_ref.at[half, k], t,
                              sems.at[half]).start(priority=k % 2)
            return carry

        lax.fori_loop(0, tm // ROW_DMA_UNROLL, issue, 0)

    def drain(half):
        def wait(c, carry):
            for u in range(ROW_DMA_UNROLL):
                for k in range(TOP_K):
                    _row_copy(ys_ref, 0, buf_ref.at[half, k], c * ROW_DMA_UNROLL + u,
                              sems.at[half]).wait()
            return carry

        lax.fori_loop(0, tm // ROW_DMA_UNROLL, wait, 0)

    def combine(half):
        rows = slice(half * tm, (half + 1) * tm)
        km = km_ref[rows, :]
        gm = gm_ref[rows, :]
        g2 = g2_ref[...] if g2_ref.shape[0] == 1 else g2_ref[rows, :]
        gates = [jnp.sum(jnp.where(km == float(k + 1), gm, 0.0), axis=-1, keepdims=True)
                 for k in range(TOP_K)]
        ssq = jnp.zeros((tm, 1), F32)
        for j in range(ROW_TILES):
            acc = [None, None]
            for k in range(TOP_K):
                for part, v in enumerate(_unpack_words(buf_ref[half, k, _chunk(j, tm), :])):
                    acc[part] = gates[k] * v if acc[part] is None else acc[part] + gates[k] * v
            for part in range(2):
                cols = slice((2 * j + part) * LANES, (2 * j + part + 1) * LANES)
                y = h_ref[rows, cols] + g2[:, cols] * acc[part]
                ssq = ssq + jnp.sum(y * y, axis=-1, keepdims=True)
                o_ref[rows, cols] = y
        inv = lax.rsqrt(ssq * (1.0 / ROW_WIDTH) + RMS_EPS)
        o_ref[rows, :] = o_ref[rows, :] * inv * nw_ref[...]

    @pl.when(i == 0)
    def _():
        gather(dest_ref, 0)

    gather(dest_ref, 1)
    drain(0)
    combine(0)

    @pl.when(i + 1 < pl.num_programs(0))
    def _():
        gather(next_dest_ref, 0)

    drain(1)
    combine(1)


def _final(dest, h, g2, g2_spec, gm, km, norm_w, ys, *, tm, step0, n_steps):
    d = h.shape[1]
    ne = gm.shape[1]
    row = lambda width: pl.BlockSpec((2 * tm, width), lambda i: (i + step0, 0))
    per_step = 2 * TOP_K * tm
    last = step0 + n_steps - 1
    return pl.pallas_call(
        _final_body,
        grid=(n_steps,),
        in_specs=[pl.BlockSpec((per_step,), lambda i: (i + step0,), memory_space=pltpu.SMEM),
                  pl.BlockSpec((per_step,), lambda i: (jnp.minimum(i + step0 + 1, last),),
                               memory_space=pltpu.SMEM),
                  row(d), g2_spec, row(ne), row(ne),
                  pl.BlockSpec(norm_w.shape, lambda i: (0, 0)),
                  pl.BlockSpec(memory_space=pl.ANY)],
        out_specs=pl.BlockSpec((2 * tm, d), lambda i: (i, 0)),
        out_shape=jax.ShapeDtypeStruct((n_steps * 2 * tm, d), F32),
        scratch_shapes=[pltpu.VMEM((2, TOP_K, tm * ROW_TILES, LANES), PACKED),
                        pltpu.SemaphoreType.DMA((2,))],
        compiler_params=_cparams("arbitrary"),
        name="moe_combine_final_norm",
    )(dest, dest, h, g2, gm, km, norm_w, ys)


def kernel(x_prompt, x_sample, cache_k_win, cache_v_win, state_conv, c_prompt, c_sample, w_ada, b_ada, norm_mix, norm_ffn, w_in, b_in, conv_w, sinks, w_branch_a, w_branch_b, w_out, b_out, w_router, b_router, w_up, b_up, w_down, b_down, norm_final):
    batch, seq, d = x_prompt.shape
    nseq, tq, _ = x_sample.shape
    depth = w_ada.shape[0]
    assert depth == 1 and tq == SUBLANES and seq % TOKEN_TILE == 0 and d == ROW_WIDTH
    assert seq % (2 * ROW_DMA_TILE) == 0 and (nseq * tq) % (2 * ROW_DMA_TILE) == 0
    wb = cache_k_win.shape[2]
    tp, ts = batch * seq, nseq * tq
    t_all = tp + ts
    tm_s = min(ts, TOKEN_TILE)
    assert ts % tm_s == 0 and tp % tm_s == 0 and t_all % ROW_DMA_TILE == 0
    nmod = w_ada.shape[2] // d

    n_c = batch + nseq
    c_rows = -(-n_c // SUBLANES) * SUBLANES
    c_all = jnp.concatenate([c_prompt, c_sample, jnp.zeros((c_rows - n_c, d), F32)], axis=0)
    mod = _ada(c_all, w_ada[0], b_ada[0])
    mod_p = mod[:batch].reshape(batch * nmod, 1, d)
    mod_s = jnp.repeat(mod[batch:n_c], tq, axis=0)
    tiles_per_seq = seq // TOKEN_TILE
    p_spec = lambda chunk: pl.BlockSpec((None, 1, d), lambda i: ((i // tiles_per_seq) * nmod + chunk, 0, 0))
    s_spec = lambda chunk: pl.BlockSpec((tm_s, d), lambda i: (i, chunk))

    q_lo = 3 * d
    q_hi = q_lo + N_HEADS * HEAD_DIM

    def q_cols(a):
        lead = a.shape[0]
        qa = a[:, q_lo:q_hi].reshape(lead, N_KV_HEADS, GROUP, HEAD_DIM).transpose(0, 2, 1, 3)
        qa = qa.reshape(lead, q_hi - q_lo) * (HEAD_DIM ** -0.5 * LOG2_E)
        return jnp.concatenate([a[:, :q_lo], qa, a[:, q_hi:]], axis=1)

    w_in_b = q_cols(w_in[0]).astype(BF16)
    b_in2 = q_cols(b_in[0].reshape(1, -1))
    nmix = norm_mix[0].reshape(1, d)
    cw = conv_w[0]

    xp = x_prompt.reshape(tp, d)
    ya_p, q_p, k_p, v_p, ga_p, gb_p, ut_p = _inproj(
        xp, mod_p, mod_p, (p_spec(0), p_spec(1)), nmix, w_in_b, b_in2, cw,
        tm=TOKEN_TILE, tiles_per_seq=tiles_per_seq)
    xs_ = x_sample.reshape(ts, d)
    st = state_conv[0]
    zpad = lambda a: jnp.pad(a, ((0, 0), (0, tq - a.shape[1]), (0, 0))).reshape(ts, d)
    hist1 = zpad(st[:, 1:2])
    hist2 = zpad(st)
    ya_s, q_s, k_s, v_s, ga_s, gb_s, u_s = _inproj(
        xs_, mod_s, mod_s, (s_spec(0), s_spec(1)), nmix, w_in_b, b_in2, cw,
        tm=tm_s, tiles_per_seq=0, hist=(hist1, hist2))

    sk = sinks[0]
    yb_p = _attn_prompt(q_p, k_p, v_p, sk, batch=batch, seq=seq)
    kbuf = cache_k_win[0].reshape(nseq, wb, KV_DIM)
    vbuf = cache_v_win[0].reshape(nseq, wb, KV_DIM)
    yb_s, nk_s, nv_s = _attn_sample(
        q_s.reshape(nseq, tq, d), k_s.reshape(nseq, tq, KV_DIM), v_s.reshape(nseq, tq, KV_DIM),
        kbuf, vbuf, sk, bs=SUBLANES)
    yb_s = yb_s.reshape(ts, d)

    wa = w_branch_a[0].astype(BF16)
    wbb = w_branch_b[0].reshape(N_KV_HEADS, GROUP, HEAD_DIM, d).transpose(1, 0, 2, 3)
    wbb = wbb.reshape(N_HEADS * HEAD_DIM, d).astype(BF16)
    wo = w_out[0].astype(BF16)
    bo = b_out[0].reshape(1, d)
    nffn = norm_ffn[0].reshape(1, d)
    wr0 = w_router[0].astype(BF16)
    wr1 = (w_router[0] - wr0.astype(F32)).astype(BF16)
    wr2 = (w_router[0] - wr0.astype(F32) - wr1.astype(F32)).astype(BF16)
    wr = jnp.concatenate([wr0, wr1, wr2], axis=1)
    br = b_router[0].reshape(1, -1)
    n_tiles_p = tp // tm_s
    pp_spec = lambda chunk: pl.BlockSpec(
        (None, 1, d),
        lambda i: ((jnp.minimum(i, n_tiles_p - 1) // (seq // tm_s)) * nmod + chunk, 0, 0))
    sp_spec = lambda chunk: pl.BlockSpec((tm_s, d), lambda i: (jnp.maximum(i - n_tiles_p, 0), chunk))
    h_all, hm_all, gm_all, km_all, counts = _post(
        (xp, ya_p, yb_p, ga_p, gb_p, mod_p, mod_p, mod_p), (pp_spec(2), pp_spec(3), pp_spec(4)),
        (xs_, ya_s, yb_s, ga_s, gb_s, mod_s, mod_s, mod_s), (sp_spec(2), sp_spec(3), sp_spec(4)),
        wa, wbb, wo, bo, nffn, wr, br, tm=tm_s)

    block = EXPERT_BLOCK
    dest = _route(km_all, counts, tm=tm_s, block=block)
    n_assign = t_all * TOP_K
    n_blocks = -(-(n_assign + N_EXPERTS * (block - 1)) // block)
    cnt = counts[0].astype(jnp.int32)
    pend = jnp.cumsum((cnt + block - 1) // block * block)
    n_used = (pend[-1] // block).astype(jnp.int32)
    blk_ids = jnp.minimum(jnp.arange(n_blocks, dtype=jnp.int32), n_used - 1)
    block_e = jnp.sum((pend[None, :] <= (blk_ids * block)[:, None]).astype(jnp.int32), axis=1)
    block_e = jnp.minimum(block_e, N_EXPERTS - 1)
    nonempty = cnt > 0
    ids = jnp.arange(N_EXPERTS, dtype=jnp.int32)
    later = jnp.where(nonempty[None, :] & (ids[None, :] > ids[:, None]), ids[None, :], N_EXPERTS)
    next_of = jnp.min(later, axis=1)
    next_of = jnp.where(next_of == N_EXPERTS, -1, next_of)
    slot_of = (jnp.cumsum(nonempty.astype(jnp.int32)) - 1) % 2
    row_end_of = pend - (cnt + block - 1) // block * block + cnt
    per_expert = jnp.stack([row_end_of, next_of, slot_of]).astype(jnp.int32)
    onehot = (block_e[None, :, None] == ids[None, None, :]).astype(jnp.int32)
    row_end, next_e, slot = jnp.sum(onehot * per_expert[:, None, :], axis=2)
    n_valid = jnp.clip(row_end - blk_ids * block, 0, block).astype(jnp.int32)

    dest = dest.reshape(TOP_K, t_all // ROW_DMA_TILE, ROW_DMA_TILE).transpose(1, 0, 2).reshape(-1)
    xs_rows = _dispatch(pend.astype(jnp.int32), dest, hm_all, tm=ROW_DMA_TILE,
                        n_slots=n_blocks * block, block=block)
    f2 = w_up.shape[3]
    bu = b_up[0].reshape(N_EXPERTS, f2 // PAIR_GROUP, LANES, 2).transpose(0, 1, 3, 2)
    bu = bu.reshape(N_EXPERTS, 1, f2)
    bd = b_down[0].reshape(N_EXPERTS, 1, d)
    ys = _experts(block_e, n_used.reshape(1), n_valid, next_e, slot, xs_rows, w_up[0], bu,
                  w_down[0], bd, block=block)

    nfin = norm_final.reshape(1, d)
    step_rows = 2 * ROW_DMA_TILE
    steps_p = tp // step_rows
    pf_spec = pl.BlockSpec((None, 1, d), lambda i: ((i // (seq // step_rows)) * nmod + 5, 0, 0))
    y_p = _final(dest, h_all, mod_p, pf_spec, gm_all, km_all, nfin, ys,
                 tm=ROW_DMA_TILE, step0=0, n_steps=steps_p)
    sf_spec = pl.BlockSpec((step_rows, d), lambda i: (i, 5))
    y_s = _final(dest, h_all, mod_s, sf_spec, gm_all, km_all, nfin, ys,
                 tm=ROW_DMA_TILE, step0=steps_p, n_steps=ts // step_rows)

    keep = min(WINDOW, seq)
    k4 = k_p.reshape(batch, seq, N_KV_HEADS, HEAD_DIM)[:, seq - keep:]
    v4 = v_p.reshape(batch, seq, N_KV_HEADS, HEAD_DIM)[:, seq - keep:]
    conv_p = ut_p.reshape(batch, tiles_per_seq, SUBLANES, d)[:, -1, SUBLANES - (CONV_WIDTH - 1):]
    conv_s = u_s.reshape(nseq, tq, d)[:, tq - (CONV_WIDTH - 1):]
    return (y_p.reshape(batch, seq, d), y_s.reshape(nseq, tq, d),
            k4[None], v4[None], conv_p[None],
            nk_s.reshape(1, nseq, wb, N_KV_HEADS, HEAD_DIM),
            nv_s.reshape(1, nseq, wb, N_KV_HEADS, HEAD_DIM), conv_s[None])
```
